```python
import jax, jax.numpy as jnp
from jax import lax
import numpy as np

D_MODEL = 4096
BATCH = 2
SEQ = 8192
DEPTH = 1

MIX_WIDTH = D_MODEL
FOURIER_WIDTH = D_MODEL // 2
FOURIER_GROUPS = 8
FOURIER_GROUP_DIM = FOURIER_WIDTH // FOURIER_GROUPS
MLA_HEADS = 16
MLA_NOPE_DIM = 128
MLA_ROPE_DIM = 64
MLA_V_DIM = 128
MLA_WIDTH = MLA_HEADS * MLA_V_DIM
Q_LORA_RANK = 1024
KV_LORA_RANK = 512
IN_PROJ_WIDTH = FOURIER_WIDTH + Q_LORA_RANK + KV_LORA_RANK + MLA_ROPE_DIM
ROPE_THETA = 10000.0
Q_BLOCK = 128
MEM_TOKENS = 256
XATTN_HEADS = 4
XATTN_HEAD_DIM = D_MODEL // XATTN_HEADS
D_FF = 4 * D_MODEL
NORM_EPS = 1e-6

kernel_name = "hybrid_fnet_mla_memxattn_encoder"


def rms_norm(x, g):
    xf = x.astype(jnp.float32)
    y = xf * lax.rsqrt(jnp.mean(xf * xf, axis=-1, keepdims=True) + NORM_EPS)
    return (y * g.astype(jnp.float32)).astype(x.dtype)


def apply_rope(t, cos, sin):
    half = MLA_ROPE_DIM // 2
    tf = t.astype(jnp.float32)
    t1, t2 = tf[..., :half], tf[..., half:]
    return jnp.concatenate([t1 * cos - t2 * sin, t1 * sin + t2 * cos], axis=-1).astype(t.dtype)


def fourier_mix(z_f, w_fourier):
    b, s, _ = z_f.shape
    zg = z_f.reshape(b, s, FOURIER_GROUPS, FOURIER_GROUP_DIM).astype(jnp.float32)
    zg = jnp.transpose(zg, (0, 2, 1, 3))
    f = jnp.fft.fft2(zg, axes=(-2, -1), norm="ortho").real
    y = jnp.einsum('bgsc,gcd->bsgd', f.astype(z_f.dtype), w_fourier)
    return y.reshape(b, s, FOURIER_WIDTH)


def mla_attention(c_q, c_kv, k_rope, g_q_lora, g_kv_lora, w_uq, w_ukv, cos, sin):
    b, s, _ = c_q.shape
    q = jnp.einsum('bsr,rhd->bshd', rms_norm(c_q, g_q_lora), w_uq)
    q_nope = q[..., :MLA_NOPE_DIM]
    q_rope = apply_rope(q[..., MLA_NOPE_DIM:], cos[:, :, None, :], sin[:, :, None, :])
    kv = jnp.einsum('bsr,rhd->bshd', rms_norm(c_kv, g_kv_lora), w_ukv)
    k_nope = kv[..., :MLA_NOPE_DIM]
    v = kv[..., MLA_NOPE_DIM:]
    k_r = apply_rope(k_rope, cos, sin)
    scale = (MLA_NOPE_DIM + MLA_ROPE_DIM) ** -0.5
    nblk = s // Q_BLOCK
    qn = q_nope.reshape(b, nblk, Q_BLOCK, MLA_HEADS, MLA_NOPE_DIM).swapaxes(0, 1)
    qr = q_rope.reshape(b, nblk, Q_BLOCK, MLA_HEADS, MLA_ROPE_DIM).swapaxes(0, 1)

    def query_block(args):
        qn_b, qr_b = args
        sc = (jnp.einsum('bqhd,bkhd->bhqk', qn_b, k_nope, preferred_element_type=jnp.float32)
              + jnp.einsum('bqhr,bkr->bhqk', qr_b, k_r, preferred_element_type=jnp.float32)) * scale
        p = jax.nn.softmax(sc, axis=-1).astype(v.dtype)
        return jnp.einsum('bhqk,bkhd->bqhd', p, v)

    o = lax.map(query_block, (qn, qr))
    return o.swapaxes(0, 1).reshape(b, s, MLA_WIDTH)


def memory_cross_attention(hn, mem_n, w_xq, w_xk, w_xv, w_xo):
    q = jnp.einsum('bsd,dhe->bshe', hn, w_xq)
    k = jnp.einsum('bmd,dhe->bmhe', mem_n, w_xk)
    v = jnp.einsum('bmd,dhe->bmhe', mem_n, w_xv)
    sc = jnp.einsum('bshe,bmhe->bhsm', q, k, preferred_element_type=jnp.float32) * (XATTN_HEAD_DIM ** -0.5)
    p = jax.nn.softmax(sc, axis=-1).astype(v.dtype)
    o = jnp.einsum('bhsm,bmhe->bshe', p, v)
    return jnp.einsum('bshe,hed->bsd', o, w_xo)


def setup_inputs(seed: int = 0) -> dict:
    key = jax.random.key(seed)
    ks = jax.random.split(key, 24)
    f32 = jnp.float32

    def nrm(k, shape, fan_in):
        return jax.random.normal(k, shape, f32) * (fan_in ** -0.5)

    def gain(k, shape):
        return 1.0 + 0.02 * jax.random.normal(k, shape, f32)

    L = DEPTH
    x = jax.random.normal(ks[0], (BATCH, SEQ, D_MODEL), f32)
    mem = jax.random.normal(ks[1], (BATCH, MEM_TOKENS, D_MODEL), f32)
    offsets = jax.random.randint(ks[2], (BATCH, 1), 0, SEQ, dtype=jnp.int32)
    positions = (jnp.arange(SEQ, dtype=jnp.int32)[None, :] + offsets).astype(jnp.int32)
    return {
        "x": x,
        "mem": mem,
        "positions": positions,
        "g_mix": gain(ks[3], (L, D_MODEL)),
        "w_in": nrm(ks[4], (L, D_MODEL, IN_PROJ_WIDTH), D_MODEL),
        "w_fourier": nrm(ks[5], (L, FOURIER_GROUPS, FOURIER_GROUP_DIM, FOURIER_GROUP_DIM), FOURIER_GROUP_DIM),
        "g_q_lora": gain(ks[6], (L, Q_LORA_RANK)),
        "w_uq": nrm(ks[7], (L, Q_LORA_RANK, MLA_HEADS, MLA_NOPE_DIM + MLA_ROPE_DIM), Q_LORA_RANK),
        "g_kv_lora": gain(ks[8], (L, KV_LORA_RANK)),
        "w_ukv": nrm(ks[9], (L, KV_LORA_RANK, MLA_HEADS, MLA_NOPE_DIM + MLA_V_DIM), KV_LORA_RANK),
        "g_fourier_out": gain(ks[10], (L, FOURIER_WIDTH)),
        "g_mla_out": gain(ks[11], (L, MLA_WIDTH)),
        "w_out": nrm(ks[12], (L, MIX_WIDTH, D_MODEL), MIX_WIDTH),
        "g_xattn": gain(ks[13], (L, D_MODEL)),
        "g_mem": gain(ks[14], (L, D_MODEL)),
        "w_xq": nrm(ks[15], (L, D_MODEL, XATTN_HEADS, XATTN_HEAD_DIM), D_MODEL),
        "w_xk": nrm(ks[16], (L, D_MODEL, XATTN_HEADS, XATTN_HEAD_DIM), D_MODEL),
        "w_xv": nrm(ks[17], (L, D_MODEL, XATTN_HEADS, XATTN_HEAD_DIM), D_MODEL),
        "w_xo": nrm(ks[18], (L, XATTN_HEADS, XATTN_HEAD_DIM, D_MODEL), D_MODEL),
        "g_mlp": gain(ks[19], (L, D_MODEL)),
        "w_ff1": nrm(ks[20], (L, D_MODEL, D_FF), D_MODEL),
        "w_ff2": nrm(ks[21], (L, D_FF, D_MODEL), D_FF),
        "g_final": gain(ks[22], (D_MODEL,)),
    }


def reference(x, mem, positions, g_mix, w_in, w_fourier, g_q_lora, w_uq, g_kv_lora, w_ukv,
              g_fourier_out, g_mla_out, w_out, g_xattn, g_mem, w_xq, w_xk, w_xv, w_xo,
              g_mlp, w_ff1, w_ff2, g_final):
    half = MLA_ROPE_DIM // 2
    inv_freq = ROPE_THETA ** (-jnp.arange(half, dtype=jnp.float32) / half)
    ang = positions.astype(jnp.float32)[..., None] * inv_freq
    cos, sin = jnp.cos(ang), jnp.sin(ang)
    split_pts = [FOURIER_WIDTH, FOURIER_WIDTH + Q_LORA_RANK, FOURIER_WIDTH + Q_LORA_RANK + KV_LORA_RANK]

    h = x
    for layer in range(DEPTH):
        u = rms_norm(h, g_mix[layer])
        z = jnp.einsum('bsd,de->bse', u, w_in[layer])
        z_f, c_q, c_kv, k_rope = jnp.split(z, split_pts, axis=-1)
        y_f = rms_norm(fourier_mix(z_f, w_fourier[layer]), g_fourier_out[layer])
        y_a = rms_norm(mla_attention(c_q, c_kv, k_rope, g_q_lora[layer], g_kv_lora[layer],
                                     w_uq[layer], w_ukv[layer], cos, sin), g_mla_out[layer])
        y = jnp.concatenate([y_f, y_a], axis=-1)
        h = h + jnp.einsum('bse,ed->bsd', y, w_out[layer])
        mem_n = rms_norm(mem, g_mem[layer])
        h = h + memory_cross_attention(rms_norm(h, g_xattn[layer]), mem_n,
                                       w_xq[layer], w_xk[layer], w_xv[layer], w_xo[layer])
        hn = rms_norm(h, g_mlp[layer])
        a = jnp.square(jax.nn.relu(jnp.einsum('bsd,df->bsf', hn, w_ff1[layer])))
        h = h + jnp.einsum('bsf,fd->bsd', a, w_ff2[layer])
    return rms_norm(h, g_final)
```

```python
import functools
import math

import numpy as np
import jax
import jax.numpy as jnp
from jax import lax
from jax.experimental import pallas as pl
from jax.experimental.pallas import tpu as pltpu

F32 = jnp.float32
BF16 = jnp.bfloat16

NORM_EPS = 1e-6
ROPE_THETA = 10000.0
LANES = 128
VMEM_LIMIT_BYTES = 56 * 1024 * 1024
FOURIER_SEQ_INNER = 128


def _params(*sem):
    return pltpu.CompilerParams(dimension_semantics=sem, vmem_limit_bytes=VMEM_LIMIT_BYTES)


def _rms(xf, g):
    ms = jnp.mean(xf * xf, axis=-1, keepdims=True)
    return xf * lax.rsqrt(ms + NORM_EPS) * g


def _rms_rows(a_ref, g_ref, o_ref, chunk=64):
    rows = a_ref.shape[0]
    chunk = min(chunk, rows)

    def body(c, carry):
        r = pl.ds(pl.multiple_of(c * chunk, chunk), chunk)
        o_ref[r, :] = _rms(a_ref[r, :].astype(F32), g_ref[...]).astype(o_ref.dtype)
        return carry

    lax.fori_loop(0, rows // chunk, body, 0)


def _tile(dim, pref):
    t = min(dim, pref)
    assert dim % t == 0, (dim, pref)
    return t


def _norm_mm_kernel(a_ref, g_ref, w_ref, *rest, act, has_res):
    if has_res:
        res_ref, o_ref, xn_ref = rest
    else:
        o_ref, xn_ref = rest

    @pl.when(pl.program_id(1) == 0)
    def _():
        _rms_rows(a_ref, g_ref, xn_ref)

    acc = jnp.dot(xn_ref[...], w_ref[...], preferred_element_type=F32)
    if act == "relu2":
        acc = jnp.square(jnp.maximum(acc, 0.0))
    if has_res:
        acc = acc + res_ref[...]
    o_ref[...] = acc.astype(o_ref.dtype)


def norm_mm(a, g, w, *, tm, tn, out_dtype, act=None, res=None):
    m, k = a.shape
    n = w.shape[1]
    tm, tn = _tile(m, tm), _tile(n, tn)
    in_specs = [
        pl.BlockSpec((tm, k), lambda i, j: (i, 0)),
        pl.BlockSpec((1, k), lambda i, j: (0, 0)),
        pl.BlockSpec((k, tn), lambda i, j: (0, j)),
    ]
    args = [a, g.reshape(1, k).astype(F32), w]
    if res is not None:
        in_specs.append(pl.BlockSpec((tm, tn), lambda i, j: (i, j)))
        args.append(res)
    return pl.pallas_call(
        functools.partial(_norm_mm_kernel, act=act, has_res=res is not None),
        grid=(m // tm, n // tn),
        in_specs=in_specs,
        out_specs=pl.BlockSpec((tm, tn), lambda i, j: (i, j)),
        out_shape=jax.ShapeDtypeStruct((m, n), out_dtype),
        scratch_shapes=[pltpu.VMEM((tm, k), BF16)],
        compiler_params=_params("parallel", "arbitrary"),
        name="norm_mm",
    )(*args)


def _mm_res_kernel(a_ref, w_ref, res_ref, o_ref, acc_ref):
    kk = pl.program_id(2)

    @pl.when(kk == 0)
    def _():
        acc_ref[...] = res_ref[...].astype(F32)

    acc_ref[...] += jnp.dot(a_ref[...], w_ref[...], preferred_element_type=F32)

    @pl.when(kk == pl.num_programs(2) - 1)
    def _():
        o_ref[...] = acc_ref[...].astype(o_ref.dtype)


def mm_res(a, w, res, *, tm, tn, tk, out_dtype):
    m, k = a.shape
    n = w.shape[1]
    tm, tn, tk = _tile(m, tm), _tile(n, tn), _tile(k, tk)
    return pl.pallas_call(
        _mm_res_kernel,
        grid=(m // tm, n // tn, k // tk),
        in_specs=[
            pl.BlockSpec((tm, tk), lambda i, j, kk: (i, kk)),
            pl.BlockSpec((tk, tn), lambda i, j, kk: (kk, j)),
            pl.BlockSpec((tm, tn), lambda i, j, kk: (i, j)),
        ],
        out_specs=pl.BlockSpec((tm, tn), lambda i, j, kk: (i, j)),
        out_shape=jax.ShapeDtypeStruct((m, n), out_dtype),
        scratch_shapes=[pltpu.VMEM((tm, tn), F32)],
        compiler_params=_params("parallel", "parallel", "arbitrary"),
        name="mm_res",
    )(a, w, res)


def _rmsnorm_kernel(a_ref, g_ref, o_ref):
    _rms_rows(a_ref, g_ref, o_ref)


def rmsnorm(a, g, *, tm, out_dtype):
    m, k = a.shape
    tm = _tile(m, tm)
    return pl.pallas_call(
        _rmsnorm_kernel,
        grid=(m // tm,),
        in_specs=[pl.BlockSpec((tm, k), lambda i: (i, 0)), pl.BlockSpec((1, k), lambda i: (0, 0))],
        out_specs=pl.BlockSpec((tm, k), lambda i: (i, 0)),
        out_shape=jax.ShapeDtypeStruct((m, k), out_dtype),
        compiler_params=_params("parallel"),
        name="rmsnorm",
    )(a, g.reshape(1, k).astype(F32))


def _chan_dft_kernel(z_ref, cs_ref, p_ref, q_ref, *, groups, gdim):
    for g in range(groups):
        pq = jnp.dot(z_ref[:, g * gdim:(g + 1) * gdim], cs_ref[...], preferred_element_type=F32)
        p_ref[:, g * gdim:(g + 1) * gdim] = pq[:, :gdim].astype(p_ref.dtype)
        q_ref[:, g * gdim:(g + 1) * gdim] = pq[:, gdim:].astype(q_ref.dtype)


def chan_dft(z, cs, *, width, groups, tm):
    m = z.shape[0]
    gdim = width // groups
    tm = _tile(m, tm)
    out = jax.ShapeDtypeStruct((m, width), BF16)
    return pl.pallas_call(
        functools.partial(_chan_dft_kernel, groups=groups, gdim=gdim),
        grid=(m // tm,),
        in_specs=[pl.BlockSpec((tm, width), lambda i: (i, 0)),
                  pl.BlockSpec((gdim, 2 * gdim), lambda i: (0, 0))],
        out_specs=[pl.BlockSpec((tm, width), lambda i: (i, 0))] * 2,
        out_shape=[out, out],
        compiler_params=_params("parallel"),
        name="chan_dft",
    )(z, cs)


def _seq_dft1_kernel(p_ref, q_ref, tp_ref, tq_ref, o_ref, *, nb, width):
    for t in range(nb):
        sl = slice(t * width, (t + 1) * width)
        acc = jnp.dot(tp_ref[t], p_ref[0, :, sl], preferred_element_type=F32)
        acc += jnp.dot(tq_ref[t], q_ref[0, :, sl], preferred_element_type=F32)
        o_ref[0, :, sl] = acc.astype(o_ref.dtype)


def seq_dft1(p3, q3, tp, tq, *, width, nb):
    b, s1, cols = p3.shape
    s2 = cols // width
    nb = _tile(s2, nb)
    dspec = pl.BlockSpec((1, s1, nb * width), lambda bi, j: (bi, 0, j))
    tspec = pl.BlockSpec((nb, 2 * s1, s1), lambda bi, j: (j, 0, 0))
    return pl.pallas_call(
        functools.partial(_seq_dft1_kernel, nb=nb, width=width),
        grid=(b, s2 // nb),
        in_specs=[dspec, dspec, tspec, tspec],
        out_specs=pl.BlockSpec((1, 2 * s1, nb * width), lambda bi, j: (bi, 0, j)),
        out_shape=jax.ShapeDtypeStruct((b, 2 * s1, cols), BF16),
        compiler_params=_params("parallel", "parallel"),
        name="seq_dft1",
    )(p3, q3, tp, tq)


def _seq_dft2_kernel(re_ref, im_ref, c_ref, s_ref, o_ref, *, kb, width):
    for t in range(kb):
        acc = jnp.dot(c_ref[...], re_ref[0, t], preferred_element_type=F32)
        acc += jnp.dot(s_ref[...], im_ref[0, t], preferred_element_type=F32)
        o_ref[0, :, t * width:(t + 1) * width] = acc.astype(o_ref.dtype)


def seq_dft2(a4, c2, s2m, *, s1, kb):
    b, _, s2, width = a4.shape
    kb = _tile(s1, kb)
    nblk = s1 // kb
    return pl.pallas_call(
        functools.partial(_seq_dft2_kernel, kb=kb, width=width),
        grid=(b, nblk),
        in_specs=[
            pl.BlockSpec((1, kb, s2, width), lambda bi, j: (bi, j, 0, 0)),
            pl.BlockSpec((1, kb, s2, width), lambda bi, j: (bi, j + nblk, 0, 0)),
            pl.BlockSpec((s2, s2), lambda bi, j: (0, 0)),
            pl.BlockSpec((s2, s2), lambda bi, j: (0, 0)),
        ],
        out_specs=pl.BlockSpec((1, s2, kb * width), lambda bi, j: (bi, 0, j)),
        out_shape=jax.ShapeDtypeStruct((b, s2, s1 * width), BF16),
        compiler_params=_params("parallel", "parallel"),
        name="seq_dft2",
    )(a4, a4, c2, s2m)


def _dft_tables(seq, gdim):
    s2 = FOURIER_SEQ_INNER
    s1 = seq // s2
    c = np.arange(gdim)
    ang = 2.0 * np.pi * ((c[:, None] * c[None, :]) % gdim) / gdim
    cs = np.concatenate([np.cos(ang), np.sin(ang)], axis=1) / math.sqrt(gdim)
    k1 = np.arange(s1)
    n1 = np.arange(s1)
    n2 = np.arange(s2)
    idx = (k1[None, :, None] * (s2 * n1[None, None, :] + n2[:, None, None])) % seq
    phi = 2.0 * np.pi * idx / seq
    co, si = np.cos(phi), np.sin(phi)
    tp = np.concatenate([co, -si], axis=1)
    tq = np.concatenate([-si, -co], axis=1)
    k2 = np.arange(s2)
    ang2 = 2.0 * np.pi * ((k2[:, None] * n2[None, :]) % s2) / s2
    c2 = np.cos(ang2) / math.sqrt(seq)
    s2m = np.sin(ang2) / math.sqrt(seq)
    as_bf16 = lambda x: jnp.asarray(x, dtype=F32).astype(BF16)
    return as_bf16(cs), as_bf16(tp), as_bf16(tq), as_bf16(c2), as_bf16(s2m)


def _rope128(r, ct, sa, sb):
    quarter = LANES // 4
    return r * ct + pltpu.roll(r, quarter, 1) * sa + pltpu.roll(r, LANES - quarter, 1) * sb


def _q_proj_kernel(c_ref, g_ref, w_ref, ct_ref, sa_ref, sb_ref, o_ref, *, heads, scale):
    xn = _rms(c_ref[...].astype(F32), g_ref[...]).astype(BF16)
    acc = jnp.dot(xn, w_ref[...], preferred_element_type=F32)
    ct, sa, sb = ct_ref[...], sa_ref[...], sb_ref[...]
    for h in range(heads):
        base = 2 * LANES * h
        o_ref[:, base:base + LANES] = (acc[:, base:base + LANES] * scale).astype(o_ref.dtype)
        r = _rope128(acc[:, base + LANES:base + 2 * LANES], ct, sa, sb)
        o_ref[:, base + LANES:base + 2 * LANES] = (r * scale).astype(o_ref.dtype)


def q_proj(z, g, w, ct, sa, sb, *, col_block, rank, heads, scale, tm):
    m = z.shape[0]
    tm = _tile(m, tm)
    n = w.shape[1]
    tab = pl.BlockSpec((tm, LANES), lambda i: (i, 0))
    return pl.pallas_call(
        functools.partial(_q_proj_kernel, heads=heads, scale=scale),
        grid=(m // tm,),
        in_specs=[pl.BlockSpec((tm, rank), lambda i: (i, col_block)),
                  pl.BlockSpec((1, rank), lambda i: (0, 0)),
                  pl.BlockSpec((rank, n), lambda i: (0, 0)),
                  tab, tab, tab],
        out_specs=pl.BlockSpec((tm, n), lambda i: (i, 0)),
        out_shape=jax.ShapeDtypeStruct((m, n), BF16),
        compiler_params=_params("parallel"),
        name="q_proj",
    )(z, g.reshape(1, rank).astype(F32), w, ct, sa, sb)


def _kv_proj_kernel(c_ref, kr_ref, g_ref, w_ref, ct_ref, sa_ref, sb_ref, k_ref, v_ref, *, heads):
    xn = _rms(c_ref[...].astype(F32), g_ref[...]).astype(BF16)
    acc = jnp.dot(xn, w_ref[...], preferred_element_type=F32)
    kr = _rope128(kr_ref[...].astype(F32), ct_ref[...], sa_ref[...], sb_ref[...]).astype(k_ref.dtype)
    for h in range(heads):
        k_ref[:, 2 * LANES * h:2 * LANES * h + LANES] = acc[:, LANES * h:LANES * (h + 1)].astype(k_ref.dtype)
        k_ref[:, 2 * LANES * h + LANES:2 * LANES * (h + 1)] = kr
    v_ref[...] = acc[:, LANES * heads:].astype(v_ref.dtype)


def kv_proj(z, g, w, ct, sa, sb, *, ckv_block, kr_block, rank, heads, tm):
    m = z.shape[0]
    tm = _tile(m, tm)
    tab = pl.BlockSpec((tm, LANES), lambda i: (i, 0))
    return pl.pallas_call(
        functools.partial(_kv_proj_kernel, heads=heads),
        grid=(m // tm,),
        in_specs=[pl.BlockSpec((tm, rank), lambda i: (i, ckv_block)),
                  pl.BlockSpec((tm, LANES), lambda i: (i, kr_block)),
                  pl.BlockSpec((1, rank), lambda i: (0, 0)),
                  pl.BlockSpec((rank, 2 * LANES * heads), lambda i: (0, 0)),
                  tab, tab, tab],
        out_specs=[pl.BlockSpec((tm, 2 * LANES * heads), lambda i: (i, 0)),
                   pl.BlockSpec((tm, LANES * heads), lambda i: (i, 0))],
        out_shape=[jax.ShapeDtypeStruct((m, 2 * LANES * heads), BF16),
                   jax.ShapeDtypeStruct((m, LANES * heads), BF16)],
        compiler_params=_params("parallel"),
        name="kv_proj",
    )(z, z, g.reshape(1, rank).astype(F32), w, ct, sa, sb)


def _flash_kernel(q_ref, k_ref, v_ref, o_ref, *, tk, nk):
    q = q_ref[0]
    tq = q.shape[0]
    dv = v_ref.shape[-1]

    def body(c, carry):
        m, l, acc = carry
        start = pl.multiple_of(c * tk, tk)
        k = k_ref[0, pl.ds(start, tk), :]
        v = v_ref[0, pl.ds(start, tk), :]
        s = lax.dot_general(q, k, (((1,), (1,)), ((), ())), preferred_element_type=F32)
        m_new = jnp.maximum(m, jnp.max(s, axis=1, keepdims=True))
        alpha = jnp.exp(m - m_new)
        p = jnp.exp(s - m_new)
        l = alpha * l + jnp.sum(p, axis=1, keepdims=True)
        acc = alpha * acc + jnp.dot(p.astype(v.dtype), v, preferred_element_type=F32)
        return m_new, l, acc

    init = (jnp.full((tq, 1), -jnp.inf, F32), jnp.zeros((tq, 1), F32), jnp.zeros((tq, dv), F32))
    _, l, acc = lax.fori_loop(0, nk, body, init)
    o_ref[0] = (acc / l).astype(o_ref.dtype)


def flash_attention(q, k, v, *, heads, tq, tk):
    b, s, _ = q.shape
    dk = q.shape[-1] // heads
    dv = v.shape[-1] // heads
    tq, tk = _tile(s, tq), _tile(s, tk)
    return pl.pallas_call(
        functools.partial(_flash_kernel, tk=tk, nk=s // tk),
        grid=(b, heads, s // tq),
        in_specs=[pl.BlockSpec((1, tq, dk), lambda bi, h, i: (bi, i, h)),
                  pl.BlockSpec((1, s, dk), lambda bi, h, i: (bi, 0, h)),
                  pl.BlockSpec((1, s, dv), lambda bi, h, i: (bi, 0, h))],
        out_specs=pl.BlockSpec((1, tq, dv), lambda bi, h, i: (bi, i, h)),
        out_shape=jax.ShapeDtypeStruct((b, s, heads * dv), BF16),
        compiler_params=_params("parallel", "parallel", "arbitrary"),
        name="flash_attention",
    )(q, k, v)


def _out_proj_kernel(f_ref, wf_ref, gf_ref, a_ref, ga_ref, w_ref, x_ref, o_ref, y_ref, *, groups, gdim):
    fw = groups * gdim

    @pl.when(pl.program_id(1) == 0)
    def _():
        parts = [jnp.dot(f_ref[:, g * gdim:(g + 1) * gdim], wf_ref[g], preferred_element_type=F32)
                 for g in range(groups)]
        ms = sum(jnp.sum(p * p, axis=-1, keepdims=True) for p in parts) / fw
        inv = lax.rsqrt(ms + NORM_EPS)
        for g in range(groups):
            sl = slice(g * gdim, (g + 1) * gdim)
            y_ref[:, sl] = (parts[g] * inv * gf_ref[:, sl]).astype(BF16)
        _rms_rows(a_ref, ga_ref, y_ref.at[:, fw:])

    acc = jnp.dot(y_ref[...], w_ref[...], preferred_element_type=F32)
    o_ref[...] = (acc + x_ref[...]).astype(o_ref.dtype)


def out_proj(f, wf, gf, a, ga, w, x, *, tm, tn):
    m, fw = f.shape
    aw = a.shape[1]
    groups, gdim, _ = wf.shape
    n = w.shape[1]
    tm, tn = _tile(m, tm), _tile(n, tn)
    return pl.pallas_call(
        functools.partial(_out_proj_kernel, groups=groups, gdim=gdim),
        grid=(m // tm, n // tn),
        in_specs=[pl.BlockSpec((tm, fw), lambda i, j: (i, 0)),
                  pl.BlockSpec((groups, gdim, gdim), lambda i, j: (0, 0, 0)),
                  pl.BlockSpec((1, fw), lambda i, j: (0, 0)),
                  pl.BlockSpec((tm, aw), lambda i, j: (i, 0)),
                  pl.BlockSpec((1, aw), lambda i, j: (0, 0)),
                  pl.BlockSpec((fw + aw, tn), lambda i, j: (0, j)),
                  pl.BlockSpec((tm, tn), lambda i, j: (i, j))],
        out_specs=pl.BlockSpec((tm, tn), lambda i, j: (i, j)),
        out_shape=jax.ShapeDtypeStruct((m, n), F32),
        scratch_shapes=[pltpu.VMEM((tm, fw + aw), BF16)],
        compiler_params=_params("parallel", "arbitrary"),
        name="out_proj",
    )(f, wf, gf.reshape(1, fw).astype(F32), a, ga.reshape(1, aw).astype(F32), w, x)


def _absorb_qk_kernel(wq_ref, k_ref, o_ref, *, scale):
    acc = lax.dot_general(wq_ref[...], k_ref[...], (((1,), (1,)), ((), ())), preferred_element_type=F32)
    o_ref[0] = (acc * scale).astype(o_ref.dtype)


def absorb_qk(wq, kv, *, batch, heads, mem, hd, scale):
    d = wq.shape[0]
    return pl.pallas_call(
        functools.partial(_absorb_qk_kernel, scale=scale),
        grid=(heads, batch),
        in_specs=[pl.BlockSpec((d, hd), lambda h, b: (0, h)),
                  pl.BlockSpec((mem, hd), lambda h, b: (b, h))],
        out_specs=pl.BlockSpec((1, d, mem), lambda h, b: (b, 0, h)),
        out_shape=jax.ShapeDtypeStruct((batch, d, heads * mem), BF16),
        compiler_params=_params("parallel", "parallel"),
        name="absorb_qk",
    )(wq, kv)


def _absorb_vo_kernel(v_ref, wo_ref, o_ref):
    o_ref[0] = jnp.dot(v_ref[...], wo_ref[...], preferred_element_type=F32).astype(o_ref.dtype)


def absorb_vo(kv, wo, *, batch, heads, mem, hd, tn):
    d = wo.shape[1]
    tn = _tile(d, tn)
    return pl.pallas_call(
        _absorb_vo_kernel,
        grid=(heads, d // tn, batch),
        in_specs=[pl.BlockSpec((mem, hd), lambda h, j, b: (b, heads + h)),
                  pl.BlockSpec((hd, tn), lambda h, j, b: (h, j))],
        out_specs=pl.BlockSpec((1, mem, tn), lambda h, j, b: (b, h, j)),
        out_shape=jax.ShapeDtypeStruct((batch, heads * mem, d), BF16),
        compiler_params=_params("parallel", "parallel", "parallel"),
        name="absorb_vo",
    )(kv, wo)


def _xattn_kernel(hf_ref, g_ref, wqk_ref, vo_ref, hres_ref, o_ref, p_ref, hn_ref, *, heads, mem):
    @pl.when(pl.program_id(1) == 0)
    def _():
        _rms_rows(hf_ref, g_ref, hn_ref)
        s = jnp.dot(hn_ref[...], wqk_ref[0], preferred_element_type=F32)
        for h in range(heads):
            seg = s[:, h * mem:(h + 1) * mem]
            e = jnp.exp(seg - jnp.max(seg, axis=-1, keepdims=True))
            p_ref[:, h * mem:(h + 1) * mem] = (e / jnp.sum(e, axis=-1, keepdims=True)).astype(BF16)

    acc = jnp.dot(p_ref[...], vo_ref[0], preferred_element_type=F32)
    o_ref[...] = acc + hres_ref[...]


def xattn(h, g, wqk, vo, *, seq, heads, mem, tm, tn):
    m, d = h.shape
    tm, tn = _tile(seq, tm), _tile(d, tn)
    per_b = seq // tm
    hm = heads * mem
    return pl.pallas_call(
        functools.partial(_xattn_kernel, heads=heads, mem=mem),
        grid=(m // tm, d // tn),
        in_specs=[pl.BlockSpec((tm, d), lambda i, j: (i, 0)),
                  pl.BlockSpec((1, d), lambda i, j: (0, 0)),
                  pl.BlockSpec((1, d, hm), lambda i, j: (i // per_b, 0, 0)),
                  pl.BlockSpec((1, hm, tn), lambda i, j: (i // per_b, 0, j)),
                  pl.BlockSpec((tm, tn), lambda i, j: (i, j))],
        out_specs=pl.BlockSpec((tm, tn), lambda i, j: (i, j)),
        out_shape=jax.ShapeDtypeStruct((m, d), F32),
        scratch_shapes=[pltpu.VMEM((tm, hm), BF16), pltpu.VMEM((tm, d), BF16)],
        compiler_params=_params("parallel", "arbitrary"),
        name="xattn",
    )(h, g.reshape(1, d).astype(F32), wqk, vo, h)


def _pad_cols(w, n):
    return jnp.pad(w, ((0, 0), (0, n - w.shape[1])))


def kernel(x, mem, positions, g_mix, w_in, w_fourier, g_q_lora, w_uq, g_kv_lora, w_ukv, g_fourier_out, g_mla_out, w_out, g_xattn, g_mem, w_xq, w_xk, w_xv, w_xo, g_mlp, w_ff1, w_ff2, g_final):
    batch, seq, d = x.shape
    depth = g_mix.shape[0]
    groups, gdim = w_fourier.shape[1], w_fourier.shape[2]
    fw = groups * gdim
    q_rank, heads, qk_dim = w_uq.shape[1], w_uq.shape[2], w_uq.shape[3]
    kv_rank = w_ukv.shape[1]
    nope = LANES
    rope = qk_dim - nope
    vdim = w_ukv.shape[3] - nope
    assert rope == LANES // 2 and vdim == LANES and fw % q_rank == 0
    assert (fw + q_rank) % kv_rank == 0 and (fw + q_rank + kv_rank) % LANES == 0
    xheads, xhd = w_xq.shape[2], w_xq.shape[3]
    mtok = mem.shape[1]
    t = batch * seq
    s2 = FOURIER_SEQ_INNER
    s1 = seq // s2

    half = rope // 2
    inv_freq = ROPE_THETA ** (-jnp.arange(half, dtype=F32) / half)
    ang = positions.astype(F32).reshape(t, 1) * inv_freq
    cos, sin = jnp.cos(ang), jnp.sin(ang)
    zq = jnp.zeros_like(cos)
    ct = jnp.concatenate([cos, cos, zq, zq], axis=1)
    sa = jnp.concatenate([zq, sin, zq, zq], axis=1)
    sb = jnp.concatenate([-sin, zq, zq, zq], axis=1)

    cs, tp, tq_tab, c2, s2m = _dft_tables(seq, gdim)

    in_width = w_in.shape[2]
    z_width = -(-(in_width + rope) // 768) * 768
    scale = float(qk_dim) ** -0.5

    h = x.reshape(t, d)
    mem2 = mem.reshape(batch * mtok, d)
    for layer in range(depth):
        w_in_b = _pad_cols(w_in[layer], z_width).astype(BF16)
        wq = jnp.pad(w_uq[layer], ((0, 0), (0, 0), (0, 2 * LANES - qk_dim))).reshape(q_rank, heads * 2 * LANES).astype(BF16)
        wkv = jnp.concatenate([w_ukv[layer][:, :, :nope].reshape(kv_rank, heads * nope),
                               w_ukv[layer][:, :, nope:].reshape(kv_rank, heads * vdim)], axis=1).astype(BF16)
        wf = w_fourier[layer].astype(BF16)
        w_out_b = w_out[layer].astype(BF16)
        wxq = w_xq[layer].reshape(d, xheads * xhd).astype(BF16)
        wxkv = jnp.concatenate([w_xk[layer].reshape(d, xheads * xhd), w_xv[layer].reshape(d, xheads * xhd)], axis=1).astype(BF16)
        wxo = w_xo[layer].reshape(xheads * xhd, d).astype(BF16)
        w1 = w_ff1[layer].astype(BF16)
        w2 = w_ff2[layer].astype(BF16)

        z = norm_mm(h, g_mix[layer], w_in_b, tm=512, tn=768, out_dtype=BF16)
        p, q = chan_dft(z, cs, width=fw, groups=groups, tm=1024)
        a = seq_dft1(p.reshape(batch, s1, s2 * fw), q.reshape(batch, s1, s2 * fw), tp, tq_tab, width=fw, nb=8)
        f = seq_dft2(a.reshape(batch, 2 * s1, s2, fw), c2, s2m, s1=s1, kb=4).reshape(t, fw)
        qh = q_proj(z, g_q_lora[layer], wq, ct, sa, sb, col_block=fw // q_rank, rank=q_rank,
                    heads=heads, scale=scale, tm=256)
        kh, vh = kv_proj(z, g_kv_lora[layer], wkv, ct, sa, sb, ckv_block=(fw + q_rank) // kv_rank,
                         kr_block=(fw + q_rank + kv_rank) // LANES, rank=kv_rank, heads=heads, tm=512)
        o = flash_attention(qh.reshape(batch, seq, -1), kh.reshape(batch, seq, -1), vh.reshape(batch, seq, -1),
                            heads=heads, tq=512, tk=512).reshape(t, heads * vdim)
        h = out_proj(f, wf, g_fourier_out[layer], o, g_mla_out[layer], w_out_b, h, tm=512, tn=512)

        kvm = norm_mm(mem2, g_mem[layer], wxkv, tm=512, tn=512, out_dtype=BF16)
        wqk = absorb_qk(wxq, kvm, batch=batch, heads=xheads, mem=mtok, hd=xhd, scale=float(xhd) ** -0.5)
        vo = absorb_vo(kvm, wxo, batch=batch, heads=xheads, mem=mtok, hd=xhd, tn=2048)
        h = xattn(h, g_xattn[layer], wqk, vo, seq=seq, heads=xheads, mem=mtok, tm=256, tn=1024)

        act = norm_mm(h, g_mlp[layer], w1, tm=512, tn=1024, out_dtype=BF16, act="relu2")
        h = mm_res(act, w2, h, tm=1024, tn=1024, tk=2048, out_dtype=F32)

    out = rmsnorm(h, g_final, tm=256, out_dtype=x.dtype)
    return out.reshape(batch, seq, d)
```

```python
import functools
import math

import numpy as np
import jax
import jax.numpy as jnp
from jax import lax
from jax.experimental import pallas as pl
from jax.experimental.pallas import tpu as pltpu

F32 = jnp.float32
BF16 = jnp.bfloat16

NORM_EPS = 1e-6
ROPE_THETA = 10000.0
LANES = 128
VMEM_LIMIT_BYTES = 56 * 1024 * 1024
FOURIER_SEQ_INNER = 128


def _params(*sem):
    return pltpu.CompilerParams(dimension_semantics=sem, vmem_limit_bytes=VMEM_LIMIT_BYTES)


def _rms(xf, g):
    ms = jnp.mean(xf * xf, axis=-1, keepdims=True)
    return xf * lax.rsqrt(ms + NORM_EPS) * g


def _rms_rows(a_ref, g_ref, o_ref, chunk=64):
    rows = a_ref.shape[0]
    chunk = min(chunk, rows)

    def body(c, carry):
        r = pl.ds(pl.multiple_of(c * chunk, chunk), chunk)
        o_ref[r, :] = _rms(a_ref[r, :].astype(F32), g_ref[...]).astype(o_ref.dtype)
        return carry

    lax.fori_loop(0, rows // chunk, body, 0)


def _tile(dim, pref):
    t = min(dim, pref)
    assert dim % t == 0, (dim, pref)
    return t


def _norm_mm_kernel(a_ref, g_ref, w_ref, *rest, act, has_res):
    if has_res:
        res_ref, o_ref, xn_ref = rest
    else:
        o_ref, xn_ref = rest

    @pl.when(pl.program_id(1) == 0)
    def _():
        _rms_rows(a_ref, g_ref, xn_ref)

    acc = jnp.dot(xn_ref[...], w_ref[...], preferred_element_type=F32)
    if act == "relu2":
        acc = jnp.square(jnp.maximum(acc, 0.0))
    if has_res:
        acc = acc + res_ref[...]
    o_ref[...] = acc.astype(o_ref.dtype)


def norm_mm(a, g, w, *, tm, tn, out_dtype, act=None, res=None):
    m, k = a.shape
    n = w.shape[1]
    tm, tn = _tile(m, tm), _tile(n, tn)
    in_specs = [
        pl.BlockSpec((tm, k), lambda i, j: (i, 0)),
        pl.BlockSpec((1, k), lambda i, j: (0, 0)),
        pl.BlockSpec((k, tn), lambda i, j: (0, j)),
    ]
    args = [a, g.reshape(1, k).astype(F32), w]
    if res is not None:
        in_specs.append(pl.BlockSpec((tm, tn), lambda i, j: (i, j)))
        args.append(res)
    return pl.pallas_call(
        functools.partial(_norm_mm_kernel, act=act, has_res=res is not None),
        grid=(m // tm, n // tn),
        in_specs=in_specs,
        out_specs=pl.BlockSpec((tm, tn), lambda i, j: (i, j)),
        out_shape=jax.ShapeDtypeStruct((m, n), out_dtype),
        scratch_shapes=[pltpu.VMEM((tm, k), BF16)],
        compiler_params=_params("parallel", "arbitrary"),
        name="norm_mm",
    )(*args)


def _mm_res_kernel(a_ref, w_ref, res_ref, *rest, final_norm):
    if final_norm:
        g_ref, o_ref = rest
    else:
        (o_ref,) = rest
    kk = pl.program_id(2)

    @pl.when(kk == 0)
    def _():
        o_ref[...] = res_ref[...]

    o_ref[...] += jnp.dot(a_ref[...], w_ref[...], preferred_element_type=F32)

    if final_norm:
        @pl.when(kk == pl.num_programs(2) - 1)
        def _():
            _rms_rows(o_ref, g_ref, o_ref)


def mm_res(a, w, res, *, tm, tn, tk, norm_gain=None):
    m, k = a.shape
    n = w.shape[1]
    tm, tn, tk = _tile(m, tm), _tile(n, tn), _tile(k, tk)
    in_specs = [
        pl.BlockSpec((tm, tk), lambda i, j, kk: (i, kk)),
        pl.BlockSpec((tk, tn), lambda i, j, kk: (kk, j)),
        pl.BlockSpec((tm, tn), lambda i, j, kk: (i, j)),
    ]
    args = [a, w, res]
    if norm_gain is not None:
        assert tn == n
        in_specs.append(pl.BlockSpec((1, n), lambda i, j, kk: (0, 0)))
        args.append(norm_gain.reshape(1, n).astype(F32))
    return pl.pallas_call(
        functools.partial(_mm_res_kernel, final_norm=norm_gain is not None),
        grid=(m // tm, n // tn, k // tk),
        in_specs=in_specs,
        out_specs=pl.BlockSpec((tm, tn), lambda i, j, kk: (i, j)),
        out_shape=jax.ShapeDtypeStruct((m, n), F32),
        compiler_params=_params("parallel", "parallel", "arbitrary"),
        name="mm_res",
    )(*args)


def _chan_dft_kernel(z_ref, cs_ref, p_ref, q_ref, *, groups, gdim):
    for g in range(groups):
        pq = jnp.dot(z_ref[:, g * gdim:(g + 1) * gdim], cs_ref[...], preferred_element_type=F32)
        p_ref[:, g * gdim:(g + 1) * gdim] = pq[:, :gdim].astype(p_ref.dtype)
        q_ref[:, g * gdim:(g + 1) * gdim] = pq[:, gdim:].astype(q_ref.dtype)


def chan_dft(z, cs, *, width, groups, tm):
    m = z.shape[0]
    gdim = width // groups
    tm = _tile(m, tm)
    out = jax.ShapeDtypeStruct((m, width), BF16)
    return pl.pallas_call(
        functools.partial(_chan_dft_kernel, groups=groups, gdim=gdim),
        grid=(m // tm,),
        in_specs=[pl.BlockSpec((tm, width), lambda i: (i, 0)),
                  pl.BlockSpec((gdim, 2 * gdim), lambda i: (0, 0))],
        out_specs=[pl.BlockSpec((tm, width), lambda i: (i, 0))] * 2,
        out_shape=[out, out],
        compiler_params=_params("parallel"),
        name="chan_dft",
    )(z, cs)


def _seq_dft1_kernel(p_ref, q_ref, tp_ref, tq_ref, o_ref, *, nb, width):
    for t in range(nb):
        sl = slice(t * width, (t + 1) * width)
        acc = jnp.dot(tp_ref[t], p_ref[0, :, sl], preferred_element_type=F32)
        acc += jnp.dot(tq_ref[t], q_ref[0, :, sl], preferred_element_type=F32)
        o_ref[0, :, sl] = acc.astype(o_ref.dtype)


def seq_dft1(p3, q3, tp, tq, *, width, nb):
    b, s1, cols = p3.shape
    s2 = cols // width
    nb = _tile(s2, nb)
    dspec = pl.BlockSpec((1, s1, nb * width), lambda bi, j: (bi, 0, j))
    tspec = pl.BlockSpec((nb, 2 * s1, s1), lambda bi, j: (j, 0, 0))
    return pl.pallas_call(
        functools.partial(_seq_dft1_kernel, nb=nb, width=width),
        grid=(b, s2 // nb),
        in_specs=[dspec, dspec, tspec, tspec],
        out_specs=pl.BlockSpec((1, 2 * s1, nb * width), lambda bi, j: (bi, 0, j)),
        out_shape=jax.ShapeDtypeStruct((b, 2 * s1, cols), BF16),
        compiler_params=_params("parallel", "parallel"),
        name="seq_dft1",
    )(p3, q3, tp, tq)


def _seq_dft2_kernel(re_ref, im_ref, c_ref, s_ref, o_ref, *, kb, width):
    for t in range(kb):
        acc = jnp.dot(c_ref[...], re_ref[0, t], preferred_element_type=F32)
        acc += jnp.dot(s_ref[...], im_ref[0, t], preferred_element_type=F32)
        o_ref[0, :, t * width:(t + 1) * width] = acc.astype(o_ref.dtype)


def seq_dft2(a4, c2, s2m, *, s1, kb):
    b, _, s2, width = a4.shape
    kb = _tile(s1, kb)
    nblk = s1 // kb
    return pl.pallas_call(
        functools.partial(_seq_dft2_kernel, kb=kb, width=width),
        grid=(b, nblk),
        in_specs=[
            pl.BlockSpec((1, kb, s2, width), lambda bi, j: (bi, j, 0, 0)),
            pl.BlockSpec((1, kb, s2, width), lambda bi, j: (bi, j + nblk, 0, 0)),
            pl.BlockSpec((s2, s2), lambda bi, j: (0, 0)),
            pl.BlockSpec((s2, s2), lambda bi, j: (0, 0)),
        ],
        out_specs=pl.BlockSpec((1, s2, kb * width), lambda bi, j: (bi, 0, j)),
        out_shape=jax.ShapeDtypeStruct((b, s2, s1 * width), BF16),
        compiler_params=_params("parallel", "parallel"),
        name="seq_dft2",
    )(a4, a4, c2, s2m)


def _dft_tables(seq, gdim):
    s2 = FOURIER_SEQ_INNER
    s1 = seq // s2
    c = np.arange(gdim)
    ang = 2.0 * np.pi * ((c[:, None] * c[None, :]) % gdim) / gdim
    cs = np.concatenate([np.cos(ang), np.sin(ang)], axis=1) / math.sqrt(gdim)
    k1 = np.arange(s1)
    n1 = np.arange(s1)
    n2 = np.arange(s2)
    idx = (k1[None, :, None] * (s2 * n1[None, None, :] + n2[:, None, None])) % seq
    phi = 2.0 * np.pi * idx / seq
    co, si = np.cos(phi), np.sin(phi)
    tp = np.concatenate([co, -si], axis=1)
    tq = np.concatenate([-si, -co], axis=1)
    k2 = np.arange(s2)
    ang2 = 2.0 * np.pi * ((k2[:, None] * n2[None, :]) % s2) / s2
    c2 = np.cos(ang2) / math.sqrt(seq)
    s2m = np.sin(ang2) / math.sqrt(seq)
    as_bf16 = lambda x: jnp.asarray(x, dtype=F32).astype(BF16)
    return as_bf16(cs), as_bf16(tp), as_bf16(tq), as_bf16(c2), as_bf16(s2m)


def _rope128(r, ct, sa, sb):
    quarter = LANES // 4
    return r * ct + pltpu.roll(r, quarter, 1) * sa + pltpu.roll(r, LANES - quarter, 1) * sb


def _q_proj_kernel(c_ref, g_ref, w_ref, ct_ref, sa_ref, sb_ref, o_ref, *, heads, scale):
    xn = _rms(c_ref[...].astype(F32), g_ref[...]).astype(BF16)
    acc = jnp.dot(xn, w_ref[...], preferred_element_type=F32)
    ct, sa, sb = ct_ref[...], sa_ref[...], sb_ref[...]
    for h in range(heads):
        base = 2 * LANES * h
        o_ref[:, base:base + LANES] = (acc[:, base:base + LANES] * scale).astype(o_ref.dtype)
        r = _rope128(acc[:, base + LANES:base + 2 * LANES], ct, sa, sb)
        o_ref[:, base + LANES:base + 2 * LANES] = (r * scale).astype(o_ref.dtype)


def q_proj(z, g, w, ct, sa, sb, *, col_block, rank, heads, scale, tm):
    m = z.shape[0]
    tm = _tile(m, tm)
    n = w.shape[1]
    tab = pl.BlockSpec((tm, LANES), lambda i: (i, 0))
    return pl.pallas_call(
        functools.partial(_q_proj_kernel, heads=heads, scale=scale),
        grid=(m // tm,),
        in_specs=[pl.BlockSpec((tm, rank), lambda i: (i, col_block)),
                  pl.BlockSpec((1, rank), lambda i: (0, 0)),
                  pl.BlockSpec((rank, n), lambda i: (0, 0)),
                  tab, tab, tab],
        out_specs=pl.BlockSpec((tm, n), lambda i: (i, 0)),
        out_shape=jax.ShapeDtypeStruct((m, n), BF16),
        compiler_params=_params("parallel"),
        name="q_proj",
    )(z, g.reshape(1, rank).astype(F32), w, ct, sa, sb)


def _kv_proj_kernel(c_ref, kr_ref, g_ref, w_ref, ct_ref, sa_ref, sb_ref, k_ref, v_ref, *, heads):
    xn = _rms(c_ref[...].astype(F32), g_ref[...]).astype(BF16)
    acc = jnp.dot(xn, w_ref[...], preferred_element_type=F32)
    kr = _rope128(kr_ref[...].astype(F32), ct_ref[...], sa_ref[...], sb_ref[...]).astype(k_ref.dtype)
    for h in range(heads):
        k_ref[:, 2 * LANES * h:2 * LANES * h + LANES] = acc[:, LANES * h:LANES * (h + 1)].astype(k_ref.dtype)
        k_ref[:, 2 * LANES * h + LANES:2 * LANES * (h + 1)] = kr
        v_ref[:, 2 * LANES * h:2 * LANES * h + LANES] = acc[:, LANES * (heads + h):LANES * (heads + h + 1)].astype(v_ref.dtype)
        v_ref[:, 2 * LANES * h + LANES:2 * LANES * (h + 1)] = jnp.ones((acc.shape[0], LANES), v_ref.dtype)


def kv_proj(z, g, w, ct, sa, sb, *, ckv_block, kr_block, rank, heads, tm):
    m = z.shape[0]
    tm = _tile(m, tm)
    tab = pl.BlockSpec((tm, LANES), lambda i: (i, 0))
    return pl.pallas_call(
        functools.partial(_kv_proj_kernel, heads=heads),
        grid=(m // tm,),
        in_specs=[pl.BlockSpec((tm, rank), lambda i: (i, ckv_block)),
                  pl.BlockSpec((tm, LANES), lambda i: (i, kr_block)),
                  pl.BlockSpec((1, rank), lambda i: (0, 0)),
                  pl.BlockSpec((rank, 2 * LANES * heads), lambda i: (0, 0)),
                  tab, tab, tab],
        out_specs=[pl.BlockSpec((tm, 2 * LANES * heads), lambda i: (i, 0))] * 2,
        out_shape=[jax.ShapeDtypeStruct((m, 2 * LANES * heads), BF16)] * 2,
        compiler_params=_params("parallel"),
        name="kv_proj",
    )(z, z, g.reshape(1, rank).astype(F32), w, ct, sa, sb)


def _flash_kernel(q_ref, k_ref, v_ref, o_ref, s0_ref, s1_ref, *, tk, nk):
    q = q_ref[0]
    tq = q.shape[0]
    dv = o_ref.shape[-1]

    def chunk(c):
        start = c * tk
        return pl.ds(start if isinstance(start, int) else pl.multiple_of(start, tk), tk)

    def scores(c):
        return lax.dot_general(q, k_ref[0, chunk(c), :], (((1,), (1,)), ((), ())), preferred_element_type=F32)

    def fold(s_ref, c, m, acc):
        s = s_ref[...]
        m_new = jnp.maximum(m, jnp.max(s, axis=1, keepdims=True))
        p = jnp.exp2(s - m_new).astype(BF16)
        acc = jnp.exp2(m - m_new) * acc + jnp.dot(p, v_ref[0, chunk(c), :], preferred_element_type=F32)
        return m_new, acc

    def pair(j, carry, last):
        m, acc = carry
        s1_ref[...] = scores(2 * j + 1)
        m, acc = fold(s0_ref, 2 * j, m, acc)
        if not last:
            s0_ref[...] = scores(2 * j + 2)
        return fold(s1_ref, 2 * j + 1, m, acc)

    s0_ref[...] = scores(0)
    init = (jnp.full((tq, 1), -jnp.inf, F32), jnp.zeros((tq, 2 * dv), F32))
    carry = init
    for j in range(nk // 2):
        carry = pair(j, carry, j == nk // 2 - 1)
    _, acc = carry
    o_ref[0] = (acc[:, :dv] / acc[:, dv:]).astype(o_ref.dtype)


def flash_attention(q, k, v, *, heads, tq, tk):
    b, s, _ = q.shape
    dk = q.shape[-1] // heads
    dv2 = v.shape[-1] // heads
    dv = dv2 // 2
    tq, tk = _tile(s, tq), _tile(s, min(tk, s // 2))
    nk = s // tk
    assert nk % 2 == 0
    return pl.pallas_call(
        functools.partial(_flash_kernel, tk=tk, nk=nk),
        grid=(b, heads, s // tq),
        in_specs=[pl.BlockSpec((1, tq, dk), lambda bi, h, i: (bi, i, h)),
                  pl.BlockSpec((1, s, dk), lambda bi, h, i: (bi, 0, h)),
                  pl.BlockSpec((1, s, dv2), lambda bi, h, i: (bi, 0, h))],
        out_specs=pl.BlockSpec((1, tq, dv), lambda bi, h, i: (bi, i, h)),
        out_shape=jax.ShapeDtypeStruct((b, s, heads * dv), BF16),
        scratch_shapes=[pltpu.VMEM((tq, tk), F32), pltpu.VMEM((tq, tk), F32)],
        compiler_params=_params("parallel", "parallel", "arbitrary"),
        name="flash_attention",
    )(q, k, v)


def _out_proj_kernel(f_ref, wf_ref, gf_ref, a_ref, ga_ref, w_ref, x_ref, o_ref, y_ref, *, groups, gdim):
    fw = groups * gdim

    @pl.when(pl.program_id(1) == 0)
    def _():
        parts = [jnp.dot(f_ref[:, g * gdim:(g + 1) * gdim], wf_ref[g], preferred_element_type=F32)
                 for g in range(groups)]
        ms = sum(jnp.sum(p * p, axis=-1, keepdims=True) for p in parts) / fw
        inv = lax.rsqrt(ms + NORM_EPS)
        for g in range(groups):
            sl = slice(g * gdim, (g + 1) * gdim)
            y_ref[:, sl] = (parts[g] * inv * gf_ref[:, sl]).astype(BF16)
        _rms_rows(a_ref, ga_ref, y_ref.at[:, fw:])

    acc = jnp.dot(y_ref[...], w_ref[...], preferred_element_type=F32)
    o_ref[...] = (acc + x_ref[...]).astype(o_ref.dtype)


def out_proj(f, wf, gf, a, ga, w, x, *, tm, tn):
    m, fw = f.shape
    aw = a.shape[1]
    groups, gdim, _ = wf.shape
    n = w.shape[1]
    tm, tn = _tile(m, tm), _tile(n, tn)
    return pl.pallas_call(
        functools.partial(_out_proj_kernel, groups=groups, gdim=gdim),
        grid=(m // tm, n // tn),
        in_specs=[pl.BlockSpec((tm, fw), lambda i, j: (i, 0)),
                  pl.BlockSpec((groups, gdim, gdim), lambda i, j: (0, 0, 0)),
                  pl.BlockSpec((1, fw), lambda i, j: (0, 0)),
                  pl.BlockSpec((tm, aw), lambda i, j: (i, 0)),
                  pl.BlockSpec((1, aw), lambda i, j: (0, 0)),
                  pl.BlockSpec((fw + aw, tn), lambda i, j: (0, j)),
                  pl.BlockSpec((tm, tn), lambda i, j: (i, j))],
        out_specs=pl.BlockSpec((tm, tn), lambda i, j: (i, j)),
        out_shape=jax.ShapeDtypeStruct((m, n), F32),
        scratch_shapes=[pltpu.VMEM((tm, fw + aw), BF16)],
        compiler_params=_params("parallel", "arbitrary"),
        name="out_proj",
    )(f, wf, gf.reshape(1, fw).astype(F32), a, ga.reshape(1, aw).astype(F32), w, x)


def _absorb_qk_kernel(wq_ref, k_ref, o_ref, *, scale):
    acc = lax.dot_general(wq_ref[...], k_ref[...], (((1,), (1,)), ((), ())), preferred_element_type=F32)
    o_ref[0] = (acc * scale).astype(o_ref.dtype)


def absorb_qk(wq, kv, *, batch, heads, mem, hd, scale):
    d = wq.shape[0]
    return pl.pallas_call(
        functools.partial(_absorb_qk_kernel, scale=scale),
        grid=(heads, batch),
        in_specs=[pl.BlockSpec((d, hd), lambda h, b: (0, h)),
                  pl.BlockSpec((mem, hd), lambda h, b: (b, h))],
        out_specs=pl.BlockSpec((1, d, mem), lambda h, b: (b, 0, h)),
        out_shape=jax.ShapeDtypeStruct((batch, d, heads * mem), BF16),
        compiler_params=_params("parallel", "parallel"),
        name="absorb_qk",
    )(wq, kv)


def _absorb_vo_kernel(v_ref, wo_ref, o_ref):
    o_ref[0] = jnp.dot(v_ref[...], wo_ref[...], preferred_element_type=F32).astype(o_ref.dtype)


def absorb_vo(kv, wo, *, batch, heads, mem, hd, tn):
    d = wo.shape[1]
    tn = _tile(d, tn)
    return pl.pallas_call(
        _absorb_vo_kernel,
        grid=(heads, d // tn, batch),
        in_specs=[pl.BlockSpec((mem, hd), lambda h, j, b: (b, heads + h)),
                  pl.BlockSpec((hd, tn), lambda h, j, b: (h, j))],
        out_specs=pl.BlockSpec((1, mem, tn), lambda h, j, b: (b, h, j)),
        out_shape=jax.ShapeDtypeStruct((batch, heads * mem, d), BF16),
        compiler_params=_params("parallel", "parallel", "parallel"),
        name="absorb_vo",
    )(kv, wo)


def _xattn_kernel(hf_ref, g_ref, wqk_ref, vo_ref, hres_ref, o_ref, p_ref, hn_ref, *, heads, mem):
    @pl.when(pl.program_id(1) == 0)
    def _():
        _rms_rows(hf_ref, g_ref, hn_ref)
        s = jnp.dot(hn_ref[...], wqk_ref[0], preferred_element_type=F32)
        for h in range(heads):
            seg = s[:, h * mem:(h + 1) * mem]
            e = jnp.exp(seg - jnp.max(seg, axis=-1, keepdims=True))
            p_ref[:, h * mem:(h + 1) * mem] = (e / jnp.sum(e, axis=-1, keepdims=True)).astype(BF16)

    acc = jnp.dot(p_ref[...], vo_ref[0], preferred_element_type=F32)
    o_ref[...] = acc + hres_ref[...]


def xattn(h, g, wqk, vo, *, seq, heads, mem, tm, tn):
    m, d = h.shape
    tm, tn = _tile(seq, tm), _tile(d, tn)
    per_b = seq // tm
    hm = heads * mem
    return pl.pallas_call(
        functools.partial(_xattn_kernel, heads=heads, mem=mem),
        grid=(m // tm, d // tn),
        in_specs=[pl.BlockSpec((tm, d), lambda i, j: (i, 0)),
                  pl.BlockSpec((1, d), lambda i, j: (0, 0)),
                  pl.BlockSpec((1, d, hm), lambda i, j: (i // per_b, 0, 0)),
                  pl.BlockSpec((1, hm, tn), lambda i, j: (i // per_b, 0, j)),
                  pl.BlockSpec((tm, tn), lambda i, j: (i, j))],
        out_specs=pl.BlockSpec((tm, tn), lambda i, j: (i, j)),
        out_shape=jax.ShapeDtypeStruct((m, d), F32),
        scratch_shapes=[pltpu.VMEM((tm, hm), BF16), pltpu.VMEM((tm, d), BF16)],
        compiler_params=_params("parallel", "arbitrary"),
        name="xattn",
    )(h, g.reshape(1, d).astype(F32), wqk, vo, h)


def _pad_cols(w, n):
    return jnp.pad(w, ((0, 0), (0, n - w.shape[1])))


def kernel(x, mem, positions, g_mix, w_in, w_fourier, g_q_lora, w_uq, g_kv_lora, w_ukv, g_fourier_out, g_mla_out, w_out, g_xattn, g_mem, w_xq, w_xk, w_xv, w_xo, g_mlp, w_ff1, w_ff2, g_final):
    batch, seq, d = x.shape
    depth = g_mix.shape[0]
    groups, gdim = w_fourier.shape[1], w_fourier.shape[2]
    fw = groups * gdim
    q_rank, heads, qk_dim = w_uq.shape[1], w_uq.shape[2], w_uq.shape[3]
    kv_rank = w_ukv.shape[1]
    nope = LANES
    rope = qk_dim - nope
    vdim = w_ukv.shape[3] - nope
    assert rope == LANES // 2 and vdim == LANES and fw % q_rank == 0
    assert (fw + q_rank) % kv_rank == 0 and (fw + q_rank + kv_rank) % LANES == 0
    xheads, xhd = w_xq.shape[2], w_xq.shape[3]
    mtok = mem.shape[1]
    t = batch * seq
    s2 = FOURIER_SEQ_INNER
    s1 = seq // s2

    half = rope // 2
    inv_freq = ROPE_THETA ** (-jnp.arange(half, dtype=F32) / half)
    ang = positions.astype(F32).reshape(t, 1) * inv_freq
    cos, sin = jnp.cos(ang), jnp.sin(ang)
    zq = jnp.zeros_like(cos)
    ct = jnp.concatenate([cos, cos, zq, zq], axis=1)
    sa = jnp.concatenate([zq, sin, zq, zq], axis=1)
    sb = jnp.concatenate([-sin, zq, zq, zq], axis=1)

    cs, tp, tq_tab, c2, s2m = _dft_tables(seq, gdim)

    in_width = w_in.shape[2]
    z_width = -(-(in_width + rope) // 768) * 768
    scale = float(qk_dim) ** -0.5 * math.log2(math.e)

    h = x.reshape(t, d)
    mem2 = mem.reshape(batch * mtok, d)
    for layer in range(depth):
        w_in_b = _pad_cols(w_in[layer], z_width).astype(BF16)
        wq = jnp.pad(w_uq[layer], ((0, 0), (0, 0), (0, 2 * LANES - qk_dim))).reshape(q_rank, heads * 2 * LANES).astype(BF16)
        wkv = jnp.concatenate([w_ukv[layer][:, :, :nope].reshape(kv_rank, heads * nope),
                               w_ukv[layer][:, :, nope:].reshape(kv_rank, heads * vdim)], axis=1).astype(BF16)
        wf = w_fourier[layer].astype(BF16)
        w_out_b = w_out[layer].astype(BF16)
        wxq = w_xq[layer].reshape(d, xheads * xhd).astype(BF16)
        wxkv = jnp.concatenate([w_xk[layer].reshape(d, xheads * xhd), w_xv[layer].reshape(d, xheads * xhd)], axis=1).astype(BF16)
        wxo = w_xo[layer].reshape(xheads * xhd, d).astype(BF16)
        w1 = w_ff1[layer].astype(BF16)
        w2 = w_ff2[layer].astype(BF16)

        z = norm_mm(h, g_mix[layer], w_in_b, tm=512, tn=768, out_dtype=BF16)
        p, q = chan_dft(z, cs, width=fw, groups=groups, tm=1024)
        a = seq_dft1(p.reshape(batch, s1, s2 * fw), q.reshape(batch, s1, s2 * fw), tp, tq_tab, width=fw, nb=8)
        f = seq_dft2(a.reshape(batch, 2 * s1, s2, fw), c2, s2m, s1=s1, kb=4).reshape(t, fw)
        qh = q_proj(z, g_q_lora[layer], wq, ct, sa, sb, col_block=fw // q_rank, rank=q_rank,
                    heads=heads, scale=scale, tm=256)
        kh, vh = kv_proj(z, g_kv_lora[layer], wkv, ct, sa, sb, ckv_block=(fw + q_rank) // kv_rank,
                         kr_block=(fw + q_rank + kv_rank) // LANES, rank=kv_rank, heads=heads, tm=512)
        o = flash_attention(qh.reshape(batch, seq, -1), kh.reshape(batch, seq, -1), vh.reshape(batch, seq, -1),
                            heads=heads, tq=512, tk=512).reshape(t, heads * vdim)
        h = out_proj(f, wf, g_fourier_out[layer], o, g_mla_out[layer], w_out_b, h, tm=512, tn=1024)

        kvm = norm_mm(mem2, g_mem[layer], wxkv, tm=512, tn=512, out_dtype=BF16)
        wqk = absorb_qk(wxq, kvm, batch=batch, heads=xheads, mem=mtok, hd=xhd, scale=float(xhd) ** -0.5)
        vo = absorb_vo(kvm, wxo, batch=batch, heads=xheads, mem=mtok, hd=xhd, tn=2048)
        h = xattn(h, g_xattn[layer], wqk, vo, seq=seq, heads=xheads, mem=mtok, tm=512, tn=1024)

        act = norm_mm(h, g_mlp[layer], w1, tm=512, tn=1024, out_dtype=BF16, act="relu2")
        if layer == depth - 1:
            h = mm_res(act, w2, h, tm=512, tn=d, tk=512, norm_gain=g_final)
        else:
            h = mm_res(act, w2, h, tm=1024, tn=1024, tk=2048)

    return h.astype(x.dtype).reshape(batch, seq, d)
```

```python
import functools
import math

import numpy as np
import jax
import jax.numpy as jnp
from jax import lax
from jax.experimental import pallas as pl
from jax.experimental.pallas import tpu as pltpu

F32 = jnp.float32
BF16 = jnp.bfloat16

NORM_EPS = 1e-6
ROPE_THETA = 10000.0
LANES = 128
VMEM_LIMIT_BYTES = 56 * 1024 * 1024
FOURIER_SEQ_INNER = 128


def _params(*sem):
    return pltpu.CompilerParams(dimension_semantics=sem, vmem_limit_bytes=VMEM_LIMIT_BYTES)


def _rms(xf, g):
    ms = jnp.mean(xf * xf, axis=-1, keepdims=True)
    return xf * lax.rsqrt(ms + NORM_EPS) * g


def _rms_rows(a_ref, g_ref, o_ref, chunk=64):
    rows = a_ref.shape[0]
    chunk = min(chunk, rows)

    def body(c, carry):
        r = pl.ds(pl.multiple_of(c * chunk, chunk), chunk)
        o_ref[r, :] = _rms(a_ref[r, :].astype(F32), g_ref[...]).astype(o_ref.dtype)
        return carry

    lax.fori_loop(0, rows // chunk, body, 0)


def _tile(dim, pref):
    t = min(dim, pref)
    assert dim % t == 0, (dim, pref)
    return t


def _norm_mm_kernel(a_ref, g_ref, w_ref, *rest, act, has_res):
    if has_res:
        res_ref, o_ref, xn_ref = rest
    else:
        o_ref, xn_ref = rest

    @pl.when(pl.program_id(1) == 0)
    def _():
        _rms_rows(a_ref, g_ref, xn_ref)

    acc = jnp.dot(xn_ref[...], w_ref[...], preferred_element_type=F32)
    if act == "relu2":
        acc = jnp.square(jnp.maximum(acc, 0.0))
    if has_res:
        acc = acc + res_ref[...]
    o_ref[...] = acc.astype(o_ref.dtype)


def norm_mm(a, g, w, *, tm, tn, out_dtype, act=None, res=None):
    m, k = a.shape
    n = w.shape[1]
    tm, tn = _tile(m, tm), _tile(n, tn)
    in_specs = [
        pl.BlockSpec((tm, k), lambda i, j: (i, 0)),
        pl.BlockSpec((1, k), lambda i, j: (0, 0)),
        pl.BlockSpec((k, tn), lambda i, j: (0, j)),
    ]
    args = [a, g.reshape(1, k).astype(F32), w]
    if res is not None:
        in_specs.append(pl.BlockSpec((tm, tn), lambda i, j: (i, j)))
        args.append(res)
    return pl.pallas_call(
        functools.partial(_norm_mm_kernel, act=act, has_res=res is not None),
        grid=(m // tm, n // tn),
        in_specs=in_specs,
        out_specs=pl.BlockSpec((tm, tn), lambda i, j: (i, j)),
        out_shape=jax.ShapeDtypeStruct((m, n), out_dtype),
        scratch_shapes=[pltpu.VMEM((tm, k), BF16)],
        compiler_params=_params("parallel", "arbitrary"),
        name="norm_mm",
    )(*args)


def _mm_res_kernel(a_ref, w_ref, res_ref, *rest, final_norm):
    if final_norm:
        g_ref, o_ref = rest
    else:
        (o_ref,) = rest
    kk = pl.program_id(2)

    @pl.when(kk == 0)
    def _():
        o_ref[...] = res_ref[...]

    o_ref[...] += jnp.dot(a_ref[...], w_ref[...], preferred_element_type=F32)

    if final_norm:
        @pl.when(kk == pl.num_programs(2) - 1)
        def _():
            _rms_rows(o_ref, g_ref, o_ref)


def mm_res(a, w, res, *, tm, tn, tk, norm_gain=None):
    m, k = a.shape
    n = w.shape[1]
    tm, tn, tk = _tile(m, tm), _tile(n, tn), _tile(k, tk)
    in_specs = [
        pl.BlockSpec((tm, tk), lambda i, j, kk: (i, kk)),
        pl.BlockSpec((tk, tn), lambda i, j, kk: (kk, j)),
        pl.BlockSpec((tm, tn), lambda i, j, kk: (i, j), pipeline_mode=pl.Buffered(1)),
    ]
    args = [a, w, res]
    if norm_gain is not None:
        assert tn == n
        in_specs.append(pl.BlockSpec((1, n), lambda i, j, kk: (0, 0)))
        args.append(norm_gain.reshape(1, n).astype(F32))
    return pl.pallas_call(
        functools.partial(_mm_res_kernel, final_norm=norm_gain is not None),
        grid=(m // tm, n // tn, k // tk),
        in_specs=in_specs,
        out_specs=pl.BlockSpec((tm, tn), lambda i, j, kk: (i, j)),
        out_shape=jax.ShapeDtypeStruct((m, n), F32),
        compiler_params=_params("parallel", "parallel", "arbitrary"),
        name="mm_res",
    )(*args)


def _chan_dft_kernel(z_ref, cs_ref, p_ref, q_ref, *, groups, gdim):
    for g in range(groups):
        pq = jnp.dot(z_ref[:, g * gdim:(g + 1) * gdim], cs_ref[...], preferred_element_type=F32)
        p_ref[:, g * gdim:(g + 1) * gdim] = pq[:, :gdim].astype(p_ref.dtype)
        q_ref[:, g * gdim:(g + 1) * gdim] = pq[:, gdim:].astype(q_ref.dtype)


def chan_dft(z, cs, *, width, groups, tm):
    m = z.shape[0]
    gdim = width // groups
    tm = _tile(m, tm)
    out = jax.ShapeDtypeStruct((m, width), BF16)
    return pl.pallas_call(
        functools.partial(_chan_dft_kernel, groups=groups, gdim=gdim),
        grid=(m // tm,),
        in_specs=[pl.BlockSpec((tm, width), lambda i: (i, 0)),
                  pl.BlockSpec((gdim, 2 * gdim), lambda i: (0, 0))],
        out_specs=[pl.BlockSpec((tm, width), lambda i: (i, 0))] * 2,
        out_shape=[out, out],
        compiler_params=_params("parallel"),
        name="chan_dft",
    )(z, cs)


def _seq_dft1_kernel(p_ref, q_ref, tp_ref, tq_ref, o_ref, *, nb, width):
    for t in range(nb):
        sl = slice(t * width, (t + 1) * width)
        acc = jnp.dot(tp_ref[t], p_ref[0, :, sl], preferred_element_type=F32)
        acc += jnp.dot(tq_ref[t], q_ref[0, :, sl], preferred_element_type=F32)
        o_ref[0, :, sl] = acc.astype(o_ref.dtype)


def seq_dft1(p3, q3, tp, tq, *, width, nb):
    b, s1, cols = p3.shape
    s2 = cols // width
    nb = _tile(s2, nb)
    dspec = pl.BlockSpec((1, s1, nb * width), lambda bi, j: (bi, 0, j))
    tspec = pl.BlockSpec((nb, 2 * s1, s1), lambda bi, j: (j, 0, 0))
    return pl.pallas_call(
        functools.partial(_seq_dft1_kernel, nb=nb, width=width),
        grid=(b, s2 // nb),
        in_specs=[dspec, dspec, tspec, tspec],
        out_specs=pl.BlockSpec((1, 2 * s1, nb * width), lambda bi, j: (bi, 0, j)),
        out_shape=jax.ShapeDtypeStruct((b, 2 * s1, cols), BF16),
        compiler_params=_params("parallel", "parallel"),
        name="seq_dft1",
    )(p3, q3, tp, tq)


def _seq_dft2_kernel(re_ref, im_ref, c_ref, s_ref, o_ref, *, kb, width):
    for t in range(kb):
        acc = jnp.dot(c_ref[...], re_ref[0, t], preferred_element_type=F32)
        acc += jnp.dot(s_ref[...], im_ref[0, t], preferred_element_type=F32)
        o_ref[0, :, t * width:(t + 1) * width] = acc.astype(o_ref.dtype)


def seq_dft2(a4, c2, s2m, *, s1, kb):
    b, _, s2, width = a4.shape
    kb = _tile(s1, kb)
    nblk = s1 // kb
    return pl.pallas_call(
        functools.partial(_seq_dft2_kernel, kb=kb, width=width),
        grid=(b, nblk),
        in_specs=[
            pl.BlockSpec((1, kb, s2, width), lambda bi, j: (bi, j, 0, 0)),
            pl.BlockSpec((1, kb, s2, width), lambda bi, j: (bi, j + nblk, 0, 0)),
            pl.BlockSpec((s2, s2), lambda bi, j: (0, 0)),
            pl.BlockSpec((s2, s2), lambda bi, j: (0, 0)),
        ],
        out_specs=pl.BlockSpec((1, s2, kb * width), lambda bi, j: (bi, 0, j)),
        out_shape=jax.ShapeDtypeStruct((b, s2, s1 * width), BF16),
        compiler_params=_params("parallel", "parallel"),
        name="seq_dft2",
    )(a4, a4, c2, s2m)


def _dft_tables(seq, gdim):
    s2 = FOURIER_SEQ_INNER
    s1 = seq // s2
    c = np.arange(gdim)
    ang = 2.0 * np.pi * ((c[:, None] * c[None, :]) % gdim) / gdim
    cs = np.concatenate([np.cos(ang), np.sin(ang)], axis=1) / math.sqrt(gdim)
    k1 = np.arange(s1)
    n1 = np.arange(s1)
    n2 = np.arange(s2)
    idx = (k1[None, :, None] * (s2 * n1[None, None, :] + n2[:, None, None])) % seq
    phi = 2.0 * np.pi * idx / seq
    co, si = np.cos(phi), np.sin(phi)
    tp = np.concatenate([co, -si], axis=1)
    tq = np.concatenate([-si, -co], axis=1)
    k2 = np.arange(s2)
    ang2 = 2.0 * np.pi * ((k2[:, None] * n2[None, :]) % s2) / s2
    c2 = np.cos(ang2) / math.sqrt(seq)
    s2m = np.sin(ang2) / math.sqrt(seq)
    as_bf16 = lambda x: jnp.asarray(x, dtype=F32).astype(BF16)
    return as_bf16(cs), as_bf16(tp), as_bf16(tq), as_bf16(c2), as_bf16(s2m)


def _rope128(r, ct, sa, sb):
    quarter = LANES // 4
    return r * ct + pltpu.roll(r, quarter, 1) * sa + pltpu.roll(r, LANES - quarter, 1) * sb


def _q_proj_kernel(c_ref, g_ref, w_ref, ct_ref, sa_ref, sb_ref, o_ref, *, heads, scale):
    xn = _rms(c_ref[...].astype(F32), g_ref[...]).astype(BF16)
    acc = jnp.dot(xn, w_ref[...], preferred_element_type=F32)
    ct, sa, sb = ct_ref[...], sa_ref[...], sb_ref[...]
    for h in range(heads):
        base = 2 * LANES * h
        o_ref[:, base:base + LANES] = (acc[:, base:base + LANES] * scale).astype(o_ref.dtype)
        r = _rope128(acc[:, base + LANES:base + 2 * LANES], ct, sa, sb)
        o_ref[:, base + LANES:base + 2 * LANES] = (r * scale).astype(o_ref.dtype)


def q_proj(z, g, w, ct, sa, sb, *, col_block, rank, heads, scale, tm):
    m = z.shape[0]
    tm = _tile(m, tm)
    n = w.shape[1]
    tab = pl.BlockSpec((tm, LANES), lambda i: (i, 0))
    return pl.pallas_call(
        functools.partial(_q_proj_kernel, heads=heads, scale=scale),
        grid=(m // tm,),
        in_specs=[pl.BlockSpec((tm, rank), lambda i: (i, col_block)),
                  pl.BlockSpec((1, rank), lambda i: (0, 0)),
                  pl.BlockSpec((rank, n), lambda i: (0, 0)),
                  tab, tab, tab],
        out_specs=pl.BlockSpec((tm, n), lambda i: (i, 0)),
        out_shape=jax.ShapeDtypeStruct((m, n), BF16),
        compiler_params=_params("parallel"),
        name="q_proj",
    )(z, g.reshape(1, rank).astype(F32), w, ct, sa, sb)


def _kv_proj_kernel(c_ref, kr_ref, g_ref, w_ref, ct_ref, sa_ref, sb_ref, k_ref, v_ref, *, heads):
    xn = _rms(c_ref[...].astype(F32), g_ref[...]).astype(BF16)
    acc = jnp.dot(xn, w_ref[...], preferred_element_type=F32)
    kr = _rope128(kr_ref[...].astype(F32), ct_ref[...], sa_ref[...], sb_ref[...]).astype(k_ref.dtype)
    for h in range(heads):
        k_ref[:, 2 * LANES * h:2 * LANES * h + LANES] = acc[:, LANES * h:LANES * (h + 1)].astype(k_ref.dtype)
        k_ref[:, 2 * LANES * h + LANES:2 * LANES * (h + 1)] = kr
        v_ref[:, 2 * LANES * h:2 * LANES * h + LANES] = acc[:, LANES * (heads + h):LANES * (heads + h + 1)].astype(v_ref.dtype)
        v_ref[:, 2 * LANES * h + LANES:2 * LANES * (h + 1)] = jnp.ones((acc.shape[0], LANES), v_ref.dtype)


def kv_proj(z, g, w, ct, sa, sb, *, ckv_block, kr_block, rank, heads, tm):
    m = z.shape[0]
    tm = _tile(m, tm)
    tab = pl.BlockSpec((tm, LANES), lambda i: (i, 0))
    return pl.pallas_call(
        functools.partial(_kv_proj_kernel, heads=heads),
        grid=(m // tm,),
        in_specs=[pl.BlockSpec((tm, rank), lambda i: (i, ckv_block)),
                  pl.BlockSpec((tm, LANES), lambda i: (i, kr_block)),
                  pl.BlockSpec((1, rank), lambda i: (0, 0)),
                  pl.BlockSpec((rank, 2 * LANES * heads), lambda i: (0, 0)),
                  tab, tab, tab],
        out_specs=[pl.BlockSpec((tm, 2 * LANES * heads), lambda i: (i, 0))] * 2,
        out_shape=[jax.ShapeDtypeStruct((m, 2 * LANES * heads), BF16)] * 2,
        compiler_params=_params("parallel"),
        name="kv_proj",
    )(z, z, g.reshape(1, rank).astype(F32), w, ct, sa, sb)


def _flash_kernel(q_ref, k_ref, v_ref, o_ref, s0_ref, s1_ref, *, tk, nk):
    q = q_ref[0]
    tq = q.shape[0]
    dv = o_ref.shape[-1]

    def chunk(c):
        start = c * tk
        return pl.ds(start if isinstance(start, int) else pl.multiple_of(start, tk), tk)

    def scores(c):
        return lax.dot_general(q, k_ref[0, chunk(c), :], (((1,), (1,)), ((), ())), preferred_element_type=F32)

    def fold(s_ref, c, m, acc):
        s = s_ref[...]
        m_new = jnp.maximum(m, jnp.max(s, axis=1, keepdims=True))
        p = jnp.exp2(s - m_new).astype(BF16)
        acc = jnp.exp2(m - m_new) * acc + jnp.dot(p, v_ref[0, chunk(c), :], preferred_element_type=F32)
        return m_new, acc

    def pair(j, carry, last):
        m, acc = carry
        s1_ref[...] = scores(2 * j + 1)
        m, acc = fold(s0_ref, 2 * j, m, acc)
        if not last:
            s0_ref[...] = scores(2 * j + 2)
        return fold(s1_ref, 2 * j + 1, m, acc)

    s0_ref[...] = scores(0)
    init = (jnp.full((tq, 1), -jnp.inf, F32), jnp.zeros((tq, 2 * dv), F32))
    carry = init
    for j in range(nk // 2):
        carry = pair(j, carry, j == nk // 2 - 1)
    _, acc = carry
    o_ref[0] = (acc[:, :dv] / acc[:, dv:]).astype(o_ref.dtype)


def flash_attention(q, k, v, *, heads, tq, tk):
    b, s, _ = q.shape
    dk = q.shape[-1] // heads
    dv2 = v.shape[-1] // heads
    dv = dv2 // 2
    tq, tk = _tile(s, tq), _tile(s, min(tk, s // 2))
    nk = s // tk
    assert nk % 2 == 0
    return pl.pallas_call(
        functools.partial(_flash_kernel, tk=tk, nk=nk),
        grid=(b, heads, s // tq),
        in_specs=[pl.BlockSpec((1, tq, dk), lambda bi, h, i: (bi, i, h)),
                  pl.BlockSpec((1, s, dk), lambda bi, h, i: (bi, 0, h)),
                  pl.BlockSpec((1, s, dv2), lambda bi, h, i: (bi, 0, h))],
        out_specs=pl.BlockSpec((1, tq, dv), lambda bi, h, i: (bi, i, h)),
        out_shape=jax.ShapeDtypeStruct((b, s, heads * dv), BF16),
        scratch_shapes=[pltpu.VMEM((tq, tk), F32), pltpu.VMEM((tq, tk), F32)],
        compiler_params=_params("parallel", "parallel", "arbitrary"),
        name="flash_attention",
    )(q, k, v)


def _out_proj_kernel(f_ref, wf_ref, gf_ref, a_ref, ga_ref, w_ref, x_ref, o_ref, y_ref, *, groups, gdim):
    fw = groups * gdim

    @pl.when(pl.program_id(1) == 0)
    def _():
        rows = f_ref.shape[0]
        chunk = min(rows, 256)

        def body(c, carry):
            r = pl.ds(pl.multiple_of(c * chunk, chunk), chunk)
            parts = [jnp.dot(f_ref[r, g * gdim:(g + 1) * gdim], wf_ref[g], preferred_element_type=F32)
                     for g in range(groups)]
            ms = sum(jnp.sum(p * p, axis=-1, keepdims=True) for p in parts) / fw
            inv = lax.rsqrt(ms + NORM_EPS)
            for g in range(groups):
                sl = slice(g * gdim, (g + 1) * gdim)
                y_ref[r, sl] = (parts[g] * inv * gf_ref[:, sl]).astype(BF16)
            return carry

        lax.fori_loop(0, rows // chunk, body, 0)
        _rms_rows(a_ref, ga_ref, y_ref.at[:, fw:])

    acc = jnp.dot(y_ref[...], w_ref[...], preferred_element_type=F32)
    o_ref[...] = (acc + x_ref[...]).astype(o_ref.dtype)


def out_proj(f, wf, gf, a, ga, w, x, *, tm, tn):
    m, fw = f.shape
    aw = a.shape[1]
    groups, gdim, _ = wf.shape
    n = w.shape[1]
    tm, tn = _tile(m, tm), _tile(n, tn)
    return pl.pallas_call(
        functools.partial(_out_proj_kernel, groups=groups, gdim=gdim),
        grid=(m // tm, n // tn),
        in_specs=[pl.BlockSpec((tm, fw), lambda i, j: (i, 0)),
                  pl.BlockSpec((groups, gdim, gdim), lambda i, j: (0, 0, 0)),
                  pl.BlockSpec((1, fw), lambda i, j: (0, 0)),
                  pl.BlockSpec((tm, aw), lambda i, j: (i, 0)),
                  pl.BlockSpec((1, aw), lambda i, j: (0, 0)),
                  pl.BlockSpec((fw + aw, tn), lambda i, j: (0, j)),
                  pl.BlockSpec((tm, tn), lambda i, j: (i, j))],
        out_specs=pl.BlockSpec((tm, tn), lambda i, j: (i, j)),
        out_shape=jax.ShapeDtypeStruct((m, n), F32),
        scratch_shapes=[pltpu.VMEM((tm, fw + aw), BF16)],
        compiler_params=_params("parallel", "arbitrary"),
        name="out_proj",
    )(f, wf, gf.reshape(1, fw).astype(F32), a, ga.reshape(1, aw).astype(F32), w, x)


def _absorb_qk_kernel(wq_ref, k_ref, o_ref, *, scale):
    acc = lax.dot_general(wq_ref[...], k_ref[...], (((1,), (1,)), ((), ())), preferred_element_type=F32)
    o_ref[0] = (acc * scale).astype(o_ref.dtype)


def absorb_qk(wq, kv, *, batch, heads, mem, hd, scale):
    d = wq.shape[0]
    return pl.pallas_call(
        functools.partial(_absorb_qk_kernel, scale=scale),
        grid=(heads, batch),
        in_specs=[pl.BlockSpec((d, hd), lambda h, b: (0, h)),
                  pl.BlockSpec((mem, hd), lambda h, b: (b, h))],
        out_specs=pl.BlockSpec((1, d, mem), lambda h, b: (b, 0, h)),
        out_shape=jax.ShapeDtypeStruct((batch, d, heads * mem), BF16),
        compiler_params=_params("parallel", "parallel"),
        name="absorb_qk",
    )(wq, kv)


def _absorb_vo_kernel(v_ref, wo_ref, o_ref):
    o_ref[0] = jnp.dot(v_ref[...], wo_ref[...], preferred_element_type=F32).astype(o_ref.dtype)


def absorb_vo(kv, wo, *, batch, heads, mem, hd, tn):
    d = wo.shape[1]
    tn = _tile(d, tn)
    return pl.pallas_call(
        _absorb_vo_kernel,
        grid=(heads, d // tn, batch),
        in_specs=[pl.BlockSpec((mem, hd), lambda h, j, b: (b, heads + h)),
                  pl.BlockSpec((hd, tn), lambda h, j, b: (h, j))],
        out_specs=pl.BlockSpec((1, mem, tn), lambda h, j, b: (b, h, j)),
        out_shape=jax.ShapeDtypeStruct((batch, heads * mem, d), BF16),
        compiler_params=_params("parallel", "parallel", "parallel"),
        name="absorb_vo",
    )(kv, wo)


def _xattn_kernel(hf_ref, g_ref, wqk_ref, vo_ref, o_ref, p_ref, hn_ref, *, heads, mem):
    tn = o_ref.shape[1]
    col = pl.multiple_of(pl.program_id(1) * tn, tn)

    @pl.when(pl.program_id(1) == 0)
    def _():
        _rms_rows(hf_ref, g_ref, hn_ref)
        s = jnp.dot(hn_ref[...], wqk_ref[0], preferred_element_type=F32)
        for h in range(heads):
            seg = s[:, h * mem:(h + 1) * mem]
            e = jnp.exp(seg - jnp.max(seg, axis=-1, keepdims=True))
            p_ref[:, h * mem:(h + 1) * mem] = (e / jnp.sum(e, axis=-1, keepdims=True)).astype(BF16)

    acc = jnp.dot(p_ref[...], vo_ref[0], preferred_element_type=F32)
    o_ref[...] = acc + hf_ref[:, pl.ds(col, tn)]


def xattn(h, g, wqk, vo, *, seq, heads, mem, tm, tn):
    m, d = h.shape
    tm, tn = _tile(seq, tm), _tile(d, tn)
    per_b = seq // tm
    hm = heads * mem
    return pl.pallas_call(
        functools.partial(_xattn_kernel, heads=heads, mem=mem),
        grid=(m // tm, d // tn),
        in_specs=[pl.BlockSpec((tm, d), lambda i, j: (i, 0)),
                  pl.BlockSpec((1, d), lambda i, j: (0, 0)),
                  pl.BlockSpec((1, d, hm), lambda i, j: (i // per_b, 0, 0)),
                  pl.BlockSpec((1, hm, tn), lambda i, j: (i // per_b, 0, j))],
        out_specs=pl.BlockSpec((tm, tn), lambda i, j: (i, j)),
        out_shape=jax.ShapeDtypeStruct((m, d), F32),
        scratch_shapes=[pltpu.VMEM((tm, hm), BF16), pltpu.VMEM((tm, d), BF16)],
        compiler_params=_params("parallel", "arbitrary"),
        name="xattn",
    )(h, g.reshape(1, d).astype(F32), wqk, vo)


def _pad_cols(w, n):
    return jnp.pad(w, ((0, 0), (0, n - w.shape[1])))


def kernel(x, mem, positions, g_mix, w_in, w_fourier, g_q_lora, w_uq, g_kv_lora, w_ukv, g_fourier_out, g_mla_out, w_out, g_xattn, g_mem, w_xq, w_xk, w_xv, w_xo, g_mlp, w_ff1, w_ff2, g_final):
    batch, seq, d = x.shape
    depth = g_mix.shape[0]
    groups, gdim = w_fourier.shape[1], w_fourier.shape[2]
    fw = groups * gdim
    q_rank, heads, qk_dim = w_uq.shape[1], w_uq.shape[2], w_uq.shape[3]
    kv_rank = w_ukv.shape[1]
    nope = LANES
    rope = qk_dim - nope
    vdim = w_ukv.shape[3] - nope
    assert rope == LANES // 2 and vdim == LANES and fw % q_rank == 0
    assert (fw + q_rank) % kv_rank == 0 and (fw + q_rank + kv_rank) % LANES == 0
    xheads, xhd = w_xq.shape[2], w_xq.shape[3]
    mtok = mem.shape[1]
    t = batch * seq
    s2 = FOURIER_SEQ_INNER
    s1 = seq // s2

    half = rope // 2
    inv_freq = ROPE_THETA ** (-jnp.arange(half, dtype=F32) / half)
    ang = positions.astype(F32).reshape(t, 1) * inv_freq
    cos, sin = jnp.cos(ang), jnp.sin(ang)
    zq = jnp.zeros_like(cos)
    ct = jnp.concatenate([cos, cos, zq, zq], axis=1)
    sa = jnp.concatenate([zq, sin, zq, zq], axis=1)
    sb = jnp.concatenate([-sin, zq, zq, zq], axis=1)

    cs, tp, tq_tab, c2, s2m = _dft_tables(seq, gdim)

    in_width = w_in.shape[2]
    z_width = -(-(in_width + rope) // 768) * 768
    scale = float(qk_dim) ** -0.5 * math.log2(math.e)

    h = x.reshape(t, d)
    mem2 = mem.reshape(batch * mtok, d)
    for layer in range(depth):
        w_in_b = _pad_cols(w_in[layer], z_width).astype(BF16)
        wq = jnp.pad(w_uq[layer], ((0, 0), (0, 0), (0, 2 * LANES - qk_dim))).reshape(q_rank, heads * 2 * LANES).astype(BF16)
        wkv = jnp.concatenate([w_ukv[layer][:, :, :nope].reshape(kv_rank, heads * nope),
                               w_ukv[layer][:, :, nope:].reshape(kv_rank, heads * vdim)], axis=1).astype(BF16)
        wf = w_fourier[layer].astype(BF16)
        w_out_b = w_out[layer].astype(BF16)
        wxq = w_xq[layer].reshape(d, xheads * xhd).astype(BF16)
        wxkv = jnp.concatenate([w_xk[layer].reshape(d, xheads * xhd), w_xv[layer].reshape(d, xheads * xhd)], axis=1).astype(BF16)
        wxo = w_xo[layer].reshape(xheads * xhd, d).astype(BF16)
        w1 = w_ff1[layer].astype(BF16)
        w2 = w_ff2[layer].astype(BF16)

        z = norm_mm(h, g_mix[layer], w_in_b, tm=512, tn=768, out_dtype=BF16)
        p, q = chan_dft(z, cs, width=fw, groups=groups, tm=1024)
        a = seq_dft1(p.reshape(batch, s1, s2 * fw), q.reshape(batch, s1, s2 * fw), tp, tq_tab, width=fw, nb=8)
        f = seq_dft2(a.reshape(batch, 2 * s1, s2, fw), c2, s2m, s1=s1, kb=4).reshape(t, fw)
        qh = q_proj(z, g_q_lora[layer], wq, ct, sa, sb, col_block=fw // q_rank, rank=q_rank,
                    heads=heads, scale=scale, tm=256)
        kh, vh = kv_proj(z, g_kv_lora[layer], wkv, ct, sa, sb, ckv_block=(fw + q_rank) // kv_rank,
                         kr_block=(fw + q_rank + kv_rank) // LANES, rank=kv_rank, heads=heads, tm=512)
        o = flash_attention(qh.reshape(batch, seq, -1), kh.reshape(batch, seq, -1), vh.reshape(batch, seq, -1),
                            heads=heads, tq=1024, tk=512).reshape(t, heads * vdim)
        h = out_proj(f, wf, g_fourier_out[layer], o, g_mla_out[layer], w_out_b, h, tm=1024, tn=512)

        kvm = norm_mm(mem2, g_mem[layer], wxkv, tm=512, tn=512, out_dtype=BF16)
        wqk = absorb_qk(wxq, kvm, batch=batch, heads=xheads, mem=mtok, hd=xhd, scale=float(xhd) ** -0.5)
        vo = absorb_vo(kvm, wxo, batch=batch, heads=xheads, mem=mtok, hd=xhd, tn=2048)
        h = xattn(h, g_xattn[layer], wqk, vo, seq=seq, heads=xheads, mem=mtok, tm=512, tn=1024)

        act = norm_mm(h, g_mlp[layer], w1, tm=512, tn=1024, out_dtype=BF16, act="relu2")
        if layer == depth - 1:
            h = mm_res(act, w2, h, tm=512, tn=d, tk=1024, norm_gain=g_final)
        else:
            h = mm_res(act, w2, h, tm=1024, tn=1024, tk=2048)

    return h.astype(x.dtype).reshape(batch, seq, d)
```

```python
import functools
import math

import numpy as np
import jax
import jax.numpy as jnp
from jax import lax
from jax.experimental import pallas as pl
from jax.experimental.pallas import tpu as pltpu

F32 = jnp.float32
BF16 = jnp.bfloat16

NORM_EPS = 1e-6
ROPE_THETA = 10000.0
LANES = 128
VMEM_LIMIT_BYTES = 56 * 1024 * 1024
FOURIER_SEQ_INNER = 128


def _params(*sem):
    return pltpu.CompilerParams(dimension_semantics=sem, vmem_limit_bytes=VMEM_LIMIT_BYTES)


def _rms(xf, g):
    ms = jnp.mean(xf * xf, axis=-1, keepdims=True)
    return xf * lax.rsqrt(ms + NORM_EPS) * g


def _rms_rows(a_ref, g_ref, o_ref, chunk=64):
    rows = a_ref.shape[0]
    chunk = min(chunk, rows)

    def body(c, carry):
        r = pl.ds(pl.multiple_of(c * chunk, chunk), chunk)
        o_ref[r, :] = _rms(a_ref[r, :].astype(F32), g_ref[...]).astype(o_ref.dtype)
        return carry

    lax.fori_loop(0, rows // chunk, body, 0)


def _tile(dim, pref):
    t = min(dim, pref)
    assert dim % t == 0, (dim, pref)
    return t


def _norm_mm_kernel(a_ref, g_ref, w_ref, o_ref, xn_ref):
    @pl.when(pl.program_id(1) == 0)
    def _():
        _rms_rows(a_ref, g_ref, xn_ref)

    o_ref[...] = jnp.dot(xn_ref[...], w_ref[...], preferred_element_type=F32).astype(o_ref.dtype)


def norm_mm(a, g, w, *, tm, tn, out_dtype):
    m, k = a.shape
    n = w.shape[1]
    tm, tn = _tile(m, tm), _tile(n, tn)
    return pl.pallas_call(
        _norm_mm_kernel,
        grid=(m // tm, n // tn),
        in_specs=[pl.BlockSpec((tm, k), lambda i, j: (i, 0)),
                  pl.BlockSpec((1, k), lambda i, j: (0, 0)),
                  pl.BlockSpec((k, tn), lambda i, j: (0, j))],
        out_specs=pl.BlockSpec((tm, tn), lambda i, j: (i, j)),
        out_shape=jax.ShapeDtypeStruct((m, n), out_dtype),
        scratch_shapes=[pltpu.VMEM((tm, k), BF16)],
        compiler_params=_params("parallel", "arbitrary"),
        name="norm_mm",
    )(a, g.reshape(1, k).astype(F32), w)


def _mm_relu2_kernel(a_ref, w_ref, o_ref):
    acc = jnp.dot(a_ref[...], w_ref[...], preferred_element_type=F32)
    o_ref[...] = jnp.square(jnp.maximum(acc, 0.0)).astype(o_ref.dtype)


def mm_relu2(a, w, *, tm, tn, out_dtype):
    m, k = a.shape
    n = w.shape[1]
    tm, tn = _tile(m, tm), _tile(n, tn)
    return pl.pallas_call(
        _mm_relu2_kernel,
        grid=(m // tm, n // tn),
        in_specs=[pl.BlockSpec((tm, k), lambda i, j: (i, 0)),
                  pl.BlockSpec((k, tn), lambda i, j: (0, j))],
        out_specs=pl.BlockSpec((tm, tn), lambda i, j: (i, j)),
        out_shape=jax.ShapeDtypeStruct((m, n), out_dtype),
        compiler_params=_params("parallel", "parallel"),
        name="mm_relu2",
    )(a, w)


def _mm_res_kernel(a_ref, w_ref, res_ref, *rest, final_norm):
    if final_norm:
        g_ref, o_ref = rest
    else:
        (o_ref,) = rest
    kk = pl.program_id(2)

    @pl.when(kk == 0)
    def _():
        o_ref[...] = res_ref[...]

    o_ref[...] += jnp.dot(a_ref[...], w_ref[...], preferred_element_type=F32)

    if final_norm:
        @pl.when(kk == pl.num_programs(2) - 1)
        def _():
            _rms_rows(o_ref, g_ref, o_ref)


def mm_res(a, w, res, *, tm, tn, tk, norm_gain=None):
    m, k = a.shape
    n = w.shape[1]
    tm, tn, tk = _tile(m, tm), _tile(n, tn), _tile(k, tk)
    in_specs = [
        pl.BlockSpec((tm, tk), lambda i, j, kk: (i, kk)),
        pl.BlockSpec((tk, tn), lambda i, j, kk: (kk, j)),
        pl.BlockSpec((tm, tn), lambda i, j, kk: (i, j), pipeline_mode=pl.Buffered(1)),
    ]
    args = [a, w, res]
    if norm_gain is not None:
        assert tn == n
        in_specs.append(pl.BlockSpec((1, n), lambda i, j, kk: (0, 0)))
        args.append(norm_gain.reshape(1, n).astype(F32))
    return pl.pallas_call(
        functools.partial(_mm_res_kernel, final_norm=norm_gain is not None),
        grid=(m // tm, n // tn, k // tk),
        in_specs=in_specs,
        out_specs=pl.BlockSpec((tm, tn), lambda i, j, kk: (i, j)),
        out_shape=jax.ShapeDtypeStruct((m, n), F32),
        compiler_params=_params("parallel", "parallel", "arbitrary"),
        name="mm_res",
    )(*args)


def _chan_dft_kernel(z_ref, cs_ref, p_ref, q_ref, *, groups, gdim):
    for g in range(groups):
        pq = jnp.dot(z_ref[:, g * gdim:(g + 1) * gdim], cs_ref[...], preferred_element_type=F32)
        p_ref[:, g * gdim:(g + 1) * gdim] = pq[:, :gdim].astype(p_ref.dtype)
        q_ref[:, g * gdim:(g + 1) * gdim] = pq[:, gdim:].astype(q_ref.dtype)


def chan_dft(z, cs, *, width, groups, tm):
    m = z.shape[0]
    gdim = width // groups
    tm = _tile(m, tm)
    out = jax.ShapeDtypeStruct((m, width), BF16)
    return pl.pallas_call(
        functools.partial(_chan_dft_kernel, groups=groups, gdim=gdim),
        grid=(m // tm,),
        in_specs=[pl.BlockSpec((tm, width), lambda i: (i, 0)),
                  pl.BlockSpec((gdim, 2 * gdim), lambda i: (0, 0))],
        out_specs=[pl.BlockSpec((tm, width), lambda i: (i, 0))] * 2,
        out_shape=[out, out],
        compiler_params=_params("parallel"),
        name="chan_dft",
    )(z, cs)


def _seq_dft1_kernel(p_ref, q_ref, tp_ref, tq_ref, o_ref, *, nb, width):
    for t in range(nb):
        acc = jnp.dot(tp_ref[t], p_ref[0, t], preferred_element_type=F32)
        acc += jnp.dot(tq_ref[t], q_ref[0, t], preferred_element_type=F32)
        o_ref[0, :, t * width:(t + 1) * width] = acc.astype(o_ref.dtype)


def seq_dft1(pt, qt, tp, tq, *, nb):
    b, s2, s1, width = pt.shape
    nb = _tile(s2, nb)
    dspec = pl.BlockSpec((1, nb, s1, width), lambda bi, j: (bi, j, 0, 0))
    tspec = pl.BlockSpec((nb, 2 * s1, s1), lambda bi, j: (j, 0, 0))
    return pl.pallas_call(
        functools.partial(_seq_dft1_kernel, nb=nb, width=width),
        grid=(b, s2 // nb),
        in_specs=[dspec, dspec, tspec, tspec],
        out_specs=pl.BlockSpec((1, 2 * s1, nb * width), lambda bi, j: (bi, 0, j)),
        out_shape=jax.ShapeDtypeStruct((b, 2 * s1, s2 * width), BF16),
        compiler_params=_params("parallel", "parallel"),
        name="seq_dft1",
    )(pt, qt, tp, tq)


def _seq_dft2_kernel(re_ref, im_ref, c_ref, s_ref, o_ref, *, kb, width):
    for t in range(kb):
        acc = jnp.dot(c_ref[...], re_ref[0, t], preferred_element_type=F32)
        acc += jnp.dot(s_ref[...], im_ref[0, t], preferred_element_type=F32)
        o_ref[0, :, t * width:(t + 1) * width] = acc.astype(o_ref.dtype)


def seq_dft2(a4, c2, s2m, *, s1, kb):
    b, _, s2, width = a4.shape
    kb = _tile(s1, kb)
    nblk = s1 // kb
    return pl.pallas_call(
        functools.partial(_seq_dft2_kernel, kb=kb, width=width),
        grid=(b, nblk),
        in_specs=[
            pl.BlockSpec((1, kb, s2, width), lambda bi, j: (bi, j, 0, 0)),
            pl.BlockSpec((1, kb, s2, width), lambda bi, j: (bi, j + nblk, 0, 0)),
            pl.BlockSpec((s2, s2), lambda bi, j: (0, 0)),
            pl.BlockSpec((s2, s2), lambda bi, j: (0, 0)),
        ],
        out_specs=pl.BlockSpec((1, s2, kb * width), lambda bi, j: (bi, 0, j)),
        out_shape=jax.ShapeDtypeStruct((b, s2, s1 * width), BF16),
        compiler_params=_params("parallel", "parallel"),
        name="seq_dft2",
    )(a4, a4, c2, s2m)


def _dft_tables(seq, gdim):
    s2 = FOURIER_SEQ_INNER
    s1 = seq // s2
    c = np.arange(gdim)
    ang = 2.0 * np.pi * ((c[:, None] * c[None, :]) % gdim) / gdim
    cs = np.concatenate([np.cos(ang), np.sin(ang)], axis=1) / math.sqrt(gdim)
    k1 = np.arange(s1)
    n1 = np.arange(s1)
    n2 = np.arange(s2)
    idx = (k1[None, :, None] * (s2 * n1[None, None, :] + n2[:, None, None])) % seq
    phi = 2.0 * np.pi * idx / seq
    co, si = np.cos(phi), np.sin(phi)
    tp = np.concatenate([co, -si], axis=1)
    tq = np.concatenate([-si, -co], axis=1)
    k2 = np.arange(s2)
    ang2 = 2.0 * np.pi * ((k2[:, None] * n2[None, :]) % s2) / s2
    c2 = np.cos(ang2) / math.sqrt(seq)
    s2m = np.sin(ang2) / math.sqrt(seq)
    as_bf16 = lambda x: jnp.asarray(x, dtype=F32).astype(BF16)
    return as_bf16(cs), as_bf16(tp), as_bf16(tq), as_bf16(c2), as_bf16(s2m)


def _rope128(r, ct, sa, sb):
    quarter = LANES // 4
    return r * ct + pltpu.roll(r, quarter, 1) * sa + pltpu.roll(r, LANES - quarter, 1) * sb


def _q_proj_kernel(c_ref, g_ref, w_ref, ct_ref, sa_ref, sb_ref, o_ref, *, heads, scale):
    xn = _rms(c_ref[...].astype(F32), g_ref[...]).astype(BF16)
    acc = jnp.dot(xn, w_ref[...], preferred_element_type=F32)
    ct, sa, sb = ct_ref[...], sa_ref[...], sb_ref[...]
    for h in range(heads):
        base = 2 * LANES * h
        o_ref[:, base:base + LANES] = (acc[:, base:base + LANES] * scale).astype(o_ref.dtype)
        r = _rope128(acc[:, base + LANES:base + 2 * LANES], ct, sa, sb)
        o_ref[:, base + LANES:base + 2 * LANES] = (r * scale).astype(o_ref.dtype)


def q_proj(z, g, w, ct, sa, sb, *, col_block, rank, heads, scale, tm):
    m = z.shape[0]
    tm = _tile(m, tm)
    n = w.shape[1]
    tab = pl.BlockSpec((tm, LANES), lambda i: (i, 0))
    return pl.pallas_call(
        functools.partial(_q_proj_kernel, heads=heads, scale=scale),
        grid=(m // tm,),
        in_specs=[pl.BlockSpec((tm, rank), lambda i: (i, col_block)),
                  pl.BlockSpec((1, rank), lambda i: (0, 0)),
                  pl.BlockSpec((rank, n), lambda i: (0, 0)),
                  tab, tab, tab],
        out_specs=pl.BlockSpec((tm, n), lambda i: (i, 0)),
        out_shape=jax.ShapeDtypeStruct((m, n), BF16),
        compiler_params=_params("parallel"),
        name="q_proj",
    )(z, g.reshape(1, rank).astype(F32), w, ct, sa, sb)


def _kv_proj_kernel(c_ref, kr_ref, g_ref, w_ref, ct_ref, sa_ref, sb_ref, k_ref, v_ref, *, heads):
    xn = _rms(c_ref[...].astype(F32), g_ref[...]).astype(BF16)
    acc = jnp.dot(xn, w_ref[...], preferred_element_type=F32)
    kr = _rope128(kr_ref[...].astype(F32), ct_ref[...], sa_ref[...], sb_ref[...]).astype(k_ref.dtype)
    for h in range(heads):
        k_ref[:, 2 * LANES * h:2 * LANES * h + LANES] = acc[:, LANES * h:LANES * (h + 1)].astype(k_ref.dtype)
        k_ref[:, 2 * LANES * h + LANES:2 * LANES * (h + 1)] = kr
        v_ref[:, 2 * LANES * h:2 * LANES * h + LANES] = acc[:, LANES * (heads + h):LANES * (heads + h + 1)].astype(v_ref.dtype)
        v_ref[:, 2 * LANES * h + LANES:2 * LANES * (h + 1)] = jnp.ones((acc.shape[0], LANES), v_ref.dtype)


def kv_proj(z, g, w, ct, sa, sb, *, ckv_block, kr_block, rank, heads, tm):
    m = z.shape[0]
    tm = _tile(m, tm)
    tab = pl.BlockSpec((tm, LANES), lambda i: (i, 0))
    return pl.pallas_call(
        functools.partial(_kv_proj_kernel, heads=heads),
        grid=(m // tm,),
        in_specs=[pl.BlockSpec((tm, rank), lambda i: (i, ckv_block)),
                  pl.BlockSpec((tm, LANES), lambda i: (i, kr_block)),
                  pl.BlockSpec((1, rank), lambda i: (0, 0)),
                  pl.BlockSpec((rank, 2 * LANES * heads), lambda i: (0, 0)),
                  tab, tab, tab],
        out_specs=[pl.BlockSpec((tm, 2 * LANES * heads), lambda i: (i, 0))] * 2,
        out_shape=[jax.ShapeDtypeStruct((m, 2 * LANES * heads), BF16)] * 2,
        compiler_params=_params("parallel"),
        name="kv_proj",
    )(z, z, g.reshape(1, rank).astype(F32), w, ct, sa, sb)


def _flash_kernel(q_ref, k_ref, v_ref, o_ref, s0_ref, s1_ref, *, tk, nk):
    q = q_ref[0]
    tq = q.shape[0]
    dv = o_ref.shape[-1]

    def chunk(c):
        start = c * tk
        return pl.ds(start if isinstance(start, int) else pl.multiple_of(start, tk), tk)

    def scores(c):
        return lax.dot_general(q, k_ref[0, chunk(c), :], (((1,), (1,)), ((), ())), preferred_element_type=F32)

    def fold(s_ref, c, m, acc):
        s = s_ref[...]
        m_new = jnp.maximum(m, jnp.max(s, axis=1, keepdims=True))
        p = jnp.exp2(s - m_new).astype(BF16)
        acc = jnp.exp2(m - m_new) * acc + jnp.dot(p, v_ref[0, chunk(c), :], preferred_element_type=F32)
        return m_new, acc

    def pair(j, carry, last):
        m, acc = carry
        s1_ref[...] = scores(2 * j + 1)
        m, acc = fold(s0_ref, 2 * j, m, acc)
        if not last:
            s0_ref[...] = scores(2 * j + 2)
        return fold(s1_ref, 2 * j + 1, m, acc)

    s0_ref[...] = scores(0)
    init = (jnp.full((tq, 1), -jnp.inf, F32), jnp.zeros((tq, 2 * dv), F32))
    carry = init
    for j in range(nk // 2):
        carry = pair(j, carry, j == nk // 2 - 1)
    _, acc = carry
    o_ref[0] = (acc[:, :dv] / acc[:, dv:]).astype(o_ref.dtype)


def flash_attention(q, k, v, *, heads, tq, tk):
    b, s, _ = q.shape
    dk = q.shape[-1] // heads
    dv2 = v.shape[-1] // heads
    dv = dv2 // 2
    tq, tk = _tile(s, tq), _tile(s, min(tk, s // 2))
    nk = s // tk
    assert nk % 2 == 0
    return pl.pallas_call(
        functools.partial(_flash_kernel, tk=tk, nk=nk),
        grid=(b, heads, s // tq),
        in_specs=[pl.BlockSpec((1, tq, dk), lambda bi, h, i: (bi, i, h)),
                  pl.BlockSpec((1, s, dk), lambda bi, h, i: (bi, 0, h)),
                  pl.BlockSpec((1, s, dv2), lambda bi, h, i: (bi, 0, h))],
        out_specs=pl.BlockSpec((1, tq, dv), lambda bi, h, i: (bi, i, h)),
        out_shape=jax.ShapeDtypeStruct((b, s, heads * dv), BF16),
        scratch_shapes=[pltpu.VMEM((tq, tk), F32), pltpu.VMEM((tq, tk), F32)],
        compiler_params=_params("parallel", "parallel", "arbitrary"),
        name="flash_attention",
    )(q, k, v)


def _out_proj_kernel(f_ref, wf_ref, gf_ref, a_ref, ga_ref, w_ref, x_ref, o_ref, y_ref, *, groups, gdim):
    fw = groups * gdim

    @pl.when(pl.program_id(1) == 0)
    def _():
        rows = f_ref.shape[0]
        chunk = min(rows, 256)

        def body(c, carry):
            r = pl.ds(pl.multiple_of(c * chunk, chunk), chunk)
            parts = [jnp.dot(f_ref[r, g * gdim:(g + 1) * gdim], wf_ref[g], preferred_element_type=F32)
                     for g in range(groups)]
            ms = sum(jnp.sum(p * p, axis=-1, keepdims=True) for p in parts) / fw
            inv = lax.rsqrt(ms + NORM_EPS)
            for g in range(groups):
                sl = slice(g * gdim, (g + 1) * gdim)
                y_ref[r, sl] = (parts[g] * inv * gf_ref[:, sl]).astype(BF16)
            return carry

        lax.fori_loop(0, rows // chunk, body, 0)
        _rms_rows(a_ref, ga_ref, y_ref.at[:, fw:])

    acc = jnp.dot(y_ref[...], w_ref[...], preferred_element_type=F32)
    o_ref[...] = (acc + x_ref[...]).astype(o_ref.dtype)


def out_proj(f, wf, gf, a, ga, w, x, *, tm, tn):
    m, fw = f.shape
    aw = a.shape[1]
    groups, gdim, _ = wf.shape
    n = w.shape[1]
    tm, tn = _tile(m, tm), _tile(n, tn)
    return pl.pallas_call(
        functools.partial(_out_proj_kernel, groups=groups, gdim=gdim),
        grid=(m // tm, n // tn),
        in_specs=[pl.BlockSpec((tm, fw), lambda i, j: (i, 0)),
                  pl.BlockSpec((groups, gdim, gdim), lambda i, j: (0, 0, 0)),
                  pl.BlockSpec((1, fw), lambda i, j: (0, 0)),
                  pl.BlockSpec((tm, aw), lambda i, j: (i, 0)),
                  pl.BlockSpec((1, aw), lambda i, j: (0, 0)),
                  pl.BlockSpec((fw + aw, tn), lambda i, j: (0, j)),
                  pl.BlockSpec((tm, tn), lambda i, j: (i, j))],
        out_specs=pl.BlockSpec((tm, tn), lambda i, j: (i, j)),
        out_shape=jax.ShapeDtypeStruct((m, n), F32),
        scratch_shapes=[pltpu.VMEM((tm, fw + aw), BF16)],
        compiler_params=_params("parallel", "arbitrary"),
        name="out_proj",
    )(f, wf, gf.reshape(1, fw).astype(F32), a, ga.reshape(1, aw).astype(F32), w, x)


def _absorb_qk_kernel(wq_ref, k_ref, o_ref, *, scale):
    acc = lax.dot_general(wq_ref[...], k_ref[...], (((1,), (1,)), ((), ())), preferred_element_type=F32)
    o_ref[0] = (acc * scale).astype(o_ref.dtype)


def absorb_qk(wq, kv, *, batch, heads, mem, hd, scale):
    d = wq.shape[0]
    return pl.pallas_call(
        functools.partial(_absorb_qk_kernel, scale=scale),
        grid=(heads, batch),
        in_specs=[pl.BlockSpec((d, hd), lambda h, b: (0, h)),
                  pl.BlockSpec((mem, hd), lambda h, b: (b, h))],
        out_specs=pl.BlockSpec((1, d, mem), lambda h, b: (b, 0, h)),
        out_shape=jax.ShapeDtypeStruct((batch, d, heads * mem), BF16),
        compiler_params=_params("parallel", "parallel"),
        name="absorb_qk",
    )(wq, kv)


def _absorb_vo_kernel(v_ref, wo_ref, o_ref):
    o_ref[0] = jnp.dot(v_ref[...], wo_ref[...], preferred_element_type=F32).astype(o_ref.dtype)


def absorb_vo(kv, wo, *, batch, heads, mem, hd, tn):
    d = wo.shape[1]
    tn = _tile(d, tn)
    return pl.pallas_call(
        _absorb_vo_kernel,
        grid=(heads, d // tn, batch),
        in_specs=[pl.BlockSpec((mem, hd), lambda h, j, b: (b, heads + h)),
                  pl.BlockSpec((hd, tn), lambda h, j, b: (h, j))],
        out_specs=pl.BlockSpec((1, mem, tn), lambda h, j, b: (b, h, j)),
        out_shape=jax.ShapeDtypeStruct((batch, heads * mem, d), BF16),
        compiler_params=_params("parallel", "parallel", "parallel"),
        name="absorb_vo",
    )(kv, wo)


def _xattn_kernel(hf_ref, g_ref, wqk_ref, vo_ref, g2_ref, o_ref, on_ref, p_ref, hn_ref, *, heads, mem):
    _rms_rows(hf_ref, g_ref, hn_ref)
    s = jnp.dot(hn_ref[...], wqk_ref[0], preferred_element_type=F32)
    for h in range(heads):
        seg = s[:, h * mem:(h + 1) * mem]
        e = jnp.exp(seg - jnp.max(seg, axis=-1, keepdims=True))
        p_ref[:, h * mem:(h + 1) * mem] = (e / jnp.sum(e, axis=-1, keepdims=True)).astype(BF16)
    o_ref[...] = jnp.dot(p_ref[...], vo_ref[0], preferred_element_type=F32) + hf_ref[...]
    _rms_rows(o_ref, g2_ref, on_ref)


def xattn(h, g, wqk, vo, g_next, *, seq, heads, mem, tm):
    m, d = h.shape
    tm = _tile(seq, tm)
    per_b = seq // tm
    hm = heads * mem
    once = pl.Buffered(1)
    return pl.pallas_call(
        functools.partial(_xattn_kernel, heads=heads, mem=mem),
        grid=(m // tm,),
        in_specs=[pl.BlockSpec((tm, d), lambda i: (i, 0)),
                  pl.BlockSpec((1, d), lambda i: (0, 0)),
                  pl.BlockSpec((1, d, hm), lambda i: (i // per_b, 0, 0), pipeline_mode=once),
                  pl.BlockSpec((1, hm, d), lambda i: (i // per_b, 0, 0), pipeline_mode=once),
                  pl.BlockSpec((1, d), lambda i: (0, 0))],
        out_specs=[pl.BlockSpec((tm, d), lambda i: (i, 0))] * 2,
        out_shape=[jax.ShapeDtypeStruct((m, d), F32), jax.ShapeDtypeStruct((m, d), BF16)],
        scratch_shapes=[pltpu.VMEM((tm, hm), BF16), pltpu.VMEM((tm, d), BF16)],
        compiler_params=_params("parallel"),
        name="xattn",
    )(h, g.reshape(1, d).astype(F32), wqk, vo, g_next.reshape(1, d).astype(F32))


def _pad_cols(w, n):
    return jnp.pad(w, ((0, 0), (0, n - w.shape[1])))


def kernel(x, mem, positions, g_mix, w_in, w_fourier, g_q_lora, w_uq, g_kv_lora, w_ukv, g_fourier_out, g_mla_out, w_out, g_xattn, g_mem, w_xq, w_xk, w_xv, w_xo, g_mlp, w_ff1, w_ff2, g_final):
    batch, seq, d = x.shape
    depth = g_mix.shape[0]
    groups, gdim = w_fourier.shape[1], w_fourier.shape[2]
    fw = groups * gdim
    q_rank, heads, qk_dim = w_uq.shape[1], w_uq.shape[2], w_uq.shape[3]
    kv_rank = w_ukv.shape[1]
    nope = LANES
    rope = qk_dim - nope
    vdim = w_ukv.shape[3] - nope
    assert rope == LANES // 2 and vdim == LANES and fw % q_rank == 0
    assert (fw + q_rank) % kv_rank == 0 and (fw + q_rank + kv_rank) % LANES == 0
    xheads, xhd = w_xq.shape[2], w_xq.shape[3]
    mtok = mem.shape[1]
    t = batch * seq
    s2 = FOURIER_SEQ_INNER
    s1 = seq // s2

    half = rope // 2
    inv_freq = ROPE_THETA ** (-jnp.arange(half, dtype=F32) / half)
    ang = positions.astype(F32).reshape(t, 1) * inv_freq
    cos, sin = jnp.cos(ang), jnp.sin(ang)
    zq = jnp.zeros_like(cos)
    ct = jnp.concatenate([cos, cos, zq, zq], axis=1)
    sa = jnp.concatenate([zq, sin, zq, zq], axis=1)
    sb = jnp.concatenate([-sin, zq, zq, zq], axis=1)

    cs, tp, tq_tab, c2, s2m = _dft_tables(seq, gdim)

    in_width = w_in.shape[2]
    z_width = -(-(in_width + rope) // 768) * 768
    scale = float(qk_dim) ** -0.5 * math.log2(math.e)

    h = x.reshape(t, d)
    mem2 = mem.reshape(batch * mtok, d)
    for layer in range(depth):
        w_in_b = _pad_cols(w_in[layer], z_width).astype(BF16)
        wq = jnp.pad(w_uq[layer], ((0, 0), (0, 0), (0, 2 * LANES - qk_dim))).reshape(q_rank, heads * 2 * LANES).astype(BF16)
        wkv = jnp.concatenate([w_ukv[layer][:, :, :nope].reshape(kv_rank, heads * nope),
                               w_ukv[layer][:, :, nope:].reshape(kv_rank, heads * vdim)], axis=1).astype(BF16)
        wf = w_fourier[layer].astype(BF16)
        w_out_b = w_out[layer].astype(BF16)
        wxq = w_xq[layer].reshape(d, xheads * xhd).astype(BF16)
        wxkv = jnp.concatenate([w_xk[layer].reshape(d, xheads * xhd), w_xv[layer].reshape(d, xheads * xhd)], axis=1).astype(BF16)
        wxo = w_xo[layer].reshape(xheads * xhd, d).astype(BF16)
        w1 = w_ff1[layer].astype(BF16)
        w2 = w_ff2[layer].astype(BF16)

        z = norm_mm(h, g_mix[layer], w_in_b, tm=512, tn=768, out_dtype=BF16)
        p, q = chan_dft(z, cs, width=fw, groups=groups, tm=1024)
        pt = p.reshape(batch, s1, s2, fw).transpose(0, 2, 1, 3)
        qt = q.reshape(batch, s1, s2, fw).transpose(0, 2, 1, 3)
        a = seq_dft1(pt, qt, tp, tq_tab, nb=8)
        f = seq_dft2(a.reshape(batch, 2 * s1, s2, fw), c2, s2m, s1=s1, kb=4).reshape(t, fw)
        qh = q_proj(z, g_q_lora[layer], wq, ct, sa, sb, col_block=fw // q_rank, rank=q_rank,
                    heads=heads, scale=scale, tm=256)
        kh, vh = kv_proj(z, g_kv_lora[layer], wkv, ct, sa, sb, ckv_block=(fw + q_rank) // kv_rank,
                         kr_block=(fw + q_rank + kv_rank) // LANES, rank=kv_rank, heads=heads, tm=512)
        o = flash_attention(qh.reshape(batch, seq, -1), kh.reshape(batch, seq, -1), vh.reshape(batch, seq, -1),
                            heads=heads, tq=1024, tk=512).reshape(t, heads * vdim)
        h = out_proj(f, wf, g_fourier_out[layer], o, g_mla_out[layer], w_out_b, h, tm=1024, tn=512)

        kvm = norm_mm(mem2, g_mem[layer], wxkv, tm=512, tn=512, out_dtype=BF16)
        wqk = absorb_qk(wxq, kvm, batch=batch, heads=xheads, mem=mtok, hd=xhd, scale=float(xhd) ** -0.5)
        vo = absorb_vo(kvm, wxo, batch=batch, heads=xheads, mem=mtok, hd=xhd, tn=2048)
        h, hn = xattn(h, g_xattn[layer], wqk, vo, g_mlp[layer], seq=seq, heads=xheads, mem=mtok, tm=256)

        act = mm_relu2(hn, w1, tm=1024, tn=1024, out_dtype=BF16)
        if layer == depth - 1:
            h = mm_res(act, w2, h, tm=512, tn=d, tk=1024, norm_gain=g_final)
        else:
            h = mm_res(act, w2, h, tm=1024, tn=1024, tk=2048)

    return h.astype(x.dtype).reshape(batch, seq, d)
```

```python
import functools
import math

import numpy as np
import jax
import jax.numpy as jnp
from jax import lax
from jax.experimental import pallas as pl
from jax.experimental.pallas import tpu as pltpu

F32 = jnp.float32
BF16 = jnp.bfloat16

NORM_EPS = 1e-6
ROPE_THETA = 10000.0
LANES = 128
VMEM_LIMIT_BYTES = 56 * 1024 * 1024
FOURIER_SEQ_INNER = 128


def _params(*sem):
    return pltpu.CompilerParams(dimension_semantics=sem, vmem_limit_bytes=VMEM_LIMIT_BYTES)


def _rms(xf, g):
    ms = jnp.mean(xf * xf, axis=-1, keepdims=True)
    return xf * lax.rsqrt(ms + NORM_EPS) * g


def _rms_rows(a_ref, g_ref, o_ref, chunk=64):
    rows = a_ref.shape[0]
    chunk = min(chunk, rows)

    def body(c, carry):
        r = pl.ds(pl.multiple_of(c * chunk, chunk), chunk)
        o_ref[r, :] = _rms(a_ref[r, :].astype(F32), g_ref[...]).astype(o_ref.dtype)
        return carry

    lax.fori_loop(0, rows // chunk, body, 0)


def _tile(dim, pref):
    t = min(dim, pref)
    assert dim % t == 0, (dim, pref)
    return t


def _norm_mm_kernel(a_ref, g_ref, w_ref, o_ref, xn_ref):
    @pl.when(pl.program_id(1) == 0)
    def _():
        _rms_rows(a_ref, g_ref, xn_ref)

    o_ref[...] = jnp.dot(xn_ref[...], w_ref[...], preferred_element_type=F32).astype(o_ref.dtype)


def norm_mm(a, g, w, *, tm, tn, out_dtype):
    m, k = a.shape
    n = w.shape[1]
    tm, tn = _tile(m, tm), _tile(n, tn)
    return pl.pallas_call(
        _norm_mm_kernel,
        grid=(m // tm, n // tn),
        in_specs=[pl.BlockSpec((tm, k), lambda i, j: (i, 0)),
                  pl.BlockSpec((1, k), lambda i, j: (0, 0)),
                  pl.BlockSpec((k, tn), lambda i, j: (0, j))],
        out_specs=pl.BlockSpec((tm, tn), lambda i, j: (i, j)),
        out_shape=jax.ShapeDtypeStruct((m, n), out_dtype),
        scratch_shapes=[pltpu.VMEM((tm, k), BF16)],
        compiler_params=_params("parallel", "arbitrary"),
        name="norm_mm",
    )(a, g.reshape(1, k).astype(F32), w)


def _mm_relu2_kernel(a_ref, w_ref, o_ref):
    acc = jnp.dot(a_ref[...], w_ref[...], preferred_element_type=F32)
    o_ref[...] = jnp.square(jnp.maximum(acc, 0.0)).astype(o_ref.dtype)


def mm_relu2(a, w, *, tm, tn, out_dtype):
    m, k = a.shape
    n = w.shape[1]
    tm, tn = _tile(m, tm), _tile(n, tn)
    return pl.pallas_call(
        _mm_relu2_kernel,
        grid=(m // tm, n // tn),
        in_specs=[pl.BlockSpec((tm, k), lambda i, j: (i, 0)),
                  pl.BlockSpec((k, tn), lambda i, j: (0, j))],
        out_specs=pl.BlockSpec((tm, tn), lambda i, j: (i, j)),
        out_shape=jax.ShapeDtypeStruct((m, n), out_dtype),
        compiler_params=_params("parallel", "parallel"),
        name="mm_relu2",
    )(a, w)


def _mm_res_kernel(a_ref, w_ref, res_ref, o_ref):
    @pl.when(pl.program_id(2) == 0)
    def _():
        o_ref[...] = res_ref[...]

    o_ref[...] += jnp.dot(a_ref[...], w_ref[...], preferred_element_type=F32)


def mm_res(a, w, res, *, tm, tn, tk):
    m, k = a.shape
    n = w.shape[1]
    tm, tn, tk = _tile(m, tm), _tile(n, tn), _tile(k, tk)
    return pl.pallas_call(
        _mm_res_kernel,
        grid=(m // tm, n // tn, k // tk),
        in_specs=[pl.BlockSpec((tm, tk), lambda i, j, kk: (i, kk)),
                  pl.BlockSpec((tk, tn), lambda i, j, kk: (kk, j)),
                  pl.BlockSpec((tm, tn), lambda i, j, kk: (i, j))],
        out_specs=pl.BlockSpec((tm, tn), lambda i, j, kk: (i, j)),
        out_shape=jax.ShapeDtypeStruct((m, n), F32),
        compiler_params=_params("parallel", "parallel", "arbitrary"),
        name="mm_res",
    )(a, w, res)


def _rmsnorm_kernel(a_ref, g_ref, o_ref):
    _rms_rows(a_ref, g_ref, o_ref)


def rmsnorm(a, g, *, tm, out_dtype):
    m, k = a.shape
    tm = _tile(m, tm)
    return pl.pallas_call(
        _rmsnorm_kernel,
        grid=(m // tm,),
        in_specs=[pl.BlockSpec((tm, k), lambda i: (i, 0)), pl.BlockSpec((1, k), lambda i: (0, 0))],
        out_specs=pl.BlockSpec((tm, k), lambda i: (i, 0)),
        out_shape=jax.ShapeDtypeStruct((m, k), out_dtype),
        compiler_params=_params("parallel"),
        name="rmsnorm",
    )(a, g.reshape(1, k).astype(F32))


def _chan_dft_kernel(z_ref, cs_ref, p_ref, q_ref, *, groups, gdim):
    for g in range(groups):
        pq = jnp.dot(z_ref[:, g * gdim:(g + 1) * gdim], cs_ref[...], preferred_element_type=F32)
        p_ref[:, g * gdim:(g + 1) * gdim] = pq[:, :gdim].astype(p_ref.dtype)
        q_ref[:, g * gdim:(g + 1) * gdim] = pq[:, gdim:].astype(q_ref.dtype)


def chan_dft(z, cs, *, width, groups, tm):
    m = z.shape[0]
    gdim = width // groups
    tm = _tile(m, tm)
    out = jax.ShapeDtypeStruct((m, width), BF16)
    return pl.pallas_call(
        functools.partial(_chan_dft_kernel, groups=groups, gdim=gdim),
        grid=(m // tm,),
        in_specs=[pl.BlockSpec((tm, width), lambda i: (i, 0)),
                  pl.BlockSpec((gdim, 2 * gdim), lambda i: (0, 0))],
        out_specs=[pl.BlockSpec((tm, width), lambda i: (i, 0))] * 2,
        out_shape=[out, out],
        compiler_params=_params("parallel"),
        name="chan_dft",
    )(z, cs)


def _seq_dft1_kernel(p_ref, q_ref, tp_ref, tq_ref, o_ref, *, nb, width):
    for t in range(nb):
        acc = jnp.dot(tp_ref[t], p_ref[0, t], preferred_element_type=F32)
        acc += jnp.dot(tq_ref[t], q_ref[0, t], preferred_element_type=F32)
        o_ref[0, :, t * width:(t + 1) * width] = acc.astype(o_ref.dtype)


def seq_dft1(pt, qt, tp, tq, *, nb):
    b, s2, s1, width = pt.shape
    nb = _tile(s2, nb)
    dspec = pl.BlockSpec((1, nb, s1, width), lambda bi, j: (bi, j, 0, 0))
    tspec = pl.BlockSpec((nb, 2 * s1, s1), lambda bi, j: (j, 0, 0))
    return pl.pallas_call(
        functools.partial(_seq_dft1_kernel, nb=nb, width=width),
        grid=(b, s2 // nb),
        in_specs=[dspec, dspec, tspec, tspec],
        out_specs=pl.BlockSpec((1, 2 * s1, nb * width), lambda bi, j: (bi, 0, j)),
        out_shape=jax.ShapeDtypeStruct((b, 2 * s1, s2 * width), BF16),
        compiler_params=_params("parallel", "parallel"),
        name="seq_dft1",
    )(pt, qt, tp, tq)


def _seq_dft2_kernel(re_ref, im_ref, c_ref, s_ref, o_ref, *, kb, width):
    for t in range(kb):
        acc = jnp.dot(c_ref[...], re_ref[0, t], preferred_element_type=F32)
        acc += jnp.dot(s_ref[...], im_ref[0, t], preferred_element_type=F32)
        o_ref[0, :, t * width:(t + 1) * width] = acc.astype(o_ref.dtype)


def seq_dft2(a4, c2, s2m, *, s1, kb):
    b, _, s2, width = a4.shape
    kb = _tile(s1, kb)
    nblk = s1 // kb
    return pl.pallas_call(
        functools.partial(_seq_dft2_kernel, kb=kb, width=width),
        grid=(b, nblk),
        in_specs=[
            pl.BlockSpec((1, kb, s2, width), lambda bi, j: (bi, j, 0, 0)),
            pl.BlockSpec((1, kb, s2, width), lambda bi, j: (bi, j + nblk, 0, 0)),
            pl.BlockSpec((s2, s2), lambda bi, j: (0, 0)),
            pl.BlockSpec((s2, s2), lambda bi, j: (0, 0)),
        ],
        out_specs=pl.BlockSpec((1, s2, kb * width), lambda bi, j: (bi, 0, j)),
        out_shape=jax.ShapeDtypeStruct((b, s2, s1 * width), BF16),
        compiler_params=_params("parallel", "parallel"),
        name="seq_dft2",
    )(a4, a4, c2, s2m)


def _dft_tables(seq, gdim):
    s2 = FOURIER_SEQ_INNER
    s1 = seq // s2
    c = np.arange(gdim)
    ang = 2.0 * np.pi * ((c[:, None] * c[None, :]) % gdim) / gdim
    cs = np.concatenate([np.cos(ang), np.sin(ang)], axis=1) / math.sqrt(gdim)
    k1 = np.arange(s1)
    n1 = np.arange(s1)
    n2 = np.arange(s2)
    idx = (k1[None, :, None] * (s2 * n1[None, None, :] + n2[:, None, None])) % seq
    phi = 2.0 * np.pi * idx / seq
    co, si = np.cos(phi), np.sin(phi)
    tp = np.concatenate([co, -si], axis=1)
    tq = np.concatenate([-si, -co], axis=1)
    k2 = np.arange(s2)
    ang2 = 2.0 * np.pi * ((k2[:, None] * n2[None, :]) % s2) / s2
    c2 = np.cos(ang2) / math.sqrt(seq)
    s2m = np.sin(ang2) / math.sqrt(seq)
    as_bf16 = lambda x: jnp.asarray(x, dtype=F32).astype(BF16)
    return as_bf16(cs), as_bf16(tp), as_bf16(tq), as_bf16(c2), as_bf16(s2m)


def _rope128(r, ct, sa, sb):
    quarter = LANES // 4
    return r * ct + pltpu.roll(r, quarter, 1) * sa + pltpu.roll(r, LANES - quarter, 1) * sb


def _q_proj_kernel(c_ref, g_ref, w_ref, ct_ref, sa_ref, sb_ref, o_ref, *, heads, scale):
    xn = _rms(c_ref[...].astype(F32), g_ref[...]).astype(BF16)
    acc = jnp.dot(xn, w_ref[...], preferred_element_type=F32)
    ct, sa, sb = ct_ref[...], sa_ref[...], sb_ref[...]
    for h in range(heads):
        base = 2 * LANES * h
        o_ref[:, base:base + LANES] = (acc[:, base:base + LANES] * scale).astype(o_ref.dtype)
        r = _rope128(acc[:, base + LANES:base + 2 * LANES], ct, sa, sb)
        o_ref[:, base + LANES:base + 2 * LANES] = (r * scale).astype(o_ref.dtype)


def q_proj(z, g, w, ct, sa, sb, *, col_block, rank, heads, scale, tm):
    m = z.shape[0]
    tm = _tile(m, tm)
    n = w.shape[1]
    tab = pl.BlockSpec((tm, LANES), lambda i: (i, 0))
    return pl.pallas_call(
        functools.partial(_q_proj_kernel, heads=heads, scale=scale),
        grid=(m // tm,),
        in_specs=[pl.BlockSpec((tm, rank), lambda i: (i, col_block)),
                  pl.BlockSpec((1, rank), lambda i: (0, 0)),
                  pl.BlockSpec((rank, n), lambda i: (0, 0)),
                  tab, tab, tab],
        out_specs=pl.BlockSpec((tm, n), lambda i: (i, 0)),
        out_shape=jax.ShapeDtypeStruct((m, n), BF16),
        compiler_params=_params("parallel"),
        name="q_proj",
    )(z, g.reshape(1, rank).astype(F32), w, ct, sa, sb)


def _kv_proj_kernel(c_ref, kr_ref, g_ref, w_ref, ct_ref, sa_ref, sb_ref, k_ref, v_ref, *, heads):
    xn = _rms(c_ref[...].astype(F32), g_ref[...]).astype(BF16)
    acc = jnp.dot(xn, w_ref[...], preferred_element_type=F32)
    kr = _rope128(kr_ref[...].astype(F32), ct_ref[...], sa_ref[...], sb_ref[...]).astype(k_ref.dtype)
    for h in range(heads):
        k_ref[:, 2 * LANES * h:2 * LANES * h + LANES] = acc[:, LANES * h:LANES * (h + 1)].astype(k_ref.dtype)
        k_ref[:, 2 * LANES * h + LANES:2 * LANES * (h + 1)] = kr
        v_ref[:, 2 * LANES * h:2 * LANES * h + LANES] = acc[:, LANES * (heads + h):LANES * (heads + h + 1)].astype(v_ref.dtype)
        v_ref[:, 2 * LANES * h + LANES:2 * LANES * (h + 1)] = jnp.ones((acc.shape[0], LANES), v_ref.dtype)


def kv_proj(z, g, w, ct, sa, sb, *, ckv_block, kr_block, rank, heads, tm):
    m = z.shape[0]
    tm = _tile(m, tm)
    tab = pl.BlockSpec((tm, LANES), lambda i: (i, 0))
    return pl.pallas_call(
        functools.partial(_kv_proj_kernel, heads=heads),
        grid=(m // tm,),
        in_specs=[pl.BlockSpec((tm, rank), lambda i: (i, ckv_block)),
                  pl.BlockSpec((tm, LANES), lambda i: (i, kr_block)),
                  pl.BlockSpec((1, rank), lambda i: (0, 0)),
                  pl.BlockSpec((rank, 2 * LANES * heads), lambda i: (0, 0)),
                  tab, tab, tab],
        out_specs=[pl.BlockSpec((tm, 2 * LANES * heads), lambda i: (i, 0))] * 2,
        out_shape=[jax.ShapeDtypeStruct((m, 2 * LANES * heads), BF16)] * 2,
        compiler_params=_params("parallel"),
        name="kv_proj",
    )(z, z, g.reshape(1, rank).astype(F32), w, ct, sa, sb)


def _flash_kernel(q_ref, k_ref, v_ref, *rest, tk, nk, ncast):
    cast_in, o_ref, cast_out = rest[:ncast], rest[ncast], rest[ncast + 1:2 * ncast + 1]
    s0_ref, s1_ref = rest[2 * ncast + 1:]
    for src, dst in zip(cast_in, cast_out):
        dst[...] = src[...].astype(dst.dtype)
    q = q_ref[0]
    tq = q.shape[0]
    dv = o_ref.shape[-1]

    def chunk(c):
        start = c * tk
        return pl.ds(start if isinstance(start, int) else pl.multiple_of(start, tk), tk)

    def scores(c):
        return lax.dot_general(q, k_ref[0, chunk(c), :], (((1,), (1,)), ((), ())), preferred_element_type=F32)

    def fold(s_ref, c, m, acc):
        s = s_ref[...]
        m_new = jnp.maximum(m, jnp.max(s, axis=1, keepdims=True))
        p = jnp.exp2(s - m_new).astype(BF16)
        acc = jnp.exp2(m - m_new) * acc + jnp.dot(p, v_ref[0, chunk(c), :], preferred_element_type=F32)
        return m_new, acc

    def pair(j, carry, last):
        m, acc = carry
        s1_ref[...] = scores(2 * j + 1)
        m, acc = fold(s0_ref, 2 * j, m, acc)
        if not last:
            s0_ref[...] = scores(2 * j + 2)
        return fold(s1_ref, 2 * j + 1, m, acc)

    s0_ref[...] = scores(0)
    init = (jnp.full((tq, 1), -jnp.inf, F32), jnp.zeros((tq, 2 * dv), F32))
    carry = init
    for j in range(nk // 2):
        carry = pair(j, carry, j == nk // 2 - 1)
    _, acc = carry
    o_ref[0] = (acc[:, :dv] / acc[:, dv:]).astype(o_ref.dtype)


def flash_attention(q, k, v, *, heads, tq, tk, cast_weights=()):
    b, s, _ = q.shape
    dk = q.shape[-1] // heads
    dv2 = v.shape[-1] // heads
    dv = dv2 // 2
    tq, tk = _tile(s, tq), _tile(s, min(tk, s // 2))
    nk = s // tk
    assert nk % 2 == 0
    nq = s // tq
    steps = b * heads * nq
    wspecs = []
    for w in cast_weights:
        rows = w.shape[0] // steps
        assert rows * steps == w.shape[0] and rows % 16 == 0, (w.shape, steps)
        wspecs.append(pl.BlockSpec((rows, w.shape[1]), lambda bi, h, i: ((bi * heads + h) * nq + i, 0)))
    outs = pl.pallas_call(
        functools.partial(_flash_kernel, tk=tk, nk=nk, ncast=len(cast_weights)),
        grid=(b, heads, nq),
        in_specs=[pl.BlockSpec((1, tq, dk), lambda bi, h, i: (bi, i, h)),
                  pl.BlockSpec((1, s, dk), lambda bi, h, i: (bi, 0, h)),
                  pl.BlockSpec((1, s, dv2), lambda bi, h, i: (bi, 0, h))] + wspecs,
        out_specs=[pl.BlockSpec((1, tq, dv), lambda bi, h, i: (bi, i, h))] + wspecs,
        out_shape=[jax.ShapeDtypeStruct((b, s, heads * dv), BF16)]
        + [jax.ShapeDtypeStruct(w.shape, BF16) for w in cast_weights],
        scratch_shapes=[pltpu.VMEM((tq, tk), F32), pltpu.VMEM((tq, tk), F32)],
        compiler_params=_params("parallel", "parallel", "arbitrary"),
        name="flash_attention",
    )(q, k, v, *cast_weights)
    return outs[0], outs[1:]


def _out_proj_kernel(f_ref, wf_ref, gf_ref, a_ref, ga_ref, w_ref, x_ref, o_ref, y_ref, *, groups, gdim):
    fw = groups * gdim

    @pl.when(pl.program_id(1) == 0)
    def _():
        rows = f_ref.shape[0]
        chunk = min(rows, 256)

        def body(c, carry):
            r = pl.ds(pl.multiple_of(c * chunk, chunk), chunk)
            parts = [jnp.dot(f_ref[r, g * gdim:(g + 1) * gdim], wf_ref[g], preferred_element_type=F32)
                     for g in range(groups)]
            ms = sum(jnp.sum(p * p, axis=-1, keepdims=True) for p in parts) / fw
            inv = lax.rsqrt(ms + NORM_EPS)
            for g in range(groups):
                sl = slice(g * gdim, (g + 1) * gdim)
                y_ref[r, sl] = (parts[g] * inv * gf_ref[:, sl]).astype(BF16)
            return carry

        lax.fori_loop(0, rows // chunk, body, 0)
        _rms_rows(a_ref, ga_ref, y_ref.at[:, fw:])

    acc = jnp.dot(y_ref[...], w_ref[...], preferred_element_type=F32)
    o_ref[...] = (acc + x_ref[...]).astype(o_ref.dtype)


def out_proj(f, wf, gf, a, ga, w, x, *, tm, tn):
    m, fw = f.shape
    aw = a.shape[1]
    groups, gdim, _ = wf.shape
    n = w.shape[1]
    tm, tn = _tile(m, tm), _tile(n, tn)
    return pl.pallas_call(
        functools.partial(_out_proj_kernel, groups=groups, gdim=gdim),
        grid=(m // tm, n // tn),
        in_specs=[pl.BlockSpec((tm, fw), lambda i, j: (i, 0)),
                  pl.BlockSpec((groups, gdim, gdim), lambda i, j: (0, 0, 0)),
                  pl.BlockSpec((1, fw), lambda i, j: (0, 0)),
                  pl.BlockSpec((tm, aw), lambda i, j: (i, 0)),
                  pl.BlockSpec((1, aw), lambda i, j: (0, 0)),
                  pl.BlockSpec((fw + aw, tn), lambda i, j: (0, j)),
                  pl.BlockSpec((tm, tn), lambda i, j: (i, j))],
        out_specs=pl.BlockSpec((tm, tn), lambda i, j: (i, j)),
        out_shape=jax.ShapeDtypeStruct((m, n), F32),
        scratch_shapes=[pltpu.VMEM((tm, fw + aw), BF16)],
        compiler_params=_params("parallel", "arbitrary"),
        name="out_proj",
    )(f, wf, gf.reshape(1, fw).astype(F32), a, ga.reshape(1, aw).astype(F32), w, x)


def _absorb_qk_kernel(wq_ref, k_ref, o_ref, *, scale):
    acc = lax.dot_general(wq_ref[...], k_ref[...], (((1,), (1,)), ((), ())), preferred_element_type=F32)
    o_ref[0] = (acc * scale).astype(o_ref.dtype)


def absorb_qk(wq, kv, *, batch, heads, mem, hd, scale):
    d = wq.shape[0]
    return pl.pallas_call(
        functools.partial(_absorb_qk_kernel, scale=scale),
        grid=(heads, batch),
        in_specs=[pl.BlockSpec((d, hd), lambda h, b: (0, h)),
                  pl.BlockSpec((mem, hd), lambda h, b: (b, h))],
        out_specs=pl.BlockSpec((1, d, mem), lambda h, b: (b, 0, h)),
        out_shape=jax.ShapeDtypeStruct((batch, d, heads * mem), BF16),
        compiler_params=_params("parallel", "parallel"),
        name="absorb_qk",
    )(wq, kv)


def _absorb_vo_kernel(v_ref, wo_ref, o_ref):
    o_ref[0] = jnp.dot(v_ref[...], wo_ref[...], preferred_element_type=F32).astype(o_ref.dtype)


def absorb_vo(kv, wo, *, batch, heads, mem, hd, tn):
    d = wo.shape[1]
    tn = _tile(d, tn)
    return pl.pallas_call(
        _absorb_vo_kernel,
        grid=(heads, d // tn, batch),
        in_specs=[pl.BlockSpec((mem, hd), lambda h, j, b: (b, heads + h)),
                  pl.BlockSpec((hd, tn), lambda h, j, b: (h, j))],
        out_specs=pl.BlockSpec((1, mem, tn), lambda h, j, b: (b, h, j)),
        out_shape=jax.ShapeDtypeStruct((batch, heads * mem, d), BF16),
        compiler_params=_params("parallel", "parallel", "parallel"),
        name="absorb_vo",
    )(kv, wo)


def _xattn_kernel(hf_ref, g_ref, wqk_ref, vo_ref, g2_ref, o_ref, on_ref, p_ref, hn_ref, *, heads, mem):
    _rms_rows(hf_ref, g_ref, hn_ref)
    s = jnp.dot(hn_ref[...], wqk_ref[0], preferred_element_type=F32)
    for h in range(heads):
        seg = s[:, h * mem:(h + 1) * mem]
        e = jnp.exp(seg - jnp.max(seg, axis=-1, keepdims=True))
        p_ref[:, h * mem:(h + 1) * mem] = (e / jnp.sum(e, axis=-1, keepdims=True)).astype(BF16)
    o_ref[...] = jnp.dot(p_ref[...], vo_ref[0], preferred_element_type=F32) + hf_ref[...]
    _rms_rows(o_ref, g2_ref, on_ref)


def xattn(h, g, wqk, vo, g_next, *, seq, heads, mem, tm):
    m, d = h.shape
    tm = _tile(seq, tm)
    per_b = seq // tm
    hm = heads * mem
    once = pl.Buffered(1)
    return pl.pallas_call(
        functools.partial(_xattn_kernel, heads=heads, mem=mem),
        grid=(m // tm,),
        in_specs=[pl.BlockSpec((tm, d), lambda i: (i, 0)),
                  pl.BlockSpec((1, d), lambda i: (0, 0)),
                  pl.BlockSpec((1, d, hm), lambda i: (i // per_b, 0, 0), pipeline_mode=once),
                  pl.BlockSpec((1, hm, d), lambda i: (i // per_b, 0, 0), pipeline_mode=once),
                  pl.BlockSpec((1, d), lambda i: (0, 0))],
        out_specs=[pl.BlockSpec((tm, d), lambda i: (i, 0))] * 2,
        out_shape=[jax.ShapeDtypeStruct((m, d), F32), jax.ShapeDtypeStruct((m, d), BF16)],
        scratch_shapes=[pltpu.VMEM((tm, hm), BF16), pltpu.VMEM((tm, d), BF16)],
        compiler_params=_params("parallel"),
        name="xattn",
    )(h, g.reshape(1, d).astype(F32), wqk, vo, g_next.reshape(1, d).astype(F32))


def _pad_cols(w, n):
    return jnp.pad(w, ((0, 0), (0, n - w.shape[1])))


def kernel(x, mem, positions, g_mix, w_in, w_fourier, g_q_lora, w_uq, g_kv_lora, w_ukv, g_fourier_out, g_mla_out, w_out, g_xattn, g_mem, w_xq, w_xk, w_xv, w_xo, g_mlp, w_ff1, w_ff2, g_final):
    batch, seq, d = x.shape
    depth = g_mix.shape[0]
    groups, gdim = w_fourier.shape[1], w_fourier.shape[2]
    fw = groups * gdim
    q_rank, heads, qk_dim = w_uq.shape[1], w_uq.shape[2], w_uq.shape[3]
    kv_rank = w_ukv.shape[1]
    nope = LANES
    rope = qk_dim - nope
    vdim = w_ukv.shape[3] - nope
    assert rope == LANES // 2 and vdim == LANES and fw % q_rank == 0
    assert (fw + q_rank) % kv_rank == 0 and (fw + q_rank + kv_rank) % LANES == 0
    xheads, xhd = w_xq.shape[2], w_xq.shape[3]
    mtok = mem.shape[1]
    t = batch * seq
    s2 = FOURIER_SEQ_INNER
    s1 = seq // s2

    half = rope // 2
    inv_freq = ROPE_THETA ** (-jnp.arange(half, dtype=F32) / half)
    ang = positions.astype(F32).reshape(t, 1) * inv_freq
    cos, sin = jnp.cos(ang), jnp.sin(ang)
    zq = jnp.zeros_like(cos)
    ct = jnp.concatenate([cos, cos, zq, zq], axis=1)
    sa = jnp.concatenate([zq, sin, zq, zq], axis=1)
    sb = jnp.concatenate([-sin, zq, zq, zq], axis=1)

    cs, tp, tq_tab, c2, s2m = _dft_tables(seq, gdim)

    in_width = w_in.shape[2]
    z_width = -(-(in_width + rope) // 768) * 768
    scale = float(qk_dim) ** -0.5 * math.log2(math.e)

    h = x.reshape(t, d)
    mem2 = mem.reshape(batch * mtok, d)
    for layer in range(depth):
        w_in_b = _pad_cols(w_in[layer], z_width).astype(BF16)
        wq = jnp.pad(w_uq[layer], ((0, 0), (0, 0), (0, 2 * LANES - qk_dim))).reshape(q_rank, heads * 2 * LANES).astype(BF16)
        wkv = jnp.concatenate([w_ukv[layer][:, :, :nope].reshape(kv_rank, heads * nope),
                               w_ukv[layer][:, :, nope:].reshape(kv_rank, heads * vdim)], axis=1).astype(BF16)
        wf = w_fourier[layer].astype(BF16)
        wxq = w_xq[layer].reshape(d, xheads * xhd).astype(BF16)
        wxkv = jnp.concatenate([w_xk[layer].reshape(d, xheads * xhd), w_xv[layer].reshape(d, xheads * xhd)], axis=1).astype(BF16)

        z = norm_mm(h, g_mix[layer], w_in_b, tm=512, tn=768, out_dtype=BF16)
        p, q = chan_dft(z, cs, width=fw, groups=groups, tm=1024)
        pt = p.reshape(batch, s1, s2, fw).transpose(0, 2, 1, 3)
        qt = q.reshape(batch, s1, s2, fw).transpose(0, 2, 1, 3)
        a = seq_dft1(pt, qt, tp, tq_tab, nb=8)
        f = seq_dft2(a.reshape(batch, 2 * s1, s2, fw), c2, s2m, s1=s1, kb=4).reshape(t, fw)
        qh = q_proj(z, g_q_lora[layer], wq, ct, sa, sb, col_block=fw // q_rank, rank=q_rank,
                    heads=heads, scale=scale, tm=256)
        kh, vh = kv_proj(z, g_kv_lora[layer], wkv, ct, sa, sb, ckv_block=(fw + q_rank) // kv_rank,
                         kr_block=(fw + q_rank + kv_rank) // LANES, rank=kv_rank, heads=heads, tm=512)
        o, (w_out_b, wxo, w1, w2) = flash_attention(
            qh.reshape(batch, seq, -1), kh.reshape(batch, seq, -1), vh.reshape(batch, seq, -1),
            heads=heads, tq=1024, tk=512,
            cast_weights=(w_out[layer], w_xo[layer].reshape(xheads * xhd, d), w_ff1[layer], w_ff2[layer]))
        o = o.reshape(t, heads * vdim)
        h = out_proj(f, wf, g_fourier_out[layer], o, g_mla_out[layer], w_out_b, h, tm=1024, tn=512)

        kvm = norm_mm(mem2, g_mem[layer], wxkv, tm=512, tn=512, out_dtype=BF16)
        wqk = absorb_qk(wxq, kvm, batch=batch, heads=xheads, mem=mtok, hd=xhd, scale=float(xhd) ** -0.5)
        vo = absorb_vo(kvm, wxo, batch=batch, heads=xheads, mem=mtok, hd=xhd, tn=2048)
        h, hn = xattn(h, g_xattn[layer], wqk, vo, g_mlp[layer], seq=seq, heads=xheads, mem=mtok, tm=256)

        act = mm_relu2(hn, w1, tm=1024, tn=1024, out_dtype=BF16)
        h = mm_res(act, w2, h, tm=1024, tn=1024, tk=2048)

    return rmsnorm(h, g_final, tm=256, out_dtype=x.dtype).reshape(batch, seq, d)
```

```python
import functools
import math

import numpy as np
import jax
import jax.numpy as jnp
from jax import lax
from jax.experimental import pallas as pl
from jax.experimental.pallas import tpu as pltpu

F32 = jnp.float32
BF16 = jnp.bfloat16

NORM_EPS = 1e-6
ROPE_THETA = 10000.0
LANES = 128
VMEM_LIMIT_BYTES = 56 * 1024 * 1024
FOURIER_SEQ_INNER = 128


def _params(*sem):
    return pltpu.CompilerParams(dimension_semantics=sem, vmem_limit_bytes=VMEM_LIMIT_BYTES)


def _rms(xf, g):
    ms = jnp.mean(xf * xf, axis=-1, keepdims=True)
    return xf * lax.rsqrt(ms + NORM_EPS) * g


def _rms_rows(a_ref, g_ref, o_ref, chunk=64):
    rows = a_ref.shape[0]
    chunk = min(chunk, rows)

    def body(c, carry):
        r = pl.ds(pl.multiple_of(c * chunk, chunk), chunk)
        o_ref[r, :] = _rms(a_ref[r, :].astype(F32), g_ref[...]).astype(o_ref.dtype)
        return carry

    lax.fori_loop(0, rows // chunk, body, 0)


def _tile(dim, pref):
    t = min(dim, pref)
    assert dim % t == 0, (dim, pref)
    return t


NORM_CHUNK_ROWS = 32


def _norm_mm_kernel(a0_ref, an_ref, g_ref, w_ref, o_ref, xa_ref, xb_ref, *, slice_rows, n_slices):
    i, j = pl.program_id(0), pl.program_id(1)

    @pl.when((i == 0) & (j == 0))
    def _():
        _rms_rows(a0_ref, g_ref, xa_ref)

    def step(cur_ref, nxt_ref):
        start = jnp.minimum(j, n_slices - 1) * slice_rows
        for c in range(slice_rows // NORM_CHUNK_ROWS):
            r = pl.ds(pl.multiple_of(start + c * NORM_CHUNK_ROWS, NORM_CHUNK_ROWS), NORM_CHUNK_ROWS)
            nxt_ref[r, :] = _rms(an_ref[r, :].astype(F32), g_ref[...]).astype(BF16)
        o_ref[...] = jnp.dot(cur_ref[...], w_ref[...], preferred_element_type=F32).astype(o_ref.dtype)

    @pl.when(i % 2 == 0)
    def _():
        step(xa_ref, xb_ref)

    @pl.when(i % 2 == 1)
    def _():
        step(xb_ref, xa_ref)


def norm_mm(a, g, w, *, tm, tn, out_dtype):
    m, k = a.shape
    n = w.shape[1]
    tm, tn = _tile(m, tm), _tile(n, tn)
    ni, nj = m // tm, n // tn
    n_slices = 1
    while n_slices * 2 <= nj and (tm // (n_slices * 2)) % NORM_CHUNK_ROWS == 0:
        n_slices *= 2
    return pl.pallas_call(
        functools.partial(_norm_mm_kernel, slice_rows=tm // n_slices, n_slices=n_slices),
        grid=(ni, nj),
        in_specs=[pl.BlockSpec((tm, k), lambda i, j: (0, 0), pipeline_mode=pl.Buffered(1)),
                  pl.BlockSpec((tm, k), lambda i, j: (jnp.minimum(i + 1, ni - 1), 0)),
                  pl.BlockSpec((1, k), lambda i, j: (0, 0)),
                  pl.BlockSpec((k, tn), lambda i, j: (0, j))],
        out_specs=pl.BlockSpec((tm, tn), lambda i, j: (i, j)),
        out_shape=jax.ShapeDtypeStruct((m, n), out_dtype),
        scratch_shapes=[pltpu.VMEM((tm, k), BF16), pltpu.VMEM((tm, k), BF16)],
        compiler_params=_params("arbitrary", "arbitrary"),
        name="norm_mm",
    )(a, a, g.reshape(1, k).astype(F32), w)


def _mm_relu2_kernel(a_ref, w_ref, o_ref):
    acc = jnp.dot(a_ref[...], w_ref[...], preferred_element_type=F32)
    o_ref[...] = jnp.square(jnp.maximum(acc, 0.0)).astype(o_ref.dtype)


def mm_relu2(a, w, *, tm, tn, out_dtype):
    m, k = a.shape
    n = w.shape[1]
    tm, tn = _tile(m, tm), _tile(n, tn)
    return pl.pallas_call(
        _mm_relu2_kernel,
        grid=(m // tm, n // tn),
        in_specs=[pl.BlockSpec((tm, k), lambda i, j: (i, 0)),
                  pl.BlockSpec((k, tn), lambda i, j: (0, j))],
        out_specs=pl.BlockSpec((tm, tn), lambda i, j: (i, j)),
        out_shape=jax.ShapeDtypeStruct((m, n), out_dtype),
        compiler_params=_params("parallel", "parallel"),
        name="mm_relu2",
    )(a, w)


def _mm_res_kernel(a_ref, w_ref, res_ref, o_ref):
    @pl.when(pl.program_id(2) == 0)
    def _():
        o_ref[...] = res_ref[...]

    o_ref[...] += jnp.dot(a_ref[...], w_ref[...], preferred_element_type=F32)


def mm_res(a, w, res, *, tm, tn, tk):
    m, k = a.shape
    n = w.shape[1]
    tm, tn, tk = _tile(m, tm), _tile(n, tn), _tile(k, tk)
    return pl.pallas_call(
        _mm_res_kernel,
        grid=(m // tm, n // tn, k // tk),
        in_specs=[pl.BlockSpec((tm, tk), lambda i, j, kk: (i, kk)),
                  pl.BlockSpec((tk, tn), lambda i, j, kk: (kk, j)),
                  pl.BlockSpec((tm, tn), lambda i, j, kk: (i, j))],
        out_specs=pl.BlockSpec((tm, tn), lambda i, j, kk: (i, j)),
        out_shape=jax.ShapeDtypeStruct((m, n), F32),
        compiler_params=_params("parallel", "parallel", "arbitrary"),
        name="mm_res",
    )(a, w, res)


def _mm_res_norm_kernel(a_ref, w_ref, res_ref, g_ref, o_ref):
    tn = res_ref.shape[1]
    j, kk = pl.program_id(1), pl.program_id(2)
    col = pl.ds(pl.multiple_of(j * tn, tn), tn)

    @pl.when(kk == 0)
    def _():
        o_ref[:, col] = res_ref[...]

    o_ref[:, col] += jnp.dot(a_ref[...], w_ref[...], preferred_element_type=F32)

    @pl.when((j == pl.num_programs(1) - 1) & (kk == pl.num_programs(2) - 1))
    def _():
        _rms_rows(o_ref, g_ref, o_ref)


def mm_res_norm(a, w, res, gain, *, tm, tn, tk):
    m, k = a.shape
    n = w.shape[1]
    tm, tn, tk = _tile(m, tm), _tile(n, tn), _tile(k, tk)
    return pl.pallas_call(
        _mm_res_norm_kernel,
        grid=(m // tm, n // tn, k // tk),
        in_specs=[pl.BlockSpec((tm, tk), lambda i, j, kk: (i, kk)),
                  pl.BlockSpec((tk, tn), lambda i, j, kk: (kk, j)),
                  pl.BlockSpec((tm, tn), lambda i, j, kk: (i, j)),
                  pl.BlockSpec((1, n), lambda i, j, kk: (0, 0))],
        out_specs=pl.BlockSpec((tm, n), lambda i, j, kk: (i, 0)),
        out_shape=jax.ShapeDtypeStruct((m, n), F32),
        compiler_params=_params("parallel", "arbitrary", "arbitrary"),
        name="mm_res_norm",
    )(a, w, res, gain.reshape(1, n).astype(F32))


def _chan_dft_kernel(z_ref, cs_ref, p_ref, q_ref, *, groups, gdim):
    for g in range(groups):
        pq = jnp.dot(z_ref[:, g * gdim:(g + 1) * gdim], cs_ref[...], preferred_element_type=F32)
        p_ref[:, g * gdim:(g + 1) * gdim] = pq[:, :gdim].astype(p_ref.dtype)
        q_ref[:, g * gdim:(g + 1) * gdim] = pq[:, gdim:].astype(q_ref.dtype)


def chan_dft(z, cs, *, width, groups, tm):
    m = z.shape[0]
    gdim = width // groups
    tm = _tile(m, tm)
    out = jax.ShapeDtypeStruct((m, width), BF16)
    return pl.pallas_call(
        functools.partial(_chan_dft_kernel, groups=groups, gdim=gdim),
        grid=(m // tm,),
        in_specs=[pl.BlockSpec((tm, width), lambda i: (i, 0)),
                  pl.BlockSpec((gdim, 2 * gdim), lambda i: (0, 0))],
        out_specs=[pl.BlockSpec((tm, width), lambda i: (i, 0))] * 2,
        out_shape=[out, out],
        compiler_params=_params("parallel"),
        name="chan_dft",
    )(z, cs)


def _seq_dft1_kernel(p_ref, q_ref, tp_ref, tq_ref, o_ref, *, nb, width):
    for t in range(nb):
        acc = jnp.dot(tp_ref[t], p_ref[0, t], preferred_element_type=F32)
        acc += jnp.dot(tq_ref[t], q_ref[0, t], preferred_element_type=F32)
        o_ref[0, :, t * width:(t + 1) * width] = acc.astype(o_ref.dtype)


def seq_dft1(pt, qt, tp, tq, *, nb):
    b, s2, s1, width = pt.shape
    nb = _tile(s2, nb)
    dspec = pl.BlockSpec((1, nb, s1, width), lambda bi, j: (bi, j, 0, 0))
    tspec = pl.BlockSpec((nb, 2 * s1, s1), lambda bi, j: (j, 0, 0))
    return pl.pallas_call(
        functools.partial(_seq_dft1_kernel, nb=nb, width=width),
        grid=(b, s2 // nb),
        in_specs=[dspec, dspec, tspec, tspec],
        out_specs=pl.BlockSpec((1, 2 * s1, nb * width), lambda bi, j: (bi, 0, j)),
        out_shape=jax.ShapeDtypeStruct((b, 2 * s1, s2 * width), BF16),
        compiler_params=_params("parallel", "parallel"),
        name="seq_dft1",
    )(pt, qt, tp, tq)


def _seq_dft2_kernel(re_ref, im_ref, c_ref, s_ref, o_ref, *, kb, width):
    for t in range(kb):
        acc = jnp.dot(c_ref[...], re_ref[0, t], preferred_element_type=F32)
        acc += jnp.dot(s_ref[...], im_ref[0, t], preferred_element_type=F32)
        o_ref[0, :, t * width:(t + 1) * width] = acc.astype(o_ref.dtype)


def seq_dft2(a4, c2, s2m, *, s1, kb):
    b, _, s2, width = a4.shape
    kb = _tile(s1, kb)
    nblk = s1 // kb
    return pl.pallas_call(
        functools.partial(_seq_dft2_kernel, kb=kb, width=width),
        grid=(b, nblk),
        in_specs=[
            pl.BlockSpec((1, kb, s2, width), lambda bi, j: (bi, j, 0, 0)),
            pl.BlockSpec((1, kb, s2, width), lambda bi, j: (bi, j + nblk, 0, 0)),
            pl.BlockSpec((s2, s2), lambda bi, j: (0, 0)),
            pl.BlockSpec((s2, s2), lambda bi, j: (0, 0)),
        ],
        out_specs=pl.BlockSpec((1, s2, kb * width), lambda bi, j: (bi, 0, j)),
        out_shape=jax.ShapeDtypeStruct((b, s2, s1 * width), BF16),
        compiler_params=_params("parallel", "parallel"),
        name="seq_dft2",
    )(a4, a4, c2, s2m)


def _dft_tables(seq, gdim):
    s2 = FOURIER_SEQ_INNER
    s1 = seq // s2
    c = np.arange(gdim)
    ang = 2.0 * np.pi * ((c[:, None] * c[None, :]) % gdim) / gdim
    cs = np.concatenate([np.cos(ang), np.sin(ang)], axis=1) / math.sqrt(gdim)
    k1 = np.arange(s1)
    n1 = np.arange(s1)
    n2 = np.arange(s2)
    idx = (k1[None, :, None] * (s2 * n1[None, None, :] + n2[:, None, None])) % seq
    phi = 2.0 * np.pi * idx / seq
    co, si = np.cos(phi), np.sin(phi)
    tp = np.concatenate([co, -si], axis=1)
    tq = np.concatenate([-si, -co], axis=1)
    k2 = np.arange(s2)
    ang2 = 2.0 * np.pi * ((k2[:, None] * n2[None, :]) % s2) / s2
    c2 = np.cos(ang2) / math.sqrt(seq)
    s2m = np.sin(ang2) / math.sqrt(seq)
    as_bf16 = lambda x: jnp.asarray(x, dtype=F32).astype(BF16)
    return as_bf16(cs), as_bf16(tp), as_bf16(tq), as_bf16(c2), as_bf16(s2m)


def _rope128(r, ct, sa, sb):
    quarter = LANES // 4
    return r * ct + pltpu.roll(r, quarter, 1) * sa + pltpu.roll(r, LANES - quarter, 1) * sb


def _q_proj_kernel(c_ref, g_ref, w_ref, ct_ref, sa_ref, sb_ref, o_ref, *, heads, scale):
    xn = _rms(c_ref[...].astype(F32), g_ref[...]).astype(BF16)
    acc = jnp.dot(xn, w_ref[...], preferred_element_type=F32)
    ct, sa, sb = ct_ref[...], sa_ref[...], sb_ref[...]
    for h in range(heads):
        base = 2 * LANES * h
        o_ref[:, base:base + LANES] = (acc[:, base:base + LANES] * scale).astype(o_ref.dtype)
        r = _rope128(acc[:, base + LANES:base + 2 * LANES], ct, sa, sb)
        o_ref[:, base + LANES:base + 2 * LANES] = (r * scale).astype(o_ref.dtype)


def q_proj(z, g, w, ct, sa, sb, *, col_block, rank, heads, scale, tm):
    m = z.shape[0]
    tm = _tile(m, tm)
    n = w.shape[1]
    tab = pl.BlockSpec((tm, LANES), lambda i: (i, 0))
    return pl.pallas_call(
        functools.partial(_q_proj_kernel, heads=heads, scale=scale),
        grid=(m // tm,),
        in_specs=[pl.BlockSpec((tm, rank), lambda i: (i, col_block)),
                  pl.BlockSpec((1, rank), lambda i: (0, 0)),
                  pl.BlockSpec((rank, n), lambda i: (0, 0)),
                  tab, tab, tab],
        out_specs=pl.BlockSpec((tm, n), lambda i: (i, 0)),
        out_shape=jax.ShapeDtypeStruct((m, n), BF16),
        compiler_params=_params("parallel"),
        name="q_proj",
    )(z, g.reshape(1, rank).astype(F32), w, ct, sa, sb)


def _kv_proj_kernel(c_ref, kr_ref, g_ref, w_ref, ct_ref, sa_ref, sb_ref, k_ref, v_ref, *, heads):
    xn = _rms(c_ref[...].astype(F32), g_ref[...]).astype(BF16)
    acc = jnp.dot(xn, w_ref[...], preferred_element_type=F32)
    kr = _rope128(kr_ref[...].astype(F32), ct_ref[...], sa_ref[...], sb_ref[...]).astype(k_ref.dtype)
    for h in range(heads):
        k_ref[:, 2 * LANES * h:2 * LANES * h + LANES] = acc[:, LANES * h:LANES * (h + 1)].astype(k_ref.dtype)
        k_ref[:, 2 * LANES * h + LANES:2 * LANES * (h + 1)] = kr
        v_ref[:, 2 * LANES * h:2 * LANES * h + LANES] = acc[:, LANES * (heads + h):LANES * (heads + h + 1)].astype(v_ref.dtype)
        v_ref[:, 2 * LANES * h + LANES:2 * LANES * (h + 1)] = jnp.ones((acc.shape[0], LANES), v_ref.dtype)


def kv_proj(z, g, w, ct, sa, sb, *, ckv_block, kr_block, rank, heads, tm):
    m = z.shape[0]
    tm = _tile(m, tm)
    tab = pl.BlockSpec((tm, LANES), lambda i: (i, 0))
    return pl.pallas_call(
        functools.partial(_kv_proj_kernel, heads=heads),
        grid=(m // tm,),
        in_specs=[pl.BlockSpec((tm, rank), lambda i: (i, ckv_block)),
                  pl.BlockSpec((tm, LANES), lambda i: (i, kr_block)),
                  pl.BlockSpec((1, rank), lambda i: (0, 0)),
                  pl.BlockSpec((rank, 2 * LANES * heads), lambda i: (0, 0)),
                  tab, tab, tab],
        out_specs=[pl.BlockSpec((tm, 2 * LANES * heads), lambda i: (i, 0))] * 2,
        out_shape=[jax.ShapeDtypeStruct((m, 2 * LANES * heads), BF16)] * 2,
        compiler_params=_params("parallel"),
        name="kv_proj",
    )(z, z, g.reshape(1, rank).astype(F32), w, ct, sa, sb)


def _flash_kernel(q_ref, k_ref, v_ref, *rest, tk, nk, ncast):
    cast_in, o_ref, cast_out = rest[:ncast], rest[ncast], rest[ncast + 1:2 * ncast + 1]
    s0_ref, s1_ref = rest[2 * ncast + 1:]
    for src, dst in zip(cast_in, cast_out):
        dst[...] = src[...].astype(dst.dtype)
    q = q_ref[0]
    tq = q.shape[0]
    dv = o_ref.shape[-1]

    def chunk(c):
        start = c * tk
        return pl.ds(start if isinstance(start, int) else pl.multiple_of(start, tk), tk)

    def scores(c):
        return lax.dot_general(q, k_ref[0, chunk(c), :], (((1,), (1,)), ((), ())), preferred_element_type=F32)

    def fold(s_ref, c, m, acc):
        s = s_ref[...]
        m_new = jnp.maximum(m, jnp.max(s, axis=1, keepdims=True))
        p = jnp.exp2(s - m_new).astype(BF16)
        acc = jnp.exp2(m - m_new) * acc + jnp.dot(p, v_ref[0, chunk(c), :], preferred_element_type=F32)
        return m_new, acc

    def pair(j, carry, last):
        m, acc = carry
        s1_ref[...] = scores(2 * j + 1)
        m, acc = fold(s0_ref, 2 * j, m, acc)
        if not last:
            s0_ref[...] = scores(2 * j + 2)
        return fold(s1_ref, 2 * j + 1, m, acc)

    s0_ref[...] = scores(0)
    init = (jnp.full((tq, 1), -jnp.inf, F32), jnp.zeros((tq, 2 * dv), F32))
    carry = init
    for j in range(nk // 2):
        carry = pair(j, carry, j == nk // 2 - 1)
    _, acc = carry
    o_ref[0] = (acc[:, :dv] / acc[:, dv:]).astype(o_ref.dtype)


def flash_attention(q, k, v, *, heads, tq, tk, cast_weights=()):
    b, s, _ = q.shape
    dk = q.shape[-1] // heads
    dv2 = v.shape[-1] // heads
    dv = dv2 // 2
    tq, tk = _tile(s, tq), _tile(s, min(tk, s // 2))
    nk = s // tk
    assert nk % 2 == 0
    nq = s // tq
    steps = b * heads * nq
    wspecs = []
    for w in cast_weights:
        rows = w.shape[0] // steps
        assert rows * steps == w.shape[0] and rows % 16 == 0, (w.shape, steps)
        wspecs.append(pl.BlockSpec((rows, w.shape[1]), lambda bi, h, i: ((bi * heads + h) * nq + i, 0)))
    wspecs_in = wspecs_out = wspecs
    wshapes = [jax.ShapeDtypeStruct(w.shape, BF16) for w in cast_weights]
    outs = pl.pallas_call(
        functools.partial(_flash_kernel, tk=tk, nk=nk, ncast=len(cast_weights)),
        grid=(b, heads, nq),
        in_specs=[pl.BlockSpec((1, tq, dk), lambda bi, h, i: (bi, i, h)),
                  pl.BlockSpec((1, s, dk), lambda bi, h, i: (bi, 0, h)),
                  pl.BlockSpec((1, s, dv2), lambda bi, h, i: (bi, 0, h))] + wspecs_in,
        out_specs=[pl.BlockSpec((1, tq, dv), lambda bi, h, i: (bi, i, h))] + wspecs_out,
        out_shape=[jax.ShapeDtypeStruct((b, s, heads * dv), BF16)] + wshapes,
        scratch_shapes=[pltpu.VMEM((tq, tk), F32), pltpu.VMEM((tq, tk), F32)],
        compiler_params=_params("parallel", "parallel", "arbitrary"),
        name="flash_attention",
    )(q, k, v, *cast_weights)
    return outs[0], outs[1:]


def _out_proj_kernel(f_ref, wf_ref, gf_ref, a_ref, ga_ref, w_ref, x_ref, o_ref, y_ref, *, groups, gdim):
    fw = groups * gdim

    @pl.when(pl.program_id(1) == 0)
    def _():
        rows = f_ref.shape[0]
        chunk = min(rows, 256)

        def body(c, carry):
            r = pl.ds(pl.multiple_of(c * chunk, chunk), chunk)
            parts = [jnp.dot(f_ref[r, g * gdim:(g + 1) * gdim], wf_ref[g], preferred_element_type=F32)
                     for g in range(groups)]
            ms = sum(jnp.sum(p * p, axis=-1, keepdims=True) for p in parts) / fw
            inv = lax.rsqrt(ms + NORM_EPS)
            for g in range(groups):
                sl = slice(g * gdim, (g + 1) * gdim)
                y_ref[r, sl] = (parts[g] * inv * gf_ref[:, sl]).astype(BF16)
            return carry

        lax.fori_loop(0, rows // chunk, body, 0)
        _rms_rows(a_ref, ga_ref, y_ref.at[:, fw:])

    acc = jnp.dot(y_ref[...], w_ref[...], preferred_element_type=F32)
    o_ref[...] = (acc + x_ref[...]).astype(o_ref.dtype)


def out_proj(f, wf, gf, a, ga, w, x, *, tm, tn):
    m, fw = f.shape
    aw = a.shape[1]
    groups, gdim, _ = wf.shape
    n = w.shape[1]
    tm, tn = _tile(m, tm), _tile(n, tn)
    return pl.pallas_call(
        functools.partial(_out_proj_kernel, groups=groups, gdim=gdim),
        grid=(m // tm, n // tn),
        in_specs=[pl.BlockSpec((tm, fw), lambda i, j: (i, 0)),
                  pl.BlockSpec((groups, gdim, gdim), lambda i, j: (0, 0, 0)),
                  pl.BlockSpec((1, fw), lambda i, j: (0, 0)),
                  pl.BlockSpec((tm, aw), lambda i, j: (i, 0)),
                  pl.BlockSpec((1, aw), lambda i, j: (0, 0)),
                  pl.BlockSpec((fw + aw, tn), lambda i, j: (0, j)),
                  pl.BlockSpec((tm, tn), lambda i, j: (i, j))],
        out_specs=pl.BlockSpec((tm, tn), lambda i, j: (i, j)),
        out_shape=jax.ShapeDtypeStruct((m, n), F32),
        scratch_shapes=[pltpu.VMEM((tm, fw + aw), BF16)],
        compiler_params=_params("parallel", "arbitrary"),
        name="out_proj",
    )(f, wf, gf.reshape(1, fw).astype(F32), a, ga.reshape(1, aw).astype(F32), w, x)


def _absorb_qk_kernel(wq_ref, k_ref, o_ref, *, scale):
    acc = lax.dot_general(wq_ref[...], k_ref[...], (((1,), (1,)), ((), ())), preferred_element_type=F32)
    o_ref[0] = (acc * scale).astype(o_ref.dtype)


def absorb_qk(wq, kv, *, batch, heads, mem, hd, scale):
    d = wq.shape[0]
    return pl.pallas_call(
        functools.partial(_absorb_qk_kernel, scale=scale),
        grid=(heads, batch),
        in_specs=[pl.BlockSpec((d, hd), lambda h, b: (0, h)),
                  pl.BlockSpec((mem, hd), lambda h, b: (b, h))],
        out_specs=pl.BlockSpec((1, d, mem), lambda h, b: (b, 0, h)),
        out_shape=jax.ShapeDtypeStruct((batch, d, heads * mem), BF16),
        compiler_params=_params("parallel", "parallel"),
        name="absorb_qk",
    )(wq, kv)


def _absorb_vo_kernel(v_ref, wo_ref, o_ref):
    o_ref[0] = jnp.dot(v_ref[...], wo_ref[...], preferred_element_type=F32).astype(o_ref.dtype)


def absorb_vo(kv, wo, *, batch, heads, mem, hd, tn):
    d = wo.shape[1]
    tn = _tile(d, tn)
    return pl.pallas_call(
        _absorb_vo_kernel,
        grid=(heads, d // tn, batch),
        in_specs=[pl.BlockSpec((mem, hd), lambda h, j, b: (b, h)),
                  pl.BlockSpec((hd, tn), lambda h, j, b: (h, j))],
        out_specs=pl.BlockSpec((1, mem, tn), lambda h, j, b: (b, h, j)),
        out_shape=jax.ShapeDtypeStruct((batch, heads * mem, d), BF16),
        compiler_params=_params("parallel", "parallel", "parallel"),
        name="absorb_vo",
    )(kv, wo)


def _xattn_kernel(hf_ref, g_ref, wqk_ref, vo_ref, g2_ref, o_ref, on_ref, p_ref, hn_ref, *, heads, mem):
    _rms_rows(hf_ref, g_ref, hn_ref)
    s = jnp.dot(hn_ref[...], wqk_ref[0], preferred_element_type=F32)
    for h in range(heads):
        seg = s[:, h * mem:(h + 1) * mem]
        e = jnp.exp(seg - jnp.max(seg, axis=-1, keepdims=True))
        p_ref[:, h * mem:(h + 1) * mem] = (e / jnp.sum(e, axis=-1, keepdims=True)).astype(BF16)
    o_ref[...] = jnp.dot(p_ref[...], vo_ref[0], preferred_element_type=F32) + hf_ref[...]
    _rms_rows(o_ref, g2_ref, on_ref)


def xattn(h, g, wqk, vo, g_next, *, seq, heads, mem, tm):
    m, d = h.shape
    tm = _tile(seq, tm)
    per_b = seq // tm
    hm = heads * mem
    once = pl.Buffered(1)
    return pl.pallas_call(
        functools.partial(_xattn_kernel, heads=heads, mem=mem),
        grid=(m // tm,),
        in_specs=[pl.BlockSpec((tm, d), lambda i: (i, 0)),
                  pl.BlockSpec((1, d), lambda i: (0, 0)),
                  pl.BlockSpec((1, d, hm), lambda i: (i // per_b, 0, 0), pipeline_mode=once),
                  pl.BlockSpec((1, hm, d), lambda i: (i // per_b, 0, 0), pipeline_mode=once),
                  pl.BlockSpec((1, d), lambda i: (0, 0))],
        out_specs=[pl.BlockSpec((tm, d), lambda i: (i, 0))] * 2,
        out_shape=[jax.ShapeDtypeStruct((m, d), F32), jax.ShapeDtypeStruct((m, d), BF16)],
        scratch_shapes=[pltpu.VMEM((tm, hm), BF16), pltpu.VMEM((tm, d), BF16)],
        compiler_params=_params("parallel"),
        name="xattn",
    )(h, g.reshape(1, d).astype(F32), wqk, vo, g_next.reshape(1, d).astype(F32))


def _pad_cols(w, n):
    return jnp.pad(w, ((0, 0), (0, n - w.shape[1])))


def kernel(x, mem, positions, g_mix, w_in, w_fourier, g_q_lora, w_uq, g_kv_lora, w_ukv, g_fourier_out, g_mla_out, w_out, g_xattn, g_mem, w_xq, w_xk, w_xv, w_xo, g_mlp, w_ff1, w_ff2, g_final):
    batch, seq, d = x.shape
    depth = g_mix.shape[0]
    groups, gdim = w_fourier.shape[1], w_fourier.shape[2]
    fw = groups * gdim
    q_rank, heads, qk_dim = w_uq.shape[1], w_uq.shape[2], w_uq.shape[3]
    kv_rank = w_ukv.shape[1]
    nope = LANES
    rope = qk_dim - nope
    vdim = w_ukv.shape[3] - nope
    assert rope == LANES // 2 and vdim == LANES and fw % q_rank == 0
    assert (fw + q_rank) % kv_rank == 0 and (fw + q_rank + kv_rank) % LANES == 0
    xheads, xhd = w_xq.shape[2], w_xq.shape[3]
    mtok = mem.shape[1]
    t = batch * seq
    s2 = FOURIER_SEQ_INNER
    s1 = seq // s2

    half = rope // 2
    inv_freq = ROPE_THETA ** (-jnp.arange(half, dtype=F32) / half)
    ang = positions.astype(F32).reshape(t, 1) * inv_freq
    cos, sin = jnp.cos(ang), jnp.sin(ang)
    zq = jnp.zeros_like(cos)
    ct = jnp.concatenate([cos, cos, zq, zq], axis=1)
    sa = jnp.concatenate([zq, sin, zq, zq], axis=1)
    sb = jnp.concatenate([-sin, zq, zq, zq], axis=1)

    cs, tp, tq_tab, c2, s2m = _dft_tables(seq, gdim)

    in_width = w_in.shape[2]
    z_width = -(-(in_width + rope) // 768) * 768
    scale = float(qk_dim) ** -0.5 * math.log2(math.e)

    h = x.reshape(t, d)
    mem2 = mem.reshape(batch * mtok, d)
    for layer in range(depth):
        w_in_b = _pad_cols(w_in[layer], z_width).astype(BF16)
        wq = jnp.pad(w_uq[layer], ((0, 0), (0, 0), (0, 2 * LANES - qk_dim))).reshape(q_rank, heads * 2 * LANES).astype(BF16)
        wkv = jnp.concatenate([w_ukv[layer][:, :, :nope].reshape(kv_rank, heads * nope),
                               w_ukv[layer][:, :, nope:].reshape(kv_rank, heads * vdim)], axis=1).astype(BF16)
        wf = w_fourier[layer].astype(BF16)
        wxq = w_xq[layer].reshape(d, xheads * xhd).astype(BF16)
        wxk = w_xk[layer].reshape(d, xheads * xhd).astype(BF16)
        wxv = w_xv[layer].reshape(d, xheads * xhd).astype(BF16)

        z = norm_mm(h, g_mix[layer], w_in_b, tm=512, tn=768, out_dtype=BF16)
        p, q = chan_dft(z, cs, width=fw, groups=groups, tm=1024)
        pt = p.reshape(batch, s1, s2, fw).transpose(0, 2, 1, 3)
        qt = q.reshape(batch, s1, s2, fw).transpose(0, 2, 1, 3)
        a = seq_dft1(pt, qt, tp, tq_tab, nb=8)
        f = seq_dft2(a.reshape(batch, 2 * s1, s2, fw), c2, s2m, s1=s1, kb=4).reshape(t, fw)
        qh = q_proj(z, g_q_lora[layer], wq, ct, sa, sb, col_block=fw // q_rank, rank=q_rank,
                    heads=heads, scale=scale, tm=512)
        kh, vh = kv_proj(z, g_kv_lora[layer], wkv, ct, sa, sb, ckv_block=(fw + q_rank) // kv_rank,
                         kr_block=(fw + q_rank + kv_rank) // LANES, rank=kv_rank, heads=heads, tm=512)
        o, (w_out_b, wxo, w1, w2) = flash_attention(
            qh.reshape(batch, seq, -1), kh.reshape(batch, seq, -1), vh.reshape(batch, seq, -1),
            heads=heads, tq=1024, tk=512,
            cast_weights=(w_out[layer], w_xo[layer].reshape(xheads * xhd, d), w_ff1[layer], w_ff2[layer]))
        o = o.reshape(t, heads * vdim)
        h = out_proj(f, wf, g_fourier_out[layer], o, g_mla_out[layer], w_out_b, h, tm=1024, tn=512)

        km = norm_mm(mem2, g_mem[layer], wxk, tm=512, tn=512, out_dtype=BF16)
        vm = norm_mm(mem2, g_mem[layer], wxv, tm=512, tn=512, out_dtype=BF16)
        wqk = absorb_qk(wxq, km, batch=batch, heads=xheads, mem=mtok, hd=xhd, scale=float(xhd) ** -0.5)
        vo = absorb_vo(vm, wxo, batch=batch, heads=xheads, mem=mtok, hd=xhd, tn=2048)
        h, hn = xattn(h, g_xattn[layer], wqk, vo, g_mlp[layer], seq=seq, heads=xheads, mem=mtok, tm=256)

        act = mm_relu2(hn, w1, tm=1024, tn=1024, out_dtype=BF16)
        if layer == depth - 1:
            h = mm_res_norm(act, w2, h, g_final, tm=1024, tn=1024, tk=1024)
        else:
            h = mm_res(act, w2, h, tm=1024, tn=1024, tk=2048)

    return h.astype(x.dtype).reshape(batch, seq, d)
```

```python
import functools
import math

import numpy as np
import jax
import jax.numpy as jnp
from jax import lax
from jax.experimental import pallas as pl
from jax.experimental.pallas import tpu as pltpu

F32 = jnp.float32
BF16 = jnp.bfloat16

NORM_EPS = 1e-6
ROPE_THETA = 10000.0
LANES = 128
VMEM_LIMIT_BYTES = 56 * 1024 * 1024
FOURIER_SEQ_INNER = 128


def _params(*sem):
    return pltpu.CompilerParams(dimension_semantics=sem, vmem_limit_bytes=VMEM_LIMIT_BYTES)


def _rms(xf, g):
    ms = jnp.mean(xf * xf, axis=-1, keepdims=True)
    return xf * lax.rsqrt(ms + NORM_EPS) * g


def _rms_rows(a_ref, g_ref, o_ref, chunk=64):
    rows = a_ref.shape[0]
    chunk = min(chunk, rows)

    def body(c, carry):
        r = pl.ds(pl.multiple_of(c * chunk, chunk), chunk)
        o_ref[r, :] = _rms(a_ref[r, :].astype(F32), g_ref[...]).astype(o_ref.dtype)
        return carry

    lax.fori_loop(0, rows // chunk, body, 0)


def _tile(dim, pref):
    t = min(dim, pref)
    assert dim % t == 0, (dim, pref)
    return t


NORM_CHUNK_ROWS = 32


def _norm_mm_kernel(a0_ref, an_ref, g_ref, w_ref, o_ref, xa_ref, xb_ref, *, slice_rows, n_slices):
    i, j = pl.program_id(0), pl.program_id(1)

    @pl.when((i == 0) & (j == 0))
    def _():
        _rms_rows(a0_ref, g_ref, xa_ref)

    def step(cur_ref, nxt_ref):
        start = jnp.minimum(j, n_slices - 1) * slice_rows
        for c in range(slice_rows // NORM_CHUNK_ROWS):
            r = pl.ds(pl.multiple_of(start + c * NORM_CHUNK_ROWS, NORM_CHUNK_ROWS), NORM_CHUNK_ROWS)
            nxt_ref[r, :] = _rms(an_ref[r, :].astype(F32), g_ref[...]).astype(BF16)
        o_ref[...] = jnp.dot(cur_ref[...], w_ref[...], preferred_element_type=F32).astype(o_ref.dtype)

    @pl.when(i % 2 == 0)
    def _():
        step(xa_ref, xb_ref)

    @pl.when(i % 2 == 1)
    def _():
        step(xb_ref, xa_ref)


def norm_mm(a, g, w, *, tm, tn, out_dtype):
    m, k = a.shape
    n = w.shape[1]
    tm, tn = _tile(m, tm), _tile(n, tn)
    ni, nj = m // tm, n // tn
    n_slices = 1
    while n_slices * 2 <= nj and (tm // (n_slices * 2)) % NORM_CHUNK_ROWS == 0:
        n_slices *= 2
    return pl.pallas_call(
        functools.partial(_norm_mm_kernel, slice_rows=tm // n_slices, n_slices=n_slices),
        grid=(ni, nj),
        in_specs=[pl.BlockSpec((tm, k), lambda i, j: (0, 0), pipeline_mode=pl.Buffered(1)),
                  pl.BlockSpec((tm, k), lambda i, j: (jnp.minimum(i + 1, ni - 1), 0)),
                  pl.BlockSpec((1, k), lambda i, j: (0, 0)),
                  pl.BlockSpec((k, tn), lambda i, j: (0, j))],
        out_specs=pl.BlockSpec((tm, tn), lambda i, j: (i, j)),
        out_shape=jax.ShapeDtypeStruct((m, n), out_dtype),
        scratch_shapes=[pltpu.VMEM((tm, k), BF16), pltpu.VMEM((tm, k), BF16)],
        compiler_params=_params("arbitrary", "arbitrary"),
        name="norm_mm",
    )(a, a, g.reshape(1, k).astype(F32), w)


def _mm_relu2_kernel(a_ref, w_ref, o_ref):
    acc = jnp.dot(a_ref[...], w_ref[...], preferred_element_type=F32)
    o_ref[...] = jnp.square(jnp.maximum(acc, 0.0)).astype(o_ref.dtype)


def mm_relu2(a, w, *, tm, tn, out_dtype):
    m, k = a.shape
    n = w.shape[1]
    tm, tn = _tile(m, tm), _tile(n, tn)
    return pl.pallas_call(
        _mm_relu2_kernel,
        grid=(m // tm, n // tn),
        in_specs=[pl.BlockSpec((tm, k), lambda i, j: (i, 0)),
                  pl.BlockSpec((k, tn), lambda i, j: (0, j))],
        out_specs=pl.BlockSpec((tm, tn), lambda i, j: (i, j)),
        out_shape=jax.ShapeDtypeStruct((m, n), out_dtype),
        compiler_params=_params("parallel", "parallel"),
        name="mm_relu2",
    )(a, w)


def _mm_res_kernel(a_ref, w_ref, res_ref, o_ref):
    @pl.when(pl.program_id(2) == 0)
    def _():
        o_ref[...] = res_ref[...]

    o_ref[...] += jnp.dot(a_ref[...], w_ref[...], preferred_element_type=F32)


def mm_res(a, w, res, *, tm, tn, tk):
    m, k = a.shape
    n = w.shape[1]
    tm, tn, tk = _tile(m, tm), _tile(n, tn), _tile(k, tk)
    return pl.pallas_call(
        _mm_res_kernel,
        grid=(m // tm, n // tn, k // tk),
        in_specs=[pl.BlockSpec((tm, tk), lambda i, j, kk: (i, kk)),
                  pl.BlockSpec((tk, tn), lambda i, j, kk: (kk, j)),
                  pl.BlockSpec((tm, tn), lambda i, j, kk: (i, j))],
        out_specs=pl.BlockSpec((tm, tn), lambda i, j, kk: (i, j)),
        out_shape=jax.ShapeDtypeStruct((m, n), F32),
        compiler_params=_params("parallel", "parallel", "arbitrary"),
        name="mm_res",
    )(a, w, res)


def _rmsnorm_kernel(a_ref, g_ref, o_ref):
    _rms_rows(a_ref, g_ref, o_ref)


def rmsnorm(a, g, *, tm, out_dtype):
    m, k = a.shape
    tm = _tile(m, tm)
    return pl.pallas_call(
        _rmsnorm_kernel,
        grid=(m // tm,),
        in_specs=[pl.BlockSpec((tm, k), lambda i: (i, 0)), pl.BlockSpec((1, k), lambda i: (0, 0))],
        out_specs=pl.BlockSpec((tm, k), lambda i: (i, 0)),
        out_shape=jax.ShapeDtypeStruct((m, k), out_dtype),
        compiler_params=_params("parallel"),
        name="rmsnorm",
    )(a, g.reshape(1, k).astype(F32))


def _chan_dft_kernel(z_ref, cs_ref, p_ref, q_ref, *, groups, gdim):
    for g in range(groups):
        pq = jnp.dot(z_ref[:, g * gdim:(g + 1) * gdim], cs_ref[...], preferred_element_type=F32)
        p_ref[:, g * gdim:(g + 1) * gdim] = pq[:, :gdim].astype(p_ref.dtype)
        q_ref[:, g * gdim:(g + 1) * gdim] = pq[:, gdim:].astype(q_ref.dtype)


def chan_dft(z, cs, *, width, groups, tm):
    m = z.shape[0]
    gdim = width // groups
    tm = _tile(m, tm)
    out = jax.ShapeDtypeStruct((m, width), BF16)
    return pl.pallas_call(
        functools.partial(_chan_dft_kernel, groups=groups, gdim=gdim),
        grid=(m // tm,),
        in_specs=[pl.BlockSpec((tm, width), lambda i: (i, 0)),
                  pl.BlockSpec((gdim, 2 * gdim), lambda i: (0, 0))],
        out_specs=[pl.BlockSpec((tm, width), lambda i: (i, 0))] * 2,
        out_shape=[out, out],
        compiler_params=_params("parallel"),
        name="chan_dft",
    )(z, cs)


def _seq_dft1_kernel(p_ref, q_ref, tp_ref, tq_ref, o_ref, *, nb, width):
    for t in range(nb):
        acc = jnp.dot(tp_ref[t], p_ref[0, t], preferred_element_type=F32)
        acc += jnp.dot(tq_ref[t], q_ref[0, t], preferred_element_type=F32)
        o_ref[0, :, t * width:(t + 1) * width] = acc.astype(o_ref.dtype)


def seq_dft1(pt, qt, tp, tq, *, nb):
    b, s2, s1, width = pt.shape
    nb = _tile(s2, nb)
    dspec = pl.BlockSpec((1, nb, s1, width), lambda bi, j: (bi, j, 0, 0))
    tspec = pl.BlockSpec((nb, 2 * s1, s1), lambda bi, j: (j, 0, 0))
    return pl.pallas_call(
        functools.partial(_seq_dft1_kernel, nb=nb, width=width),
        grid=(b, s2 // nb),
        in_specs=[dspec, dspec, tspec, tspec],
        out_specs=pl.BlockSpec((1, 2 * s1, nb * width), lambda bi, j: (bi, 0, j)),
        out_shape=jax.ShapeDtypeStruct((b, 2 * s1, s2 * width), BF16),
        compiler_params=_params("parallel", "parallel"),
        name="seq_dft1",
    )(pt, qt, tp, tq)


def _seq_dft2_kernel(re_ref, im_ref, c_ref, s_ref, o_ref, *, kb, width):
    for t in range(kb):
        acc = jnp.dot(c_ref[...], re_ref[0, t], preferred_element_type=F32)
        acc += jnp.dot(s_ref[...], im_ref[0, t], preferred_element_type=F32)
        o_ref[0, :, t * width:(t + 1) * width] = acc.astype(o_ref.dtype)


def seq_dft2(a4, c2, s2m, *, s1, kb):
    b, _, s2, width = a4.shape
    kb = _tile(s1, kb)
    nblk = s1 // kb
    return pl.pallas_call(
        functools.partial(_seq_dft2_kernel, kb=kb, width=width),
        grid=(b, nblk),
        in_specs=[
            pl.BlockSpec((1, kb, s2, width), lambda bi, j: (bi, j, 0, 0)),
            pl.BlockSpec((1, kb, s2, width), lambda bi, j: (bi, j + nblk, 0, 0)),
            pl.BlockSpec((s2, s2), lambda bi, j: (0, 0)),
            pl.BlockSpec((s2, s2), lambda bi, j: (0, 0)),
        ],
        out_specs=pl.BlockSpec((1, s2, kb * width), lambda bi, j: (bi, 0, j)),
        out_shape=jax.ShapeDtypeStruct((b, s2, s1 * width), BF16),
        compiler_params=_params("parallel", "parallel"),
        name="seq_dft2",
    )(a4, a4, c2, s2m)


def _dft_tables(seq, gdim):
    s2 = FOURIER_SEQ_INNER
    s1 = seq // s2
    c = np.arange(gdim)
    ang = 2.0 * np.pi * ((c[:, None] * c[None, :]) % gdim) / gdim
    cs = np.concatenate([np.cos(ang), np.sin(ang)], axis=1) / math.sqrt(gdim)
    k1 = np.arange(s1)
    n1 = np.arange(s1)
    n2 = np.arange(s2)
    idx = (k1[None, :, None] * (s2 * n1[None, None, :] + n2[:, None, None])) % seq
    phi = 2.0 * np.pi * idx / seq
    co, si = np.cos(phi), np.sin(phi)
    tp = np.concatenate([co, -si], axis=1)
    tq = np.concatenate([-si, -co], axis=1)
    k2 = np.arange(s2)
    ang2 = 2.0 * np.pi * ((k2[:, None] * n2[None, :]) % s2) / s2
    c2 = np.cos(ang2) / math.sqrt(seq)
    s2m = np.sin(ang2) / math.sqrt(seq)
    as_bf16 = lambda x: jnp.asarray(x, dtype=F32).astype(BF16)
    return as_bf16(cs), as_bf16(tp), as_bf16(tq), as_bf16(c2), as_bf16(s2m)


def _rope128(r, ct, sa, sb):
    quarter = LANES // 4
    return r * ct + pltpu.roll(r, quarter, 1) * sa + pltpu.roll(r, LANES - quarter, 1) * sb


def _q_proj_kernel(c_ref, g_ref, w_ref, ct_ref, sa_ref, sb_ref, o_ref, *, heads, scale):
    xn = _rms(c_ref[...].astype(F32), g_ref[...]).astype(BF16)
    acc = jnp.dot(xn, w_ref[...], preferred_element_type=F32)
    ct, sa, sb = ct_ref[...], sa_ref[...], sb_ref[...]
    for h in range(heads):
        base = 2 * LANES * h
        o_ref[:, base:base + LANES] = (acc[:, base:base + LANES] * scale).astype(o_ref.dtype)
        r = _rope128(acc[:, base + LANES:base + 2 * LANES], ct, sa, sb)
        o_ref[:, base + LANES:base + 2 * LANES] = (r * scale).astype(o_ref.dtype)


def q_proj(z, g, w, ct, sa, sb, *, col_block, rank, heads, scale, tm):
    m = z.shape[0]
    tm = _tile(m, tm)
    n = w.shape[1]
    tab = pl.BlockSpec((tm, LANES), lambda i: (i, 0))
    return pl.pallas_call(
        functools.partial(_q_proj_kernel, heads=heads, scale=scale),
        grid=(m // tm,),
        in_specs=[pl.BlockSpec((tm, rank), lambda i: (i, col_block)),
                  pl.BlockSpec((1, rank), lambda i: (0, 0)),
                  pl.BlockSpec((rank, n), lambda i: (0, 0)),
                  tab, tab, tab],
        out_specs=pl.BlockSpec((tm, n), lambda i: (i, 0)),
        out_shape=jax.ShapeDtypeStruct((m, n), BF16),
        compiler_params=_params("parallel"),
        name="q_proj",
    )(z, g.reshape(1, rank).astype(F32), w, ct, sa, sb)


def _kv_proj_kernel(c_ref, kr_ref, g_ref, w_ref, ct_ref, sa_ref, sb_ref, k_ref, v_ref, *, heads):
    xn = _rms(c_ref[...].astype(F32), g_ref[...]).astype(BF16)
    acc = jnp.dot(xn, w_ref[...], preferred_element_type=F32)
    kr = _rope128(kr_ref[...].astype(F32), ct_ref[...], sa_ref[...], sb_ref[...]).astype(k_ref.dtype)
    for h in range(heads):
        k_ref[:, 2 * LANES * h:2 * LANES * h + LANES] = acc[:, LANES * h:LANES * (h + 1)].astype(k_ref.dtype)
        k_ref[:, 2 * LANES * h + LANES:2 * LANES * (h + 1)] = kr
        v_ref[:, 2 * LANES * h:2 * LANES * h + LANES] = acc[:, LANES * (heads + h):LANES * (heads + h + 1)].astype(v_ref.dtype)
        v_ref[:, 2 * LANES * h + LANES:2 * LANES * (h + 1)] = jnp.ones((acc.shape[0], LANES), v_ref.dtype)


def kv_proj(z, g, w, ct, sa, sb, *, ckv_block, kr_block, rank, heads, tm):
    m = z.shape[0]
    tm = _tile(m, tm)
    tab = pl.BlockSpec((tm, LANES), lambda i: (i, 0))
    return pl.pallas_call(
        functools.partial(_kv_proj_kernel, heads=heads),
        grid=(m // tm,),
        in_specs=[pl.BlockSpec((tm, rank), lambda i: (i, ckv_block)),
                  pl.BlockSpec((tm, LANES), lambda i: (i, kr_block)),
                  pl.BlockSpec((1, rank), lambda i: (0, 0)),
                  pl.BlockSpec((rank, 2 * LANES * heads), lambda i: (0, 0)),
                  tab, tab, tab],
        out_specs=[pl.BlockSpec((tm, 2 * LANES * heads), lambda i: (i, 0))] * 2,
        out_shape=[jax.ShapeDtypeStruct((m, 2 * LANES * heads), BF16)] * 2,
        compiler_params=_params("parallel"),
        name="kv_proj",
    )(z, z, g.reshape(1, rank).astype(F32), w, ct, sa, sb)


def _flash_kernel(q_ref, k_ref, v_ref, *rest, tk, nk, ncast):
    cast_in, o_ref, cast_out = rest[:ncast], rest[ncast], rest[ncast + 1:2 * ncast + 1]
    s0_ref, s1_ref = rest[2 * ncast + 1:]
    for src, dst in zip(cast_in, cast_out):
        dst[...] = src[...].astype(dst.dtype)
    q = q_ref[0]
    tq = q.shape[0]
    dv = o_ref.shape[-1]

    def chunk(c):
        start = c * tk
        return pl.ds(start if isinstance(start, int) else pl.multiple_of(start, tk), tk)

    def scores(c):
        return lax.dot_general(q, k_ref[0, chunk(c), :], (((1,), (1,)), ((), ())), preferred_element_type=F32)

    def fold(s_ref, c, m, acc):
        s = s_ref[...]
        m_new = jnp.maximum(m, jnp.max(s, axis=1, keepdims=True))
        p = jnp.exp2(s - m_new).astype(BF16)
        acc = jnp.exp2(m - m_new) * acc + jnp.dot(p, v_ref[0, chunk(c), :], preferred_element_type=F32)
        return m_new, acc

    def pair(j, carry, last):
        m, acc = carry
        s1_ref[...] = scores(2 * j + 1)
        m, acc = fold(s0_ref, 2 * j, m, acc)
        if not last:
            s0_ref[...] = scores(2 * j + 2)
        return fold(s1_ref, 2 * j + 1, m, acc)

    s0_ref[...] = scores(0)
    init = (jnp.full((tq, 1), -jnp.inf, F32), jnp.zeros((tq, 2 * dv), F32))
    carry = init
    for j in range(nk // 2):
        carry = pair(j, carry, j == nk // 2 - 1)
    _, acc = carry
    o_ref[0] = (acc[:, :dv] / acc[:, dv:]).astype(o_ref.dtype)


def flash_attention(q, k, v, *, heads, tq, tk, cast_weights=()):
    b, s, _ = q.shape
    dk = q.shape[-1] // heads
    dv2 = v.shape[-1] // heads
    dv = dv2 // 2
    tq, tk = _tile(s, tq), _tile(s, min(tk, s // 2))
    nk = s // tk
    assert nk % 2 == 0
    nq = s // tq
    steps = b * heads * nq
    wspecs = []
    for w in cast_weights:
        rows = w.shape[0] // steps
        assert rows * steps == w.shape[0] and rows % 16 == 0, (w.shape, steps)
        wspecs.append(pl.BlockSpec((rows, w.shape[1]), lambda bi, h, i: ((bi * heads + h) * nq + i, 0)))
    wspecs_in = wspecs_out = wspecs
    wshapes = [jax.ShapeDtypeStruct(w.shape, BF16) for w in cast_weights]
    outs = pl.pallas_call(
        functools.partial(_flash_kernel, tk=tk, nk=nk, ncast=len(cast_weights)),
        grid=(b, heads, nq),
        in_specs=[pl.BlockSpec((1, tq, dk), lambda bi, h, i: (bi, i, h)),
                  pl.BlockSpec((1, s, dk), lambda bi, h, i: (bi, 0, h)),
                  pl.BlockSpec((1, s, dv2), lambda bi, h, i: (bi, 0, h))] + wspecs_in,
        out_specs=[pl.BlockSpec((1, tq, dv), lambda bi, h, i: (bi, i, h))] + wspecs_out,
        out_shape=[jax.ShapeDtypeStruct((b, s, heads * dv), BF16)] + wshapes,
        scratch_shapes=[pltpu.VMEM((tq, tk), F32), pltpu.VMEM((tq, tk), F32)],
        compiler_params=_params("parallel", "parallel", "arbitrary"),
        name="flash_attention",
    )(q, k, v, *cast_weights)
    return outs[0], outs[1:]


def _out_proj_kernel(f0_ref, a0_ref, fn_ref, an_ref, wf_ref, gf_ref, ga_ref, w_ref, x_ref, o_ref,
                     ya_ref, yb_ref, *, groups, gdim, slice_rows, n_slices):
    fw = groups * gdim
    i, j = pl.program_id(0), pl.program_id(1)

    def mix_rows(f_ref, a_ref, y_ref, r):
        parts = [jnp.dot(f_ref[r, g * gdim:(g + 1) * gdim], wf_ref[g], preferred_element_type=F32)
                 for g in range(groups)]
        ms = sum(jnp.sum(p * p, axis=-1, keepdims=True) for p in parts) / fw
        inv = lax.rsqrt(ms + NORM_EPS)
        for g in range(groups):
            sl = slice(g * gdim, (g + 1) * gdim)
            y_ref[r, sl] = (parts[g] * inv * gf_ref[:, sl]).astype(BF16)
        y_ref[r, fw:] = _rms(a_ref[r, :].astype(F32), ga_ref[...]).astype(BF16)

    @pl.when((i == 0) & (j == 0))
    def _():
        def body(c, carry):
            mix_rows(f0_ref, a0_ref, ya_ref, pl.ds(pl.multiple_of(c * slice_rows, slice_rows), slice_rows))
            return carry

        lax.fori_loop(0, n_slices, body, 0)

    def step(cur_ref, nxt_ref):
        start = jnp.minimum(j, n_slices - 1) * slice_rows
        mix_rows(fn_ref, an_ref, nxt_ref, pl.ds(pl.multiple_of(start, slice_rows), slice_rows))
        acc = jnp.dot(cur_ref[...], w_ref[...], preferred_element_type=F32)
        o_ref[...] = (acc + x_ref[...]).astype(o_ref.dtype)

    @pl.when(i % 2 == 0)
    def _():
        step(ya_ref, yb_ref)

    @pl.when(i % 2 == 1)
    def _():
        step(yb_ref, ya_ref)


def out_proj(f, wf, gf, a, ga, w, x, *, tm, tn):
    m, fw = f.shape
    aw = a.shape[1]
    groups, gdim, _ = wf.shape
    n = w.shape[1]
    tm, tn = _tile(m, tm), _tile(n, tn)
    ni, nj = m // tm, n // tn
    n_slices = 1
    while n_slices * 2 <= nj and (tm // (n_slices * 2)) % NORM_CHUNK_ROWS == 0:
        n_slices *= 2
    first = lambda width: pl.BlockSpec((tm, width), lambda i, j: (0, 0), pipeline_mode=pl.Buffered(1))
    nxt = lambda width: pl.BlockSpec((tm, width), lambda i, j: (jnp.minimum(i + 1, ni - 1), 0))
    return pl.pallas_call(
        functools.partial(_out_proj_kernel, groups=groups, gdim=gdim, slice_rows=tm // n_slices,
                          n_slices=n_slices),
        grid=(ni, nj),
        in_specs=[first(fw), first(aw), nxt(fw), nxt(aw),
                  pl.BlockSpec((groups, gdim, gdim), lambda i, j: (0, 0, 0)),
                  pl.BlockSpec((1, fw), lambda i, j: (0, 0)),
                  pl.BlockSpec((1, aw), lambda i, j: (0, 0)),
                  pl.BlockSpec((fw + aw, tn), lambda i, j: (0, j)),
                  pl.BlockSpec((tm, tn), lambda i, j: (i, j))],
        out_specs=pl.BlockSpec((tm, tn), lambda i, j: (i, j)),
        out_shape=jax.ShapeDtypeStruct((m, n), F32),
        scratch_shapes=[pltpu.VMEM((tm, fw + aw), BF16), pltpu.VMEM((tm, fw + aw), BF16)],
        compiler_params=_params("arbitrary", "arbitrary"),
        name="out_proj",
    )(f, a, f, a, wf, gf.reshape(1, fw).astype(F32), ga.reshape(1, aw).astype(F32), w, x)


def _absorb_qk_kernel(wq_ref, k_ref, o_ref, *, scale):
    acc = lax.dot_general(wq_ref[...], k_ref[...], (((1,), (1,)), ((), ())), preferred_element_type=F32)
    o_ref[0] = (acc * scale).astype(o_ref.dtype)


def absorb_qk(wq, kv, *, batch, heads, mem, hd, scale):
    d = wq.shape[0]
    return pl.pallas_call(
        functools.partial(_absorb_qk_kernel, scale=scale),
        grid=(heads, batch),
        in_specs=[pl.BlockSpec((d, hd), lambda h, b: (0, h)),
                  pl.BlockSpec((mem, hd), lambda h, b: (b, h))],
        out_specs=pl.BlockSpec((1, d, mem), lambda h, b: (b, 0, h)),
        out_shape=jax.ShapeDtypeStruct((batch, d, heads * mem), BF16),
        compiler_params=_params("parallel", "parallel"),
        name="absorb_qk",
    )(wq, kv)


def _absorb_vo_kernel(v_ref, wo_ref, o_ref):
    o_ref[0] = jnp.dot(v_ref[...], wo_ref[...], preferred_element_type=F32).astype(o_ref.dtype)


def absorb_vo(kv, wo, *, batch, heads, mem, hd, tn):
    d = wo.shape[1]
    tn = _tile(d, tn)
    return pl.pallas_call(
        _absorb_vo_kernel,
        grid=(heads, d // tn, batch),
        in_specs=[pl.BlockSpec((mem, hd), lambda h, j, b: (b, h)),
                  pl.BlockSpec((hd, tn), lambda h, j, b: (h, j))],
        out_specs=pl.BlockSpec((1, mem, tn), lambda h, j, b: (b, h, j)),
        out_shape=jax.ShapeDtypeStruct((batch, heads * mem, d), BF16),
        compiler_params=_params("parallel", "parallel", "parallel"),
        name="absorb_vo",
    )(kv, wo)


def _xattn_kernel(hf_ref, g_ref, wqk_ref, vo_ref, g2_ref, o_ref, on_ref, p_ref, hn_ref, *, heads, mem):
    _rms_rows(hf_ref, g_ref, hn_ref)
    s = jnp.dot(hn_ref[...], wqk_ref[0], preferred_element_type=F32)
    for h in range(heads):
        seg = s[:, h * mem:(h + 1) * mem]
        e = jnp.exp(seg - jnp.max(seg, axis=-1, keepdims=True))
        p_ref[:, h * mem:(h + 1) * mem] = (e / jnp.sum(e, axis=-1, keepdims=True)).astype(BF16)
    o_ref[...] = jnp.dot(p_ref[...], vo_ref[0], preferred_element_type=F32) + hf_ref[...]
    _rms_rows(o_ref, g2_ref, on_ref)


def xattn(h, g, wqk, vo, g_next, *, seq, heads, mem, tm):
    m, d = h.shape
    tm = _tile(seq, tm)
    per_b = seq // tm
    hm = heads * mem
    once = pl.Buffered(1)
    return pl.pallas_call(
        functools.partial(_xattn_kernel, heads=heads, mem=mem),
        grid=(m // tm,),
        in_specs=[pl.BlockSpec((tm, d), lambda i: (i, 0)),
                  pl.BlockSpec((1, d), lambda i: (0, 0)),
                  pl.BlockSpec((1, d, hm), lambda i: (i // per_b, 0, 0), pipeline_mode=once),
                  pl.BlockSpec((1, hm, d), lambda i: (i // per_b, 0, 0), pipeline_mode=once),
                  pl.BlockSpec((1, d), lambda i: (0, 0))],
        out_specs=[pl.BlockSpec((tm, d), lambda i: (i, 0))] * 2,
        out_shape=[jax.ShapeDtypeStruct((m, d), F32), jax.ShapeDtypeStruct((m, d), BF16)],
        scratch_shapes=[pltpu.VMEM((tm, hm), BF16), pltpu.VMEM((tm, d), BF16)],
        compiler_params=_params("parallel"),
        name="xattn",
    )(h, g.reshape(1, d).astype(F32), wqk, vo, g_next.reshape(1, d).astype(F32))


def _pad_cols(w, n):
    return jnp.pad(w, ((0, 0), (0, n - w.shape[1])))


def kernel(x, mem, positions, g_mix, w_in, w_fourier, g_q_lora, w_uq, g_kv_lora, w_ukv, g_fourier_out, g_mla_out, w_out, g_xattn, g_mem, w_xq, w_xk, w_xv, w_xo, g_mlp, w_ff1, w_ff2, g_final):
    batch, seq, d = x.shape
    depth = g_mix.shape[0]
    groups, gdim = w_fourier.shape[1], w_fourier.shape[2]
    fw = groups * gdim
    q_rank, heads, qk_dim = w_uq.shape[1], w_uq.shape[2], w_uq.shape[3]
    kv_rank = w_ukv.shape[1]
    nope = LANES
    rope = qk_dim - nope
    vdim = w_ukv.shape[3] - nope
    assert rope == LANES // 2 and vdim == LANES and fw % q_rank == 0
    assert (fw + q_rank) % kv_rank == 0 and (fw + q_rank + kv_rank) % LANES == 0
    xheads, xhd = w_xq.shape[2], w_xq.shape[3]
    mtok = mem.shape[1]
    t = batch * seq
    s2 = FOURIER_SEQ_INNER
    s1 = seq // s2

    half = rope // 2
    inv_freq = ROPE_THETA ** (-jnp.arange(half, dtype=F32) / half)
    ang = positions.astype(F32).reshape(t, 1) * inv_freq
    cos, sin = jnp.cos(ang), jnp.sin(ang)
    zq = jnp.zeros_like(cos)
    ct = jnp.concatenate([cos, cos, zq, zq], axis=1)
    sa = jnp.concatenate([zq, sin, zq, zq], axis=1)
    sb = jnp.concatenate([-sin, zq, zq, zq], axis=1)

    cs, tp, tq_tab, c2, s2m = _dft_tables(seq, gdim)

    in_width = w_in.shape[2]
    z_width = -(-(in_width + rope) // 768) * 768
    scale = float(qk_dim) ** -0.5 * math.log2(math.e)

    h = x.reshape(t, d)
    mem2 = mem.reshape(batch * mtok, d)
    for layer in range(depth):
        w_in_b = _pad_cols(w_in[layer], z_width).astype(BF16)
        wq = jnp.pad(w_uq[layer], ((0, 0), (0, 0), (0, 2 * LANES - qk_dim))).reshape(q_rank, heads * 2 * LANES).astype(BF16)
        wkv = jnp.concatenate([w_ukv[layer][:, :, :nope].reshape(kv_rank, heads * nope),
                               w_ukv[layer][:, :, nope:].reshape(kv_rank, heads * vdim)], axis=1).astype(BF16)
        wf = w_fourier[layer].astype(BF16)
        wxq = w_xq[layer].reshape(d, xheads * xhd).astype(BF16)
        wxk = w_xk[layer].reshape(d, xheads * xhd).astype(BF16)
        wxv = w_xv[layer].reshape(d, xheads * xhd).astype(BF16)

        z = norm_mm(h, g_mix[layer], w_in_b, tm=512, tn=768, out_dtype=BF16)
        p, q = chan_dft(z, cs, width=fw, groups=groups, tm=1024)
        pt = p.reshape(batch, s1, s2, fw).transpose(0, 2, 1, 3)
        qt = q.reshape(batch, s1, s2, fw).transpose(0, 2, 1, 3)
        a = seq_dft1(pt, qt, tp, tq_tab, nb=8)
        f = seq_dft2(a.reshape(batch, 2 * s1, s2, fw), c2, s2m, s1=s1, kb=4).reshape(t, fw)
        qh = q_proj(z, g_q_lora[layer], wq, ct, sa, sb, col_block=fw // q_rank, rank=q_rank,
                    heads=heads, scale=scale, tm=512)
        kh, vh = kv_proj(z, g_kv_lora[layer], wkv, ct, sa, sb, ckv_block=(fw + q_rank) // kv_rank,
                         kr_block=(fw + q_rank + kv_rank) // LANES, rank=kv_rank, heads=heads, tm=512)
        o, (w_out_b, wxo, w1, w2) = flash_attention(
            qh.reshape(batch, seq, -1), kh.reshape(batch, seq, -1), vh.reshape(batch, seq, -1),
            heads=heads, tq=1024, tk=512,
            cast_weights=(w_out[layer], w_xo[layer].reshape(xheads * xhd, d), w_ff1[layer], w_ff2[layer]))
        o = o.reshape(t, heads * vdim)
        h = out_proj(f, wf, g_fourier_out[layer], o, g_mla_out[layer], w_out_b, h, tm=512, tn=512)

        km = norm_mm(mem2, g_mem[layer], wxk, tm=512, tn=512, out_dtype=BF16)
        vm = norm_mm(mem2, g_mem[layer], wxv, tm=512, tn=512, out_dtype=BF16)
        wqk = absorb_qk(wxq, km, batch=batch, heads=xheads, mem=mtok, hd=xhd, scale=float(xhd) ** -0.5)
        vo = absorb_vo(vm, wxo, batch=batch, heads=xheads, mem=mtok, hd=xhd, tn=2048)
        h, hn = xattn(h, g_xattn[layer], wqk, vo, g_mlp[layer], seq=seq, heads=xheads, mem=mtok, tm=256)

        act = mm_relu2(hn, w1, tm=1024, tn=1024, out_dtype=BF16)
        h = mm_res(act, w2, h, tm=1024, tn=1024, tk=4096)

    return rmsnorm(h, g_final, tm=256, out_dtype=x.dtype).reshape(batch, seq, d)
```

```python
import functools
import math

import numpy as np
import jax
import jax.numpy as jnp
from jax import lax
from jax.experimental import pallas as pl
from jax.experimental.pallas import tpu as pltpu

F32 = jnp.float32
BF16 = jnp.bfloat16

NORM_EPS = 1e-6
ROPE_THETA = 10000.0
LANES = 128
VMEM_LIMIT_BYTES = 56 * 1024 * 1024
FOURIER_SEQ_INNER = 128


def _params(*sem):
    return pltpu.CompilerParams(dimension_semantics=sem, vmem_limit_bytes=VMEM_LIMIT_BYTES)


def _rms(xf, g):
    ms = jnp.mean(xf * xf, axis=-1, keepdims=True)
    return xf * lax.rsqrt(ms + NORM_EPS) * g


def _rms_rows(a_ref, g_ref, o_ref, chunk=64):
    rows = a_ref.shape[0]
    chunk = min(chunk, rows)

    def body(c, carry):
        r = pl.ds(pl.multiple_of(c * chunk, chunk), chunk)
        o_ref[r, :] = _rms(a_ref[r, :].astype(F32), g_ref[...]).astype(o_ref.dtype)
        return carry

    lax.fori_loop(0, rows // chunk, body, 0)


def _tile(dim, pref):
    t = min(dim, pref)
    assert dim % t == 0, (dim, pref)
    return t


NORM_CHUNK_ROWS = 32


def _norm_mm_kernel(a0_ref, an_ref, g_ref, w_ref, o_ref, xa_ref, xb_ref, *, slice_rows, n_slices):
    i, j = pl.program_id(0), pl.program_id(1)

    @pl.when((i == 0) & (j == 0))
    def _():
        _rms_rows(a0_ref, g_ref, xa_ref)

    def step(cur_ref, nxt_ref):
        start = jnp.minimum(j, n_slices - 1) * slice_rows
        for c in range(slice_rows // NORM_CHUNK_ROWS):
            r = pl.ds(pl.multiple_of(start + c * NORM_CHUNK_ROWS, NORM_CHUNK_ROWS), NORM_CHUNK_ROWS)
            nxt_ref[r, :] = _rms(an_ref[r, :].astype(F32), g_ref[...]).astype(BF16)
        o_ref[...] = jnp.dot(cur_ref[...], w_ref[...], preferred_element_type=F32).astype(o_ref.dtype)

    @pl.when(i % 2 == 0)
    def _():
        step(xa_ref, xb_ref)

    @pl.when(i % 2 == 1)
    def _():
        step(xb_ref, xa_ref)


def norm_mm(a, g, w, *, tm, tn, out_dtype):
    m, k = a.shape
    n = w.shape[1]
    tm, tn = _tile(m, tm), _tile(n, tn)
    ni, nj = m // tm, n // tn
    n_slices = 1
    while n_slices * 2 <= nj and (tm // (n_slices * 2)) % NORM_CHUNK_ROWS == 0:
        n_slices *= 2
    return pl.pallas_call(
        functools.partial(_norm_mm_kernel, slice_rows=tm // n_slices, n_slices=n_slices),
        grid=(ni, nj),
        in_specs=[pl.BlockSpec((tm, k), lambda i, j: (0, 0), pipeline_mode=pl.Buffered(1)),
                  pl.BlockSpec((tm, k), lambda i, j: (jnp.minimum(i + 1, ni - 1), 0)),
                  pl.BlockSpec((1, k), lambda i, j: (0, 0)),
                  pl.BlockSpec((k, tn), lambda i, j: (0, j))],
        out_specs=pl.BlockSpec((tm, tn), lambda i, j: (i, j)),
        out_shape=jax.ShapeDtypeStruct((m, n), out_dtype),
        scratch_shapes=[pltpu.VMEM((tm, k), BF16), pltpu.VMEM((tm, k), BF16)],
        compiler_params=_params("arbitrary", "arbitrary"),
        name="norm_mm",
    )(a, a, g.reshape(1, k).astype(F32), w)


def _mm_relu2_kernel(a_ref, w_ref, o_ref):
    acc = jnp.dot(a_ref[...], w_ref[...], preferred_element_type=F32)
    o_ref[...] = jnp.square(jnp.maximum(acc, 0.0)).astype(o_ref.dtype)


def mm_relu2(a, w, *, tm, tn, out_dtype):
    m, k = a.shape
    n = w.shape[1]
    tm, tn = _tile(m, tm), _tile(n, tn)
    return pl.pallas_call(
        _mm_relu2_kernel,
        grid=(m // tm, n // tn),
        in_specs=[pl.BlockSpec((tm, k), lambda i, j: (i, 0)),
                  pl.BlockSpec((k, tn), lambda i, j: (0, j))],
        out_specs=pl.BlockSpec((tm, tn), lambda i, j: (i, j)),
        out_shape=jax.ShapeDtypeStruct((m, n), out_dtype),
        compiler_params=_params("parallel", "parallel"),
        name="mm_relu2",
    )(a, w)


def _mm_res_kernel(a_ref, w_ref, res_ref, o_ref):
    @pl.when(pl.program_id(2) == 0)
    def _():
        o_ref[...] = res_ref[...]

    o_ref[...] += jnp.dot(a_ref[...], w_ref[...], preferred_element_type=F32)


def mm_res(a, w, res, *, tm, tn, tk):
    m, k = a.shape
    n = w.shape[1]
    tm, tn, tk = _tile(m, tm), _tile(n, tn), _tile(k, tk)
    return pl.pallas_call(
        _mm_res_kernel,
        grid=(m // tm, n // tn, k // tk),
        in_specs=[pl.BlockSpec((tm, tk), lambda i, j, kk: (i, kk)),
                  pl.BlockSpec((tk, tn), lambda i, j, kk: (kk, j)),
                  pl.BlockSpec((tm, tn), lambda i, j, kk: (i, j))],
        out_specs=pl.BlockSpec((tm, tn), lambda i, j, kk: (i, j)),
        out_shape=jax.ShapeDtypeStruct((m, n), F32),
        compiler_params=_params("parallel", "parallel", "arbitrary"),
        name="mm_res",
    )(a, w, res)


def _rmsnorm_kernel(a_ref, g_ref, o_ref):
    _rms_rows(a_ref, g_ref, o_ref)


def rmsnorm(a, g, *, tm, out_dtype):
    m, k = a.shape
    tm = _tile(m, tm)
    return pl.pallas_call(
        _rmsnorm_kernel,
        grid=(m // tm,),
        in_specs=[pl.BlockSpec((tm, k), lambda i: (i, 0)), pl.BlockSpec((1, k), lambda i: (0, 0))],
        out_specs=pl.BlockSpec((tm, k), lambda i: (i, 0)),
        out_shape=jax.ShapeDtypeStruct((m, k), out_dtype),
        compiler_params=_params("parallel"),
        name="rmsnorm",
    )(a, g.reshape(1, k).astype(F32))


def _chan_dft_kernel(z_ref, cs_ref, p_ref, q_ref, *, groups, gdim):
    for g in range(groups):
        pq = jnp.dot(z_ref[:, g * gdim:(g + 1) * gdim], cs_ref[...], preferred_element_type=F32)
        p_ref[:, g * gdim:(g + 1) * gdim] = pq[:, :gdim].astype(p_ref.dtype)
        q_ref[:, g * gdim:(g + 1) * gdim] = pq[:, gdim:].astype(q_ref.dtype)


def chan_dft(z, cs, *, width, groups, tm):
    m = z.shape[0]
    gdim = width // groups
    tm = _tile(m, tm)
    out = jax.ShapeDtypeStruct((m, width), BF16)
    return pl.pallas_call(
        functools.partial(_chan_dft_kernel, groups=groups, gdim=gdim),
        grid=(m // tm,),
        in_specs=[pl.BlockSpec((tm, width), lambda i: (i, 0)),
                  pl.BlockSpec((gdim, 2 * gdim), lambda i: (0, 0))],
        out_specs=[pl.BlockSpec((tm, width), lambda i: (i, 0))] * 2,
        out_shape=[out, out],
        compiler_params=_params("parallel"),
        name="chan_dft",
    )(z, cs)


def _seq_dft1_kernel(p_ref, q_ref, tp_ref, tq_ref, o_ref, *, nb, width):
    for t in range(nb):
        acc = jnp.dot(tp_ref[t], p_ref[0, t], preferred_element_type=F32)
        acc += jnp.dot(tq_ref[t], q_ref[0, t], preferred_element_type=F32)
        o_ref[0, :, t * width:(t + 1) * width] = acc.astype(o_ref.dtype)


def seq_dft1(pt, qt, tp, tq, *, nb):
    b, s2, s1, width = pt.shape
    nb = _tile(s2, nb)
    dspec = pl.BlockSpec((1, nb, s1, width), lambda bi, j: (bi, j, 0, 0))
    tspec = pl.BlockSpec((nb, 2 * s1, s1), lambda bi, j: (j, 0, 0))
    return pl.pallas_call(
        functools.partial(_seq_dft1_kernel, nb=nb, width=width),
        grid=(b, s2 // nb),
        in_specs=[dspec, dspec, tspec, tspec],
        out_specs=pl.BlockSpec((1, 2 * s1, nb * width), lambda bi, j: (bi, 0, j)),
        out_shape=jax.ShapeDtypeStruct((b, 2 * s1, s2 * width), BF16),
        compiler_params=_params("parallel", "parallel"),
        name="seq_dft1",
    )(pt, qt, tp, tq)


def _seq_dft2_kernel(re_ref, im_ref, c_ref, s_ref, wf_ref, gf_ref, o_ref, *, kb, width):
    groups, gdim, _ = wf_ref.shape
    for t in range(kb):
        acc = jnp.dot(c_ref[...], re_ref[0, t], preferred_element_type=F32)
        acc += jnp.dot(s_ref[...], im_ref[0, t], preferred_element_type=F32)
        f = acc.astype(BF16)
        parts = [jnp.dot(f[:, g * gdim:(g + 1) * gdim], wf_ref[g], preferred_element_type=F32)
                 for g in range(groups)]
        ms = sum(jnp.sum(p * p, axis=-1, keepdims=True) for p in parts) / width
        inv = lax.rsqrt(ms + NORM_EPS)
        for g in range(groups):
            sl = slice(g * gdim, (g + 1) * gdim)
            o_ref[0, :, t * width + g * gdim:t * width + (g + 1) * gdim] = (
                parts[g] * inv * gf_ref[:, sl]).astype(o_ref.dtype)


def seq_dft2(a4, c2, s2m, wf, gf, *, s1, kb):
    b, _, s2, width = a4.shape
    kb = _tile(s1, kb)
    nblk = s1 // kb
    return pl.pallas_call(
        functools.partial(_seq_dft2_kernel, kb=kb, width=width),
        grid=(b, nblk),
        in_specs=[
            pl.BlockSpec((1, kb, s2, width), lambda bi, j: (bi, j, 0, 0)),
            pl.BlockSpec((1, kb, s2, width), lambda bi, j: (bi, j + nblk, 0, 0)),
            pl.BlockSpec((s2, s2), lambda bi, j: (0, 0)),
            pl.BlockSpec((s2, s2), lambda bi, j: (0, 0)),
            pl.BlockSpec(wf.shape, lambda bi, j: (0, 0, 0)),
            pl.BlockSpec((1, width), lambda bi, j: (0, 0)),
        ],
        out_specs=pl.BlockSpec((1, s2, kb * width), lambda bi, j: (bi, 0, j)),
        out_shape=jax.ShapeDtypeStruct((b, s2, s1 * width), BF16),
        compiler_params=_params("parallel", "parallel"),
        name="seq_dft2",
    )(a4, a4, c2, s2m, wf, gf.reshape(1, width).astype(F32))


def _dft_tables(seq, gdim):
    s2 = FOURIER_SEQ_INNER
    s1 = seq // s2
    c = np.arange(gdim)
    ang = 2.0 * np.pi * ((c[:, None] * c[None, :]) % gdim) / gdim
    cs = np.concatenate([np.cos(ang), np.sin(ang)], axis=1) / math.sqrt(gdim)
    k1 = np.arange(s1)
    n1 = np.arange(s1)
    n2 = np.arange(s2)
    idx = (k1[None, :, None] * (s2 * n1[None, None, :] + n2[:, None, None])) % seq
    phi = 2.0 * np.pi * idx / seq
    co, si = np.cos(phi), np.sin(phi)
    tp = np.concatenate([co, -si], axis=1)
    tq = np.concatenate([-si, -co], axis=1)
    k2 = np.arange(s2)
    ang2 = 2.0 * np.pi * ((k2[:, None] * n2[None, :]) % s2) / s2
    c2 = np.cos(ang2) / math.sqrt(seq)
    s2m = np.sin(ang2) / math.sqrt(seq)
    as_bf16 = lambda x: jnp.asarray(x, dtype=F32).astype(BF16)
    return as_bf16(cs), as_bf16(tp), as_bf16(tq), as_bf16(c2), as_bf16(s2m)


def _rope128(r, ct, sa, sb):
    quarter = LANES // 4
    return r * ct + pltpu.roll(r, quarter, 1) * sa + pltpu.roll(r, LANES - quarter, 1) * sb


def _q_proj_kernel(c_ref, g_ref, w_ref, ct_ref, sa_ref, sb_ref, o_ref, *, heads, scale):
    xn = _rms(c_ref[...].astype(F32), g_ref[...]).astype(BF16)
    acc = jnp.dot(xn, w_ref[...], preferred_element_type=F32)
    ct, sa, sb = ct_ref[...], sa_ref[...], sb_ref[...]
    for h in range(heads):
        base = 2 * LANES * h
        o_ref[:, base:base + LANES] = (acc[:, base:base + LANES] * scale).astype(o_ref.dtype)
        r = _rope128(acc[:, base + LANES:base + 2 * LANES], ct, sa, sb)
        o_ref[:, base + LANES:base + 2 * LANES] = (r * scale).astype(o_ref.dtype)


def q_proj(z, g, w, ct, sa, sb, *, col_block, rank, heads, scale, tm):
    m = z.shape[0]
    tm = _tile(m, tm)
    n = w.shape[1]
    tab = pl.BlockSpec((tm, LANES), lambda i: (i, 0))
    return pl.pallas_call(
        functools.partial(_q_proj_kernel, heads=heads, scale=scale),
        grid=(m // tm,),
        in_specs=[pl.BlockSpec((tm, rank), lambda i: (i, col_block)),
                  pl.BlockSpec((1, rank), lambda i: (0, 0)),
                  pl.BlockSpec((rank, n), lambda i: (0, 0)),
                  tab, tab, tab],
        out_specs=pl.BlockSpec((tm, n), lambda i: (i, 0)),
        out_shape=jax.ShapeDtypeStruct((m, n), BF16),
        compiler_params=_params("parallel"),
        name="q_proj",
    )(z, g.reshape(1, rank).astype(F32), w, ct, sa, sb)


def _kv_proj_kernel(c_ref, kr_ref, g_ref, w_ref, ct_ref, sa_ref, sb_ref, k_ref, v_ref, *, heads):
    xn = _rms(c_ref[...].astype(F32), g_ref[...]).astype(BF16)
    acc = jnp.dot(xn, w_ref[...], preferred_element_type=F32)
    kr = _rope128(kr_ref[...].astype(F32), ct_ref[...], sa_ref[...], sb_ref[...]).astype(k_ref.dtype)
    for h in range(heads):
        k_ref[:, 2 * LANES * h:2 * LANES * h + LANES] = acc[:, LANES * h:LANES * (h + 1)].astype(k_ref.dtype)
        k_ref[:, 2 * LANES * h + LANES:2 * LANES * (h + 1)] = kr
        v_ref[:, 2 * LANES * h:2 * LANES * h + LANES] = acc[:, LANES * (heads + h):LANES * (heads + h + 1)].astype(v_ref.dtype)
        v_ref[:, 2 * LANES * h + LANES:2 * LANES * (h + 1)] = jnp.ones((acc.shape[0], LANES), v_ref.dtype)


def kv_proj(z, g, w, ct, sa, sb, *, ckv_block, kr_block, rank, heads, tm):
    m = z.shape[0]
    tm = _tile(m, tm)
    tab = pl.BlockSpec((tm, LANES), lambda i: (i, 0))
    return pl.pallas_call(
        functools.partial(_kv_proj_kernel, heads=heads),
        grid=(m // tm,),
        in_specs=[pl.BlockSpec((tm, rank), lambda i: (i, ckv_block)),
                  pl.BlockSpec((tm, LANES), lambda i: (i, kr_block)),
                  pl.BlockSpec((1, rank), lambda i: (0, 0)),
                  pl.BlockSpec((rank, 2 * LANES * heads), lambda i: (0, 0)),
                  tab, tab, tab],
        out_specs=[pl.BlockSpec((tm, 2 * LANES * heads), lambda i: (i, 0))] * 2,
        out_shape=[jax.ShapeDtypeStruct((m, 2 * LANES * heads), BF16)] * 2,
        compiler_params=_params("parallel"),
        name="kv_proj",
    )(z, z, g.reshape(1, rank).astype(F32), w, ct, sa, sb)


def _flash_kernel(q_ref, k_ref, v_ref, *rest, tk, nk, ncast):
    cast_in, o_ref, cast_out = rest[:ncast], rest[ncast], rest[ncast + 1:2 * ncast + 1]
    s0_ref, s1_ref = rest[2 * ncast + 1:]
    for src, dst in zip(cast_in, cast_out):
        dst[...] = src[...].astype(dst.dtype)
    q = q_ref[0]
    tq = q.shape[0]
    dv = o_ref.shape[-1]

    def chunk(c):
        start = c * tk
        return pl.ds(start if isinstance(start, int) else pl.multiple_of(start, tk), tk)

    def scores(c):
        return lax.dot_general(q, k_ref[0, chunk(c), :], (((1,), (1,)), ((), ())), preferred_element_type=F32)

    def fold(s_ref, c, m, acc):
        s = s_ref[...]
        m_new = jnp.maximum(m, jnp.max(s, axis=1, keepdims=True))
        p = jnp.exp2(s - m_new).astype(BF16)
        acc = jnp.exp2(m - m_new) * acc + jnp.dot(p, v_ref[0, chunk(c), :], preferred_element_type=F32)
        return m_new, acc

    def pair(j, carry, last):
        m, acc = carry
        s1_ref[...] = scores(2 * j + 1)
        m, acc = fold(s0_ref, 2 * j, m, acc)
        if not last:
            s0_ref[...] = scores(2 * j + 2)
        return fold(s1_ref, 2 * j + 1, m, acc)

    s0_ref[...] = scores(0)
    init = (jnp.full((tq, 1), -jnp.inf, F32), jnp.zeros((tq, 2 * dv), F32))
    carry = init
    for j in range(nk // 2):
        carry = pair(j, carry, j == nk // 2 - 1)
    _, acc = carry
    o_ref[0] = (acc[:, :dv] / acc[:, dv:]).astype(o_ref.dtype)


def flash_attention(q, k, v, *, heads, tq, tk, cast_weights=()):
    b, s, _ = q.shape
    dk = q.shape[-1] // heads
    dv2 = v.shape[-1] // heads
    dv = dv2 // 2
    tq, tk = _tile(s, tq), _tile(s, min(tk, s // 2))
    nk = s // tk
    assert nk % 2 == 0
    nq = s // tq
    steps = b * heads * nq
    wspecs = []
    for w in cast_weights:
        rows = w.shape[0] // steps
        assert rows * steps == w.shape[0] and rows % 16 == 0, (w.shape, steps)
        wspecs.append(pl.BlockSpec((rows, w.shape[1]), lambda bi, h, i: ((bi * heads + h) * nq + i, 0)))
    wspecs_in = wspecs_out = wspecs
    wshapes = [jax.ShapeDtypeStruct(w.shape, BF16) for w in cast_weights]
    outs = pl.pallas_call(
        functools.partial(_flash_kernel, tk=tk, nk=nk, ncast=len(cast_weights)),
        grid=(b, heads, nq),
        in_specs=[pl.BlockSpec((1, tq, dk), lambda bi, h, i: (bi, i, h)),
                  pl.BlockSpec((1, s, dk), lambda bi, h, i: (bi, 0, h)),
                  pl.BlockSpec((1, s, dv2), lambda bi, h, i: (bi, 0, h))] + wspecs_in,
        out_specs=[pl.BlockSpec((1, tq, dv), lambda bi, h, i: (bi, i, h))] + wspecs_out,
        out_shape=[jax.ShapeDtypeStruct((b, s, heads * dv), BF16)] + wshapes,
        scratch_shapes=[pltpu.VMEM((tq, tk), F32), pltpu.VMEM((tq, tk), F32)],
        compiler_params=_params("parallel", "parallel", "arbitrary"),
        name="flash_attention",
    )(q, k, v, *cast_weights)
    return outs[0], outs[1:]


def _out_proj_kernel(yf_ref, a_ref, ga_ref, w_ref, x_ref, o_ref, ya_ref):
    fw = yf_ref.shape[1]

    @pl.when(pl.program_id(1) == 0)
    def _():
        _rms_rows(a_ref, ga_ref, ya_ref)

    acc = jnp.dot(yf_ref[...], w_ref[:fw, :], preferred_element_type=F32)
    acc += jnp.dot(ya_ref[...], w_ref[fw:, :], preferred_element_type=F32)
    o_ref[...] = (acc + x_ref[...]).astype(o_ref.dtype)


def out_proj(yf, a, ga, w, x, *, tm, tn):
    m, fw = yf.shape
    aw = a.shape[1]
    n = w.shape[1]
    tm, tn = _tile(m, tm), _tile(n, tn)
    return pl.pallas_call(
        _out_proj_kernel,
        grid=(m // tm, n // tn),
        in_specs=[pl.BlockSpec((tm, fw), lambda i, j: (i, 0)),
                  pl.BlockSpec((tm, aw), lambda i, j: (i, 0)),
                  pl.BlockSpec((1, aw), lambda i, j: (0, 0)),
                  pl.BlockSpec((fw + aw, tn), lambda i, j: (0, j)),
                  pl.BlockSpec((tm, tn), lambda i, j: (i, j))],
        out_specs=pl.BlockSpec((tm, tn), lambda i, j: (i, j)),
        out_shape=jax.ShapeDtypeStruct((m, n), F32),
        scratch_shapes=[pltpu.VMEM((tm, aw), BF16)],
        compiler_params=_params("parallel", "arbitrary"),
        name="out_proj",
    )(yf, a, ga.reshape(1, aw).astype(F32), w, x)


def _absorb_qk_kernel(wq_ref, k_ref, o_ref, *, scale):
    acc = lax.dot_general(wq_ref[...], k_ref[...], (((1,), (1,)), ((), ())), preferred_element_type=F32)
    o_ref[0] = (acc * scale).astype(o_ref.dtype)


def absorb_qk(wq, kv, *, batch, heads, mem, hd, scale):
    d = wq.shape[0]
    return pl.pallas_call(
        functools.partial(_absorb_qk_kernel, scale=scale),
        grid=(heads, batch),
        in_specs=[pl.BlockSpec((d, hd), lambda h, b: (0, h)),
                  pl.BlockSpec((mem, hd), lambda h, b: (b, h))],
        out_specs=pl.BlockSpec((1, d, mem), lambda h, b: (b, 0, h)),
        out_shape=jax.ShapeDtypeStruct((batch, d, heads * mem), BF16),
        compiler_params=_params("parallel", "parallel"),
        name="absorb_qk",
    )(wq, kv)


def _absorb_vo_kernel(v_ref, wo_ref, o_ref):
    o_ref[0] = jnp.dot(v_ref[...], wo_ref[...], preferred_element_type=F32).astype(o_ref.dtype)


def absorb_vo(kv, wo, *, batch, heads, mem, hd, tn):
    d = wo.shape[1]
    tn = _tile(d, tn)
    return pl.pallas_call(
        _absorb_vo_kernel,
        grid=(heads, d // tn, batch),
        in_specs=[pl.BlockSpec((mem, hd), lambda h, j, b: (b, h)),
                  pl.BlockSpec((hd, tn), lambda h, j, b: (h, j))],
        out_specs=pl.BlockSpec((1, mem, tn), lambda h, j, b: (b, h, j)),
        out_shape=jax.ShapeDtypeStruct((batch, heads * mem, d), BF16),
        compiler_params=_params("parallel", "parallel", "parallel"),
        name="absorb_vo",
    )(kv, wo)


def _xattn_kernel(hf_ref, hnext_ref, g_ref, wqk_ref, vo_ref, g2_ref, o_ref, on_ref, p_ref, na_ref, nb_ref,
                  *, heads, mem):
    i = pl.program_id(0)
    rows = hf_ref.shape[0]

    @pl.when(i == 0)
    def _():
        _rms_rows(hf_ref, g_ref, na_ref)

    def step(cur_ref, nxt_ref):
        for c in range(rows // NORM_CHUNK_ROWS):
            r = pl.ds(c * NORM_CHUNK_ROWS, NORM_CHUNK_ROWS)
            nxt_ref[r, :] = _rms(hnext_ref[r, :], g_ref[...]).astype(BF16)
        s = jnp.dot(cur_ref[...], wqk_ref[0], preferred_element_type=F32)
        for h in range(heads):
            seg = s[:, h * mem:(h + 1) * mem]
            e = jnp.exp(seg - jnp.max(seg, axis=-1, keepdims=True))
            p_ref[:, h * mem:(h + 1) * mem] = (e / jnp.sum(e, axis=-1, keepdims=True)).astype(BF16)
        o_ref[...] = jnp.dot(p_ref[...], vo_ref[0], preferred_element_type=F32) + hf_ref[...]
        _rms_rows(o_ref, g2_ref, on_ref)

    @pl.when(i % 2 == 0)
    def _():
        step(na_ref, nb_ref)

    @pl.when(i % 2 == 1)
    def _():
        step(nb_ref, na_ref)


def xattn(h, g, wqk, vo, g_next, *, seq, heads, mem, tm):
    m, d = h.shape
    tm = _tile(seq, tm)
    per_b = seq // tm
    hm = heads * mem
    once = pl.Buffered(1)
    nt = m // tm
    return pl.pallas_call(
        functools.partial(_xattn_kernel, heads=heads, mem=mem),
        grid=(nt,),
        in_specs=[pl.BlockSpec((tm, d), lambda i: (i, 0)),
                  pl.BlockSpec((tm, d), lambda i: (jnp.minimum(i + 1, nt - 1), 0)),
                  pl.BlockSpec((1, d), lambda i: (0, 0)),
                  pl.BlockSpec((1, d, hm), lambda i: (i // per_b, 0, 0), pipeline_mode=once),
                  pl.BlockSpec((1, hm, d), lambda i: (i // per_b, 0, 0), pipeline_mode=once),
                  pl.BlockSpec((1, d), lambda i: (0, 0))],
        out_specs=[pl.BlockSpec((tm, d), lambda i: (i, 0))] * 2,
        out_shape=[jax.ShapeDtypeStruct((m, d), F32), jax.ShapeDtypeStruct((m, d), BF16)],
        scratch_shapes=[pltpu.VMEM((tm, hm), BF16), pltpu.VMEM((tm, d), BF16), pltpu.VMEM((tm, d), BF16)],
        compiler_params=_params("arbitrary"),
        name="xattn",
    )(h, h, g.reshape(1, d).astype(F32), wqk, vo, g_next.reshape(1, d).astype(F32))


def _pad_cols(w, n):
    return jnp.pad(w, ((0, 0), (0, n - w.shape[1])))


def kernel(x, mem, positions, g_mix, w_in, w_fourier, g_q_lora, w_uq, g_kv_lora, w_ukv, g_fourier_out, g_mla_out, w_out, g_xattn, g_mem, w_xq, w_xk, w_xv, w_xo, g_mlp, w_ff1, w_ff2, g_final):
    batch, seq, d = x.shape
    depth = g_mix.shape[0]
    groups, gdim = w_fourier.shape[1], w_fourier.shape[2]
    fw = groups * gdim
    q_rank, heads, qk_dim = w_uq.shape[1], w_uq.shape[2], w_uq.shape[3]
    kv_rank = w_ukv.shape[1]
    nope = LANES
    rope = qk_dim - nope
    vdim = w_ukv.shape[3] - nope
    assert rope == LANES // 2 and vdim == LANES and fw % q_rank == 0
    assert (fw + q_rank) % kv_rank == 0 and (fw + q_rank + kv_rank) % LANES == 0
    xheads, xhd = w_xq.shape[2], w_xq.shape[3]
    mtok = mem.shape[1]
    t = batch * seq
    s2 = FOURIER_SEQ_INNER
    s1 = seq // s2

    half = rope // 2
    inv_freq = ROPE_THETA ** (-jnp.arange(half, dtype=F32) / half)
    ang = positions.astype(F32).reshape(t, 1) * inv_freq
    cos, sin = jnp.cos(ang), jnp.sin(ang)
    zq = jnp.zeros_like(cos)
    ct = jnp.concatenate([cos, cos, zq, zq], axis=1)
    sa = jnp.concatenate([zq, sin, zq, zq], axis=1)
    sb = jnp.concatenate([-sin, zq, zq, zq], axis=1)

    cs, tp, tq_tab, c2, s2m = _dft_tables(seq, gdim)

    in_width = w_in.shape[2]
    z_width = -(-(in_width + rope) // 768) * 768
    scale = float(qk_dim) ** -0.5 * math.log2(math.e)

    h = x.reshape(t, d)
    mem2 = mem.reshape(batch * mtok, d)
    for layer in range(depth):
        w_in_b = _pad_cols(w_in[layer], z_width).astype(BF16)
        wq = jnp.pad(w_uq[layer], ((0, 0), (0, 0), (0, 2 * LANES - qk_dim))).reshape(q_rank, heads * 2 * LANES).astype(BF16)
        wkv = jnp.concatenate([w_ukv[layer][:, :, :nope].reshape(kv_rank, heads * nope),
                               w_ukv[layer][:, :, nope:].reshape(kv_rank, heads * vdim)], axis=1).astype(BF16)
        wf = w_fourier[layer].astype(BF16)
        wxq = w_xq[layer].reshape(d, xheads * xhd).astype(BF16)
        wxk = w_xk[layer].reshape(d, xheads * xhd).astype(BF16)
        wxv = w_xv[layer].reshape(d, xheads * xhd).astype(BF16)

        z = norm_mm(h, g_mix[layer], w_in_b, tm=512, tn=768, out_dtype=BF16)
        p, q = chan_dft(z, cs, width=fw, groups=groups, tm=1024)
        pt = p.reshape(batch, s1, s2, fw).transpose(0, 2, 1, 3)
        qt = q.reshape(batch, s1, s2, fw).transpose(0, 2, 1, 3)
        a = seq_dft1(pt, qt, tp, tq_tab, nb=8)
        yf = seq_dft2(a.reshape(batch, 2 * s1, s2, fw), c2, s2m, wf, g_fourier_out[layer],
                      s1=s1, kb=4).reshape(t, fw)
        qh = q_proj(z, g_q_lora[layer], wq, ct, sa, sb, col_block=fw // q_rank, rank=q_rank,
                    heads=heads, scale=scale, tm=512)
        kh, vh = kv_proj(z, g_kv_lora[layer], wkv, ct, sa, sb, ckv_block=(fw + q_rank) // kv_rank,
                         kr_block=(fw + q_rank + kv_rank) // LANES, rank=kv_rank, heads=heads, tm=512)
        o, (w_out_b, wxo, w1, w2) = flash_attention(
            qh.reshape(batch, seq, -1), kh.reshape(batch, seq, -1), vh.reshape(batch, seq, -1),
            heads=heads, tq=1024, tk=512,
            cast_weights=(w_out[layer], w_xo[layer].reshape(xheads * xhd, d), w_ff1[layer], w_ff2[layer]))
        o = o.reshape(t, heads * vdim)
        h = out_proj(yf, o, g_mla_out[layer], w_out_b, h, tm=1024, tn=512)

        km = norm_mm(mem2, g_mem[layer], wxk, tm=512, tn=512, out_dtype=BF16)
        vm = norm_mm(mem2, g_mem[layer], wxv, tm=512, tn=512, out_dtype=BF16)
        wqk = absorb_qk(wxq, km, batch=batch, heads=xheads, mem=mtok, hd=xhd, scale=float(xhd) ** -0.5)
        vo = absorb_vo(vm, wxo, batch=batch, heads=xheads, mem=mtok, hd=xhd, tn=2048)
        h, hn = xattn(h, g_xattn[layer], wqk, vo, g_mlp[layer], seq=seq, heads=xheads, mem=mtok, tm=256)

        act = mm_relu2(hn, w1, tm=1024, tn=1024, out_dtype=BF16)
        h = mm_res(act, w2, h, tm=1024, tn=1024, tk=4096)

    return rmsnorm(h, g_final, tm=256, out_dtype=x.dtype).reshape(batch, seq, d)
```

```python
import functools
import math

import numpy as np
import jax
import jax.numpy as jnp
from jax import lax
from jax.experimental import pallas as pl
from jax.experimental.pallas import tpu as pltpu

F32 = jnp.float32
BF16 = jnp.bfloat16

NORM_EPS = 1e-6
ROPE_THETA = 10000.0
LANES = 128
VMEM_LIMIT_BYTES = 56 * 1024 * 1024
FOURIER_SEQ_INNER = 128


def _params(*sem):
    return pltpu.CompilerParams(dimension_semantics=sem, vmem_limit_bytes=VMEM_LIMIT_BYTES)


def _rms(xf, g):
    ms = jnp.mean(xf * xf, axis=-1, keepdims=True)
    return xf * lax.rsqrt(ms + NORM_EPS) * g


def _rms_rows(a_ref, g_ref, o_ref, chunk=64):
    rows = a_ref.shape[0]
    chunk = min(chunk, rows)

    def body(c, carry):
        r = pl.ds(pl.multiple_of(c * chunk, chunk), chunk)
        o_ref[r, :] = _rms(a_ref[r, :].astype(F32), g_ref[...]).astype(o_ref.dtype)
        return carry

    lax.fori_loop(0, rows // chunk, body, 0)


def _tile(dim, pref):
    t = min(dim, pref)
    assert dim % t == 0, (dim, pref)
    return t


NORM_CHUNK_ROWS = 32


def _norm_mm_kernel(a0_ref, an_ref, g_ref, w_ref, o_ref, xa_ref, xb_ref, *, slice_rows, n_slices):
    i, j = pl.program_id(0), pl.program_id(1)

    @pl.when((i == 0) & (j == 0))
    def _():
        _rms_rows(a0_ref, g_ref, xa_ref)

    def step(cur_ref, nxt_ref):
        start = jnp.minimum(j, n_slices - 1) * slice_rows
        for c in range(slice_rows // NORM_CHUNK_ROWS):
            r = pl.ds(pl.multiple_of(start + c * NORM_CHUNK_ROWS, NORM_CHUNK_ROWS), NORM_CHUNK_ROWS)
            nxt_ref[r, :] = _rms(an_ref[r, :].astype(F32), g_ref[...]).astype(BF16)
        o_ref[...] = jnp.dot(cur_ref[...], w_ref[...], preferred_element_type=F32).astype(o_ref.dtype)

    @pl.when(i % 2 == 0)
    def _():
        step(xa_ref, xb_ref)

    @pl.when(i % 2 == 1)
    def _():
        step(xb_ref, xa_ref)


def norm_mm(a, g, w, *, tm, tn, out_dtype):
    m, k = a.shape
    n = w.shape[1]
    tm, tn = _tile(m, tm), _tile(n, tn)
    ni, nj = m // tm, n // tn
    n_slices = 1
    while n_slices * 2 <= nj and (tm // (n_slices * 2)) % NORM_CHUNK_ROWS == 0:
        n_slices *= 2
    return pl.pallas_call(
        functools.partial(_norm_mm_kernel, slice_rows=tm // n_slices, n_slices=n_slices),
        grid=(ni, nj),
        in_specs=[pl.BlockSpec((tm, k), lambda i, j: (0, 0), pipeline_mode=pl.Buffered(1)),
                  pl.BlockSpec((tm, k), lambda i, j: (jnp.minimum(i + 1, ni - 1), 0)),
                  pl.BlockSpec((1, k), lambda i, j: (0, 0)),
                  pl.BlockSpec((k, tn), lambda i, j: (0, j))],
        out_specs=pl.BlockSpec((tm, tn), lambda i, j: (i, j)),
        out_shape=jax.ShapeDtypeStruct((m, n), out_dtype),
        scratch_shapes=[pltpu.VMEM((tm, k), BF16), pltpu.VMEM((tm, k), BF16)],
        compiler_params=_params("arbitrary", "arbitrary"),
        name="norm_mm",
    )(a, a, g.reshape(1, k).astype(F32), w)


def _mm_relu2_kernel(a_ref, w_ref, o_ref):
    acc = jnp.dot(a_ref[...], w_ref[...], preferred_element_type=F32)
    o_ref[...] = jnp.square(jnp.maximum(acc, 0.0)).astype(o_ref.dtype)


def mm_relu2(a, w, *, tm, tn, out_dtype):
    m, k = a.shape
    n = w.shape[1]
    tm, tn = _tile(m, tm), _tile(n, tn)
    return pl.pallas_call(
        _mm_relu2_kernel,
        grid=(m // tm, n // tn),
        in_specs=[pl.BlockSpec((tm, k), lambda i, j: (i, 0)),
                  pl.BlockSpec((k, tn), lambda i, j: (0, j))],
        out_specs=pl.BlockSpec((tm, tn), lambda i, j: (i, j)),
        out_shape=jax.ShapeDtypeStruct((m, n), out_dtype),
        compiler_params=_params("parallel", "parallel"),
        name="mm_relu2",
    )(a, w)


def _mm_res_kernel(a_ref, w_ref, res_ref, o_ref):
    @pl.when(pl.program_id(2) == 0)
    def _():
        o_ref[...] = res_ref[...]

    o_ref[...] += jnp.dot(a_ref[...], w_ref[...], preferred_element_type=F32)


def mm_res(a, w, res, *, tm, tn, tk):
    m, k = a.shape
    n = w.shape[1]
    tm, tn, tk = _tile(m, tm), _tile(n, tn), _tile(k, tk)
    return pl.pallas_call(
        _mm_res_kernel,
        grid=(m // tm, n // tn, k // tk),
        in_specs=[pl.BlockSpec((tm, tk), lambda i, j, kk: (i, kk)),
                  pl.BlockSpec((tk, tn), lambda i, j, kk: (kk, j)),
                  pl.BlockSpec((tm, tn), lambda i, j, kk: (i, j))],
        out_specs=pl.BlockSpec((tm, tn), lambda i, j, kk: (i, j)),
        out_shape=jax.ShapeDtypeStruct((m, n), F32),
        compiler_params=_params("parallel", "parallel", "arbitrary"),
        name="mm_res",
    )(a, w, res)


def _rmsnorm_kernel(a_ref, g_ref, o_ref):
    _rms_rows(a_ref, g_ref, o_ref)


def rmsnorm(a, g, *, tm, out_dtype):
    m, k = a.shape
    tm = _tile(m, tm)
    return pl.pallas_call(
        _rmsnorm_kernel,
        grid=(m // tm,),
        in_specs=[pl.BlockSpec((tm, k), lambda i: (i, 0)), pl.BlockSpec((1, k), lambda i: (0, 0))],
        out_specs=pl.BlockSpec((tm, k), lambda i: (i, 0)),
        out_shape=jax.ShapeDtypeStruct((m, k), out_dtype),
        compiler_params=_params("parallel"),
        name="rmsnorm",
    )(a, g.reshape(1, k).astype(F32))


def _fold_fourier_kernel(cs_ref, wf_ref, o_ref):
    o_ref[0] = jnp.dot(cs_ref[...], wf_ref[0], preferred_element_type=F32).astype(o_ref.dtype)


def fold_fourier_weights(cs_rows, wf):
    groups, gdim, _ = wf.shape
    return pl.pallas_call(
        _fold_fourier_kernel,
        grid=(groups,),
        in_specs=[pl.BlockSpec((2 * gdim, gdim), lambda g: (0, 0)),
                  pl.BlockSpec((1, gdim, gdim), lambda g: (g, 0, 0))],
        out_specs=pl.BlockSpec((1, 2 * gdim, gdim), lambda g: (g, 0, 0)),
        out_shape=jax.ShapeDtypeStruct((groups, 2 * gdim, gdim), BF16),
        compiler_params=_params("parallel"),
        name="fold_fourier_weights",
    )(cs_rows, wf)


def _seq_dft1_kernel(z_ref, t_ref, o_ref, *, nb, width):
    for t in range(nb):
        acc = jnp.dot(t_ref[t], z_ref[0, t], preferred_element_type=F32)
        o_ref[0, :, t * width:(t + 1) * width] = acc.astype(o_ref.dtype)


def seq_dft1(zt, tab, *, nb):
    b, s2, s1, width = zt.shape
    nb = _tile(s2, nb)
    return pl.pallas_call(
        functools.partial(_seq_dft1_kernel, nb=nb, width=width),
        grid=(b, s2 // nb),
        in_specs=[pl.BlockSpec((1, nb, s1, width), lambda bi, j: (bi, j, 0, 0)),
                  pl.BlockSpec((nb, 2 * s1, s1), lambda bi, j: (j, 0, 0))],
        out_specs=pl.BlockSpec((1, 2 * s1, nb * width), lambda bi, j: (bi, 0, j)),
        out_shape=jax.ShapeDtypeStruct((b, 2 * s1, s2 * width), BF16),
        compiler_params=_params("parallel", "parallel"),
        name="seq_dft1",
    )(zt, tab)


def _seq_dft2_kernel(re_ref, im_ref, m2_ref, wcs_ref, gf_ref, o_ref, *, kb, width):
    groups, _, gdim = wcs_ref.shape
    s2 = re_ref.shape[2]
    for t in range(kb):
        a = jnp.concatenate([re_ref[0, t], im_ref[0, t]], axis=0)
        y = jnp.dot(m2_ref[...], a, preferred_element_type=F32).astype(BF16)
        y_re, y_im = y[:s2], y[s2:]
        parts = []
        for g in range(groups):
            sl = slice(g * gdim, (g + 1) * gdim)
            lhs = jnp.concatenate([y_re[:, sl], y_im[:, sl]], axis=1)
            parts.append(jnp.dot(lhs, wcs_ref[g], preferred_element_type=F32))
        ms = sum(jnp.sum(p * p, axis=-1, keepdims=True) for p in parts) / width
        inv = lax.rsqrt(ms + NORM_EPS)
        for g in range(groups):
            sl = slice(g * gdim, (g + 1) * gdim)
            o_ref[0, :, t * width + g * gdim:t * width + (g + 1) * gdim] = (
                parts[g] * inv * gf_ref[:, sl]).astype(o_ref.dtype)


def seq_dft2(a4, m2, wcs, gf, *, s1, kb):
    b, _, s2, width = a4.shape
    kb = _tile(s1, kb)
    nblk = s1 // kb
    return pl.pallas_call(
        functools.partial(_seq_dft2_kernel, kb=kb, width=width),
        grid=(b, nblk),
        in_specs=[
            pl.BlockSpec((1, kb, s2, width), lambda bi, j: (bi, j, 0, 0)),
            pl.BlockSpec((1, kb, s2, width), lambda bi, j: (bi, j + nblk, 0, 0)),
            pl.BlockSpec((2 * s2, 2 * s2), lambda bi, j: (0, 0)),
            pl.BlockSpec(wcs.shape, lambda bi, j: (0, 0, 0)),
            pl.BlockSpec((1, width), lambda bi, j: (0, 0)),
        ],
        out_specs=pl.BlockSpec((1, s2, kb * width), lambda bi, j: (bi, 0, j)),
        out_shape=jax.ShapeDtypeStruct((b, s2, s1 * width), BF16),
        compiler_params=_params("parallel", "parallel"),
        name="seq_dft2",
    )(a4, a4, m2, wcs, gf.reshape(1, width).astype(F32))


def _dft_tables(seq, gdim):
    s2 = FOURIER_SEQ_INNER
    s1 = seq // s2
    c = np.arange(gdim)
    ang = 2.0 * np.pi * ((c[:, None] * c[None, :]) % gdim) / gdim
    cs_rows = np.concatenate([np.cos(ang), np.sin(ang)], axis=0) / math.sqrt(gdim)
    k1 = np.arange(s1)
    n1 = np.arange(s1)
    n2 = np.arange(s2)
    idx = (k1[None, :, None] * (s2 * n1[None, None, :] + n2[:, None, None])) % seq
    phi = 2.0 * np.pi * idx / seq
    tab1 = np.concatenate([np.cos(phi), -np.sin(phi)], axis=1)
    k2 = np.arange(s2)
    ang2 = 2.0 * np.pi * ((k2[:, None] * n2[None, :]) % s2) / s2
    c2 = np.cos(ang2) / math.sqrt(seq)
    s2m = np.sin(ang2) / math.sqrt(seq)
    m2 = np.block([[c2, s2m], [-s2m, c2]])
    as_bf16 = lambda x: jnp.asarray(x, dtype=F32).astype(BF16)
    return as_bf16(cs_rows), as_bf16(tab1), as_bf16(m2)


def _rope128(r, ct, sa, sb):
    quarter = LANES // 4
    return r * ct + pltpu.roll(r, quarter, 1) * sa + pltpu.roll(r, LANES - quarter, 1) * sb


def _q_proj_kernel(c_ref, g_ref, w_ref, ct_ref, sa_ref, sb_ref, o_ref, *, heads, scale):
    xn = _rms(c_ref[...].astype(F32), g_ref[...]).astype(BF16)
    acc = jnp.dot(xn, w_ref[...], preferred_element_type=F32)
    ct, sa, sb = ct_ref[...], sa_ref[...], sb_ref[...]
    for h in range(heads):
        base = 2 * LANES * h
        o_ref[:, base:base + LANES] = (acc[:, base:base + LANES] * scale).astype(o_ref.dtype)
        r = _rope128(acc[:, base + LANES:base + 2 * LANES], ct, sa, sb)
        o_ref[:, base + LANES:base + 2 * LANES] = (r * scale).astype(o_ref.dtype)


def q_proj(z, g, w, ct, sa, sb, *, col_block, rank, heads, scale, tm):
    m = z.shape[0]
    tm = _tile(m, tm)
    n = w.shape[1]
    tab = pl.BlockSpec((tm, LANES), lambda i: (i, 0))
    return pl.pallas_call(
        functools.partial(_q_proj_kernel, heads=heads, scale=scale),
        grid=(m // tm,),
        in_specs=[pl.BlockSpec((tm, rank), lambda i: (i, col_block)),
                  pl.BlockSpec((1, rank), lambda i: (0, 0)),
                  pl.BlockSpec((rank, n), lambda i: (0, 0)),
                  tab, tab, tab],
        out_specs=pl.BlockSpec((tm, n), lambda i: (i, 0)),
        out_shape=jax.ShapeDtypeStruct((m, n), BF16),
        compiler_params=_params("parallel"),
        name="q_proj",
    )(z, g.reshape(1, rank).astype(F32), w, ct, sa, sb)


def _kv_proj_kernel(c_ref, kr_ref, g_ref, w_ref, ct_ref, sa_ref, sb_ref, k_ref, v_ref, *, heads):
    xn = _rms(c_ref[...].astype(F32), g_ref[...]).astype(BF16)
    acc = jnp.dot(xn, w_ref[...], preferred_element_type=F32)
    kr = _rope128(kr_ref[...].astype(F32), ct_ref[...], sa_ref[...], sb_ref[...]).astype(k_ref.dtype)
    for h in range(heads):
        k_ref[:, 2 * LANES * h:2 * LANES * h + LANES] = acc[:, LANES * h:LANES * (h + 1)].astype(k_ref.dtype)
        k_ref[:, 2 * LANES * h + LANES:2 * LANES * (h + 1)] = kr
        v_ref[:, 2 * LANES * h:2 * LANES * h + LANES] = acc[:, LANES * (heads + h):LANES * (heads + h + 1)].astype(v_ref.dtype)
        v_ref[:, 2 * LANES * h + LANES:2 * LANES * (h + 1)] = jnp.ones((acc.shape[0], LANES), v_ref.dtype)


def kv_proj(z, g, w, ct, sa, sb, *, ckv_block, kr_block, rank, heads, tm):
    m = z.shape[0]
    tm = _tile(m, tm)
    tab = pl.BlockSpec((tm, LANES), lambda i: (i, 0))
    return pl.pallas_call(
        functools.partial(_kv_proj_kernel, heads=heads),
        grid=(m // tm,),
        in_specs=[pl.BlockSpec((tm, rank), lambda i: (i, ckv_block)),
                  pl.BlockSpec((tm, LANES), lambda i: (i, kr_block)),
                  pl.BlockSpec((1, rank), lambda i: (0, 0)),
                  pl.BlockSpec((rank, 2 * LANES * heads), lambda i: (0, 0)),
                  tab, tab, tab],
        out_specs=[pl.BlockSpec((tm, 2 * LANES * heads), lambda i: (i, 0))] * 2,
        out_shape=[jax.ShapeDtypeStruct((m, 2 * LANES * heads), BF16)] * 2,
        compiler_params=_params("parallel"),
        name="kv_proj",
    )(z, z, g.reshape(1, rank).astype(F32), w, ct, sa, sb)


def _flash_kernel(q_ref, k_ref, v_ref, *rest, tk, nk, ncast):
    cast_in, o_ref, cast_out = rest[:ncast], rest[ncast], rest[ncast + 1:2 * ncast + 1]
    s0_ref, s1_ref = rest[2 * ncast + 1:]
    for src, dst in zip(cast_in, cast_out):
        dst[...] = src[...].astype(dst.dtype)
    q = q_ref[0]
    tq = q.shape[0]
    dv = o_ref.shape[-1]

    def chunk(c):
        start = c * tk
        return pl.ds(start if isinstance(start, int) else pl.multiple_of(start, tk), tk)

    def scores(c):
        return lax.dot_general(q, k_ref[0, chunk(c), :], (((1,), (1,)), ((), ())), preferred_element_type=F32)

    def fold(s_ref, c, m, acc):
        s = s_ref[...]
        m_new = jnp.maximum(m, jnp.max(s, axis=1, keepdims=True))
        p = jnp.exp2(s - m_new).astype(BF16)
        acc = jnp.exp2(m - m_new) * acc + jnp.dot(p, v_ref[0, chunk(c), :], preferred_element_type=F32)
        return m_new, acc

    def pair(j, carry, last):
        m, acc = carry
        s1_ref[...] = scores(2 * j + 1)
        m, acc = fold(s0_ref, 2 * j, m, acc)
        if not last:
            s0_ref[...] = scores(2 * j + 2)
        return fold(s1_ref, 2 * j + 1, m, acc)

    s0_ref[...] = scores(0)
    init = (jnp.full((tq, 1), -jnp.inf, F32), jnp.zeros((tq, 2 * dv), F32))
    carry = init
    for j in range(nk // 2):
        carry = pair(j, carry, j == nk // 2 - 1)
    _, acc = carry
    o_ref[0] = (acc[:, :dv] / acc[:, dv:]).astype(o_ref.dtype)


def flash_attention(q, k, v, *, heads, tq, tk, cast_weights=()):
    b, s, _ = q.shape
    dk = q.shape[-1] // heads
    dv2 = v.shape[-1] // heads
    dv = dv2 // 2
    tq, tk = _tile(s, tq), _tile(s, min(tk, s // 2))
    nk = s // tk
    assert nk % 2 == 0
    nq = s // tq
    steps = b * heads * nq
    wspecs = []
    for w in cast_weights:
        rows = w.shape[0] // steps
        assert rows * steps == w.shape[0] and rows % 16 == 0, (w.shape, steps)
        wspecs.append(pl.BlockSpec((rows, w.shape[1]), lambda bi, h, i: ((bi * heads + h) * nq + i, 0)))
    wspecs_in = wspecs_out = wspecs
    wshapes = [jax.ShapeDtypeStruct(w.shape, BF16) for w in cast_weights]
    outs = pl.pallas_call(
        functools.partial(_flash_kernel, tk=tk, nk=nk, ncast=len(cast_weights)),
        grid=(b, heads, nq),
        in_specs=[pl.BlockSpec((1, tq, dk), lambda bi, h, i: (bi, i, h)),
                  pl.BlockSpec((1, s, dk), lambda bi, h, i: (bi, 0, h)),
                  pl.BlockSpec((1, s, dv2), lambda bi, h, i: (bi, 0, h))] + wspecs_in,
        out_specs=[pl.BlockSpec((1, tq, dv), lambda bi, h, i: (bi, i, h))] + wspecs_out,
        out_shape=[jax.ShapeDtypeStruct((b, s, heads * dv), BF16)] + wshapes,
        scratch_shapes=[pltpu.VMEM((tq, tk), F32), pltpu.VMEM((tq, tk), F32)],
        compiler_params=_params("parallel", "parallel", "arbitrary"),
        name="flash_attention",
    )(q, k, v, *cast_weights)
    return outs[0], outs[1:]


def _out_proj_kernel(yf_ref, a_ref, ga_ref, w_ref, x_ref, o_ref, ya_ref):
    fw = yf_ref.shape[1]

    @pl.when(pl.program_id(1) == 0)
    def _():
        _rms_rows(a_ref, ga_ref, ya_ref)

    acc = jnp.dot(yf_ref[...], w_ref[:fw, :], preferred_element_type=F32)
    acc += jnp.dot(ya_ref[...], w_ref[fw:, :], preferred_element_type=F32)
    o_ref[...] = (acc + x_ref[...]).astype(o_ref.dtype)


def out_proj(yf, a, ga, w, x, *, tm, tn):
    m, fw = yf.shape
    aw = a.shape[1]
    n = w.shape[1]
    tm, tn = _tile(m, tm), _tile(n, tn)
    return pl.pallas_call(
        _out_proj_kernel,
        grid=(m // tm, n // tn),
        in_specs=[pl.BlockSpec((tm, fw), lambda i, j: (i, 0)),
                  pl.BlockSpec((tm, aw), lambda i, j: (i, 0)),
                  pl.BlockSpec((1, aw), lambda i, j: (0, 0)),
                  pl.BlockSpec((fw + aw, tn), lambda i, j: (0, j)),
                  pl.BlockSpec((tm, tn), lambda i, j: (i, j))],
        out_specs=pl.BlockSpec((tm, tn), lambda i, j: (i, j)),
        out_shape=jax.ShapeDtypeStruct((m, n), F32),
        scratch_shapes=[pltpu.VMEM((tm, aw), BF16)],
        compiler_params=_params("parallel", "arbitrary"),
        name="out_proj",
    )(yf, a, ga.reshape(1, aw).astype(F32), w, x)


def _absorb_qk_kernel(wq_ref, k_ref, o_ref, *, scale):
    acc = lax.dot_general(wq_ref[...], k_ref[...], (((1,), (1,)), ((), ())), preferred_element_type=F32)
    o_ref[0] = (acc * scale).astype(o_ref.dtype)


def absorb_qk(wq, kv, *, batch, heads, mem, hd, scale):
    d = wq.shape[0]
    return pl.pallas_call(
        functools.partial(_absorb_qk_kernel, scale=scale),
        grid=(heads, batch),
        in_specs=[pl.BlockSpec((d, hd), lambda h, b: (0, h)),
                  pl.BlockSpec((mem, hd), lambda h, b: (b, h))],
        out_specs=pl.BlockSpec((1, d, mem), lambda h, b: (b, 0, h)),
        out_shape=jax.ShapeDtypeStruct((batch, d, heads * mem), BF16),
        compiler_params=_params("parallel", "parallel"),
        name="absorb_qk",
    )(wq, kv)


def _absorb_vo_kernel(v_ref, wo_ref, o_ref):
    o_ref[0] = jnp.dot(v_ref[...], wo_ref[...], preferred_element_type=F32).astype(o_ref.dtype)


def absorb_vo(kv, wo, *, batch, heads, mem, hd, tn):
    d = wo.shape[1]
    tn = _tile(d, tn)
    return pl.pallas_call(
        _absorb_vo_kernel,
        grid=(heads, d // tn, batch),
        in_specs=[pl.BlockSpec((mem, hd), lambda h, j, b: (b, h)),
                  pl.BlockSpec((hd, tn), lambda h, j, b: (h, j))],
        out_specs=pl.BlockSpec((1, mem, tn), lambda h, j, b: (b, h, j)),
        out_shape=jax.ShapeDtypeStruct((batch, heads * mem, d), BF16),
        compiler_params=_params("parallel", "parallel", "parallel"),
        name="absorb_vo",
    )(kv, wo)


def _xattn_kernel(hf_ref, hnext_ref, g_ref, wqk_ref, vo_ref, g2_ref, o_ref, on_ref, p_ref, na_ref, nb_ref,
                  *, heads, mem):
    i = pl.program_id(0)
    rows = hf_ref.shape[0]

    @pl.when(i == 0)
    def _():
        _rms_rows(hf_ref, g_ref, na_ref)

    def step(cur_ref, nxt_ref):
        for c in range(rows // NORM_CHUNK_ROWS):
            r = pl.ds(c * NORM_CHUNK_ROWS, NORM_CHUNK_ROWS)
            nxt_ref[r, :] = _rms(hnext_ref[r, :], g_ref[...]).astype(BF16)
        s = jnp.dot(cur_ref[...], wqk_ref[0], preferred_element_type=F32)
        for h in range(heads):
            seg = s[:, h * mem:(h + 1) * mem]
            e = jnp.exp(seg - jnp.max(seg, axis=-1, keepdims=True))
            p_ref[:, h * mem:(h + 1) * mem] = (e / jnp.sum(e, axis=-1, keepdims=True)).astype(BF16)
        o_ref[...] = jnp.dot(p_ref[...], vo_ref[0], preferred_element_type=F32) + hf_ref[...]
        _rms_rows(o_ref, g2_ref, on_ref)

    @pl.when(i % 2 == 0)
    def _():
        step(na_ref, nb_ref)

    @pl.when(i % 2 == 1)
    def _():
        step(nb_ref, na_ref)


def xattn(h, g, wqk, vo, g_next, *, seq, heads, mem, tm):
    m, d = h.shape
    tm = _tile(seq, tm)
    per_b = seq // tm
    hm = heads * mem
    once = pl.Buffered(1)
    nt = m // tm
    return pl.pallas_call(
        functools.partial(_xattn_kernel, heads=heads, mem=mem),
        grid=(nt,),
        in_specs=[pl.BlockSpec((tm, d), lambda i: (i, 0)),
                  pl.BlockSpec((tm, d), lambda i: (jnp.minimum(i + 1, nt - 1), 0)),
                  pl.BlockSpec((1, d), lambda i: (0, 0)),
                  pl.BlockSpec((1, d, hm), lambda i: (i // per_b, 0, 0), pipeline_mode=once),
                  pl.BlockSpec((1, hm, d), lambda i: (i // per_b, 0, 0), pipeline_mode=once),
                  pl.BlockSpec((1, d), lambda i: (0, 0))],
        out_specs=[pl.BlockSpec((tm, d), lambda i: (i, 0))] * 2,
        out_shape=[jax.ShapeDtypeStruct((m, d), F32), jax.ShapeDtypeStruct((m, d), BF16)],
        scratch_shapes=[pltpu.VMEM((tm, hm), BF16), pltpu.VMEM((tm, d), BF16), pltpu.VMEM((tm, d), BF16)],
        compiler_params=_params("arbitrary"),
        name="xattn",
    )(h, h, g.reshape(1, d).astype(F32), wqk, vo, g_next.reshape(1, d).astype(F32))


def _pad_cols(w, n):
    return jnp.pad(w, ((0, 0), (0, n - w.shape[1])))


def kernel(x, mem, positions, g_mix, w_in, w_fourier, g_q_lora, w_uq, g_kv_lora, w_ukv, g_fourier_out, g_mla_out, w_out, g_xattn, g_mem, w_xq, w_xk, w_xv, w_xo, g_mlp, w_ff1, w_ff2, g_final):
    batch, seq, d = x.shape
    depth = g_mix.shape[0]
    groups, gdim = w_fourier.shape[1], w_fourier.shape[2]
    fw = groups * gdim
    q_rank, heads, qk_dim = w_uq.shape[1], w_uq.shape[2], w_uq.shape[3]
    kv_rank = w_ukv.shape[1]
    nope = LANES
    rope = qk_dim - nope
    vdim = w_ukv.shape[3] - nope
    assert rope == LANES // 2 and vdim == LANES and fw % q_rank == 0
    assert (fw + q_rank) % kv_rank == 0 and (fw + q_rank + kv_rank) % LANES == 0
    xheads, xhd = w_xq.shape[2], w_xq.shape[3]
    mtok = mem.shape[1]
    t = batch * seq
    s2 = FOURIER_SEQ_INNER
    s1 = seq // s2

    half = rope // 2
    inv_freq = ROPE_THETA ** (-jnp.arange(half, dtype=F32) / half)
    ang = positions.astype(F32).reshape(t, 1) * inv_freq
    cos, sin = jnp.cos(ang), jnp.sin(ang)
    zq = jnp.zeros_like(cos)
    ct = jnp.concatenate([cos, cos, zq, zq], axis=1)
    sa = jnp.concatenate([zq, sin, zq, zq], axis=1)
    sb = jnp.concatenate([-sin, zq, zq, zq], axis=1)

    cs_rows, tab1, m2 = _dft_tables(seq, gdim)

    in_width = w_in.shape[2]
    z_width = -(-(in_width + rope) // 768) * 768
    scale = float(qk_dim) ** -0.5 * math.log2(math.e)

    h = x.reshape(t, d)
    mem2 = mem.reshape(batch * mtok, d)
    for layer in range(depth):
        w_in_b = _pad_cols(w_in[layer], z_width).astype(BF16)
        wq = jnp.pad(w_uq[layer], ((0, 0), (0, 0), (0, 2 * LANES - qk_dim))).reshape(q_rank, heads * 2 * LANES).astype(BF16)
        wkv = jnp.concatenate([w_ukv[layer][:, :, :nope].reshape(kv_rank, heads * nope),
                               w_ukv[layer][:, :, nope:].reshape(kv_rank, heads * vdim)], axis=1).astype(BF16)
        wcs = fold_fourier_weights(cs_rows, w_fourier[layer].astype(BF16))
        wxq = w_xq[layer].reshape(d, xheads * xhd).astype(BF16)
        wxk = w_xk[layer].reshape(d, xheads * xhd).astype(BF16)
        wxv = w_xv[layer].reshape(d, xheads * xhd).astype(BF16)

        z = norm_mm(h, g_mix[layer], w_in_b, tm=512, tn=768, out_dtype=BF16)
        zt = z[:, :fw].reshape(batch, s1, s2, fw).transpose(0, 2, 1, 3)
        a = seq_dft1(zt, tab1, nb=8)
        yf = seq_dft2(a.reshape(batch, 2 * s1, s2, fw), m2, wcs, g_fourier_out[layer],
                      s1=s1, kb=4).reshape(t, fw)
        qh = q_proj(z, g_q_lora[layer], wq, ct, sa, sb, col_block=fw // q_rank, rank=q_rank,
                    heads=heads, scale=scale, tm=512)
        kh, vh = kv_proj(z, g_kv_lora[layer], wkv, ct, sa, sb, ckv_block=(fw + q_rank) // kv_rank,
                         kr_block=(fw + q_rank + kv_rank) // LANES, rank=kv_rank, heads=heads, tm=512)
        o, (w_out_b, wxo, w1, w2) = flash_attention(
            qh.reshape(batch, seq, -1), kh.reshape(batch, seq, -1), vh.reshape(batch, seq, -1),
            heads=heads, tq=1024, tk=512,
            cast_weights=(w_out[layer], w_xo[layer].reshape(xheads * xhd, d), w_ff1[layer], w_ff2[layer]))
        o = o.reshape(t, heads * vdim)
        h = out_proj(yf, o, g_mla_out[layer], w_out_b, h, tm=1024, tn=512)

        km = norm_mm(mem2, g_mem[layer], wxk, tm=512, tn=512, out_dtype=BF16)
        vm = norm_mm(mem2, g_mem[layer], wxv, tm=512, tn=512, out_dtype=BF16)
        wqk = absorb_qk(wxq, km, batch=batch, heads=xheads, mem=mtok, hd=xhd, scale=float(xhd) ** -0.5)
        vo = absorb_vo(vm, wxo, batch=batch, heads=xheads, mem=mtok, hd=xhd, tn=2048)
        h, hn = xattn(h, g_xattn[layer], wqk, vo, g_mlp[layer], seq=seq, heads=xheads, mem=mtok, tm=256)

        act = mm_relu2(hn, w1, tm=1024, tn=1024, out_dtype=BF16)
        h = mm_res(act, w2, h, tm=1024, tn=1024, tk=4096)

    return rmsnorm(h, g_final, tm=256, out_dtype=x.dtype).reshape(batch, seq, d)
```

```python
import functools
import math

import numpy as np
import jax
import jax.numpy as jnp
from jax import lax
from jax.experimental import pallas as pl
from jax.experimental.pallas import tpu as pltpu

F32 = jnp.float32
BF16 = jnp.bfloat16

NORM_EPS = 1e-6
ROPE_THETA = 10000.0
LANES = 128
VMEM_LIMIT_BYTES = 56 * 1024 * 1024
FOURIER_SEQ_INNER = 128


def _params(*sem):
    return pltpu.CompilerParams(dimension_semantics=sem, vmem_limit_bytes=VMEM_LIMIT_BYTES)


def _rms(xf, g):
    ms = jnp.mean(xf * xf, axis=-1, keepdims=True)
    return xf * lax.rsqrt(ms + NORM_EPS) * g


def _rms_rows(a_ref, g_ref, o_ref, chunk=64):
    rows = a_ref.shape[0]
    chunk = min(chunk, rows)

    def body(c, carry):
        r = pl.ds(pl.multiple_of(c * chunk, chunk), chunk)
        o_ref[r, :] = _rms(a_ref[r, :].astype(F32), g_ref[...]).astype(o_ref.dtype)
        return carry

    lax.fori_loop(0, rows // chunk, body, 0)


def _tile(dim, pref):
    t = min(dim, pref)
    assert dim % t == 0, (dim, pref)
    return t


NORM_CHUNK_ROWS = 32


def _norm_mm_kernel(a0_ref, an_ref, g_ref, w_ref, o_ref, xa_ref, xb_ref, *, slice_rows, n_slices):
    i, j = pl.program_id(0), pl.program_id(1)

    @pl.when((i == 0) & (j == 0))
    def _():
        _rms_rows(a0_ref, g_ref, xa_ref)

    def step(cur_ref, nxt_ref):
        start = jnp.minimum(j, n_slices - 1) * slice_rows
        for c in range(slice_rows // NORM_CHUNK_ROWS):
            r = pl.ds(pl.multiple_of(start + c * NORM_CHUNK_ROWS, NORM_CHUNK_ROWS), NORM_CHUNK_ROWS)
            nxt_ref[r, :] = _rms(an_ref[r, :].astype(F32), g_ref[...]).astype(BF16)
        o_ref[...] = jnp.dot(cur_ref[...], w_ref[...], preferred_element_type=F32).astype(o_ref.dtype)

    @pl.when(i % 2 == 0)
    def _():
        step(xa_ref, xb_ref)

    @pl.when(i % 2 == 1)
    def _():
        step(xb_ref, xa_ref)


def norm_mm(a, g, w, *, tm, tn, out_dtype):
    m, k = a.shape
    n = w.shape[1]
    tm, tn = _tile(m, tm), _tile(n, tn)
    ni, nj = m // tm, n // tn
    n_slices = 1
    while n_slices * 2 <= nj and (tm // (n_slices * 2)) % NORM_CHUNK_ROWS == 0:
        n_slices *= 2
    return pl.pallas_call(
        functools.partial(_norm_mm_kernel, slice_rows=tm // n_slices, n_slices=n_slices),
        grid=(ni, nj),
        in_specs=[pl.BlockSpec((tm, k), lambda i, j: (0, 0), pipeline_mode=pl.Buffered(1)),
                  pl.BlockSpec((tm, k), lambda i, j: (jnp.minimum(i + 1, ni - 1), 0)),
                  pl.BlockSpec((1, k), lambda i, j: (0, 0)),
                  pl.BlockSpec((k, tn), lambda i, j: (0, j))],
        out_specs=pl.BlockSpec((tm, tn), lambda i, j: (i, j)),
        out_shape=jax.ShapeDtypeStruct((m, n), out_dtype),
        scratch_shapes=[pltpu.VMEM((tm, k), BF16), pltpu.VMEM((tm, k), BF16)],
        compiler_params=_params("arbitrary", "arbitrary"),
        name="norm_mm",
    )(a, a, g.reshape(1, k).astype(F32), w)


def _mm_relu2_kernel(a_ref, w_ref, o_ref):
    acc = jnp.dot(a_ref[...], w_ref[...], preferred_element_type=F32)
    o_ref[...] = jnp.square(jnp.maximum(acc, 0.0)).astype(o_ref.dtype)


def mm_relu2(a, w, *, tm, tn, out_dtype):
    m, k = a.shape
    n = w.shape[1]
    tm, tn = _tile(m, tm), _tile(n, tn)
    return pl.pallas_call(
        _mm_relu2_kernel,
        grid=(m // tm, n // tn),
        in_specs=[pl.BlockSpec((tm, k), lambda i, j: (i, 0)),
                  pl.BlockSpec((k, tn), lambda i, j: (0, j))],
        out_specs=pl.BlockSpec((tm, tn), lambda i, j: (i, j)),
        out_shape=jax.ShapeDtypeStruct((m, n), out_dtype),
        compiler_params=_params("parallel", "parallel"),
        name="mm_relu2",
    )(a, w)


def _mm_res_kernel(a_ref, w_ref, res_ref, o_ref):
    @pl.when(pl.program_id(2) == 0)
    def _():
        o_ref[...] = res_ref[...]

    o_ref[...] += jnp.dot(a_ref[...], w_ref[...], preferred_element_type=F32)


def mm_res(a, w, res, *, tm, tn, tk):
    m, k = a.shape
    n = w.shape[1]
    tm, tn, tk = _tile(m, tm), _tile(n, tn), _tile(k, tk)
    return pl.pallas_call(
        _mm_res_kernel,
        grid=(m // tm, n // tn, k // tk),
        in_specs=[pl.BlockSpec((tm, tk), lambda i, j, kk: (i, kk)),
                  pl.BlockSpec((tk, tn), lambda i, j, kk: (kk, j)),
                  pl.BlockSpec((tm, tn), lambda i, j, kk: (i, j))],
        out_specs=pl.BlockSpec((tm, tn), lambda i, j, kk: (i, j)),
        out_shape=jax.ShapeDtypeStruct((m, n), F32),
        compiler_params=_params("parallel", "parallel", "arbitrary"),
        name="mm_res",
    )(a, w, res)


def _rmsnorm_kernel(a_ref, g_ref, o_ref):
    _rms_rows(a_ref, g_ref, o_ref)


def rmsnorm(a, g, *, tm, out_dtype):
    m, k = a.shape
    tm = _tile(m, tm)
    return pl.pallas_call(
        _rmsnorm_kernel,
        grid=(m // tm,),
        in_specs=[pl.BlockSpec((tm, k), lambda i: (i, 0)), pl.BlockSpec((1, k), lambda i: (0, 0))],
        out_specs=pl.BlockSpec((tm, k), lambda i: (i, 0)),
        out_shape=jax.ShapeDtypeStruct((m, k), out_dtype),
        compiler_params=_params("parallel"),
        name="rmsnorm",
    )(a, g.reshape(1, k).astype(F32))


def _fold_fourier_kernel(cs_ref, wf_ref, o_ref):
    o_ref[0] = jnp.dot(cs_ref[...], wf_ref[0], preferred_element_type=F32).astype(o_ref.dtype)


def fold_fourier_weights(cs_rows, wf):
    groups, gdim, _ = wf.shape
    return pl.pallas_call(
        _fold_fourier_kernel,
        grid=(groups,),
        in_specs=[pl.BlockSpec((2 * gdim, gdim), lambda g: (0, 0)),
                  pl.BlockSpec((1, gdim, gdim), lambda g: (g, 0, 0))],
        out_specs=pl.BlockSpec((1, 2 * gdim, gdim), lambda g: (g, 0, 0)),
        out_shape=jax.ShapeDtypeStruct((groups, 2 * gdim, gdim), BF16),
        compiler_params=_params("parallel"),
        name="fold_fourier_weights",
    )(cs_rows, wf)


def _seq_dft1_kernel(z_ref, t_ref, o_ref, *, nb, width):
    for t in range(nb):
        acc = jnp.dot(t_ref[t], z_ref[0, t], preferred_element_type=F32)
        o_ref[0, :, t * width:(t + 1) * width] = acc.astype(o_ref.dtype)


def seq_dft1(zt, tab, *, nb):
    b, s2, s1, width = zt.shape
    nb = _tile(s2, nb)
    return pl.pallas_call(
        functools.partial(_seq_dft1_kernel, nb=nb, width=width),
        grid=(b, s2 // nb),
        in_specs=[pl.BlockSpec((1, nb, s1, width), lambda bi, j: (bi, j, 0, 0)),
                  pl.BlockSpec((nb, 2 * s1, s1), lambda bi, j: (j, 0, 0))],
        out_specs=pl.BlockSpec((1, 2 * s1, nb * width), lambda bi, j: (bi, 0, j)),
        out_shape=jax.ShapeDtypeStruct((b, 2 * s1, s2 * width), BF16),
        compiler_params=_params("parallel", "parallel"),
        name="seq_dft1",
    )(zt, tab)


def _seq_dft2_kernel(re_ref, im_ref, m2_ref, wcs_ref, gf_ref, o_ref, *, kb, width):
    groups, _, gdim = wcs_ref.shape
    s2 = re_ref.shape[2]
    for t in range(kb):
        a = jnp.concatenate([re_ref[0, t], im_ref[0, t]], axis=0)
        y = jnp.dot(m2_ref[...], a, preferred_element_type=F32).astype(BF16)
        y_re, y_im = y[:s2], y[s2:]
        parts = []
        for g in range(groups):
            sl = slice(g * gdim, (g + 1) * gdim)
            lhs = jnp.concatenate([y_re[:, sl], y_im[:, sl]], axis=1)
            parts.append(jnp.dot(lhs, wcs_ref[g], preferred_element_type=F32))
        ms = sum(jnp.sum(p * p, axis=-1, keepdims=True) for p in parts) / width
        inv = lax.rsqrt(ms + NORM_EPS)
        for g in range(groups):
            sl = slice(g * gdim, (g + 1) * gdim)
            o_ref[0, :, t * width + g * gdim:t * width + (g + 1) * gdim] = (
                parts[g] * inv * gf_ref[:, sl]).astype(o_ref.dtype)


def seq_dft2(a4, m2, wcs, gf, *, s1, kb):
    b, _, s2, width = a4.shape
    kb = _tile(s1, kb)
    nblk = s1 // kb
    return pl.pallas_call(
        functools.partial(_seq_dft2_kernel, kb=kb, width=width),
        grid=(b, nblk),
        in_specs=[
            pl.BlockSpec((1, kb, s2, width), lambda bi, j: (bi, j, 0, 0)),
            pl.BlockSpec((1, kb, s2, width), lambda bi, j: (bi, j + nblk, 0, 0)),
            pl.BlockSpec((2 * s2, 2 * s2), lambda bi, j: (0, 0)),
            pl.BlockSpec(wcs.shape, lambda bi, j: (0, 0, 0)),
            pl.BlockSpec((1, width), lambda bi, j: (0, 0)),
        ],
        out_specs=pl.BlockSpec((1, s2, kb * width), lambda bi, j: (bi, 0, j)),
        out_shape=jax.ShapeDtypeStruct((b, s2, s1 * width), BF16),
        compiler_params=_params("parallel", "parallel"),
        name="seq_dft2",
    )(a4, a4, m2, wcs, gf.reshape(1, width).astype(F32))


def _dft_tables(seq, gdim):
    s2 = FOURIER_SEQ_INNER
    s1 = seq // s2
    c = np.arange(gdim)
    ang = 2.0 * np.pi * ((c[:, None] * c[None, :]) % gdim) / gdim
    cs_rows = np.concatenate([np.cos(ang), np.sin(ang)], axis=0) / math.sqrt(gdim)
    k1 = np.arange(s1)
    n1 = np.arange(s1)
    n2 = np.arange(s2)
    idx = (k1[None, :, None] * (s2 * n1[None, None, :] + n2[:, None, None])) % seq
    phi = 2.0 * np.pi * idx / seq
    tab1 = np.concatenate([np.cos(phi), -np.sin(phi)], axis=1)
    k2 = np.arange(s2)
    ang2 = 2.0 * np.pi * ((k2[:, None] * n2[None, :]) % s2) / s2
    c2 = np.cos(ang2) / math.sqrt(seq)
    s2m = np.sin(ang2) / math.sqrt(seq)
    m2 = np.block([[c2, s2m], [-s2m, c2]])
    as_bf16 = lambda x: jnp.asarray(x, dtype=F32).astype(BF16)
    return as_bf16(cs_rows), as_bf16(tab1), as_bf16(m2)


def _rope_tables(rt):
    quarter = LANES // 4
    lane = lax.broadcasted_iota(jnp.int32, rt.shape, 1)
    q0 = lane < quarter
    q1 = (lane >= quarter) & (lane < 2 * quarter)
    ct = jnp.where(q0, rt, 0.0) + jnp.where(q1, pltpu.roll(rt, quarter, 1), 0.0)
    sa = jnp.where(q1, rt, 0.0)
    sb = jnp.where(q0, -pltpu.roll(rt, LANES - quarter, 1), 0.0)
    return ct, sa, sb


def _rope128(r, ct, sa, sb):
    quarter = LANES // 4
    return r * ct + pltpu.roll(r, quarter, 1) * sa + pltpu.roll(r, LANES - quarter, 1) * sb


def _q_proj_kernel(c_ref, g_ref, w_ref, rt_ref, o_ref, *, heads, scale):
    xn = _rms(c_ref[...].astype(F32), g_ref[...]).astype(BF16)
    acc = jnp.dot(xn, w_ref[...], preferred_element_type=F32)
    ct, sa, sb = _rope_tables(rt_ref[...])
    for h in range(heads):
        base = 2 * LANES * h
        o_ref[:, base:base + LANES] = (acc[:, base:base + LANES] * scale).astype(o_ref.dtype)
        r = _rope128(acc[:, base + LANES:base + 2 * LANES], ct, sa, sb)
        o_ref[:, base + LANES:base + 2 * LANES] = (r * scale).astype(o_ref.dtype)


def q_proj(z, g, w, rt, *, col_block, rank, heads, scale, tm):
    m = z.shape[0]
    tm = _tile(m, tm)
    n = w.shape[1]
    return pl.pallas_call(
        functools.partial(_q_proj_kernel, heads=heads, scale=scale),
        grid=(m // tm,),
        in_specs=[pl.BlockSpec((tm, rank), lambda i: (i, col_block)),
                  pl.BlockSpec((1, rank), lambda i: (0, 0)),
                  pl.BlockSpec((rank, n), lambda i: (0, 0)),
                  pl.BlockSpec((tm, LANES), lambda i: (i, 0))],
        out_specs=pl.BlockSpec((tm, n), lambda i: (i, 0)),
        out_shape=jax.ShapeDtypeStruct((m, n), BF16),
        compiler_params=_params("parallel"),
        name="q_proj",
    )(z, g.reshape(1, rank).astype(F32), w, rt)


def _kv_proj_kernel(c_ref, kr_ref, g_ref, w_ref, rt_ref, k_ref, v_ref, *, heads):
    xn = _rms(c_ref[...].astype(F32), g_ref[...]).astype(BF16)
    acc = jnp.dot(xn, w_ref[...], preferred_element_type=F32)
    kr = _rope128(kr_ref[...].astype(F32), *_rope_tables(rt_ref[...])).astype(k_ref.dtype)
    for h in range(heads):
        k_ref[:, 2 * LANES * h:2 * LANES * h + LANES] = acc[:, LANES * h:LANES * (h + 1)].astype(k_ref.dtype)
        k_ref[:, 2 * LANES * h + LANES:2 * LANES * (h + 1)] = kr
        v_ref[:, 2 * LANES * h:2 * LANES * h + LANES] = acc[:, LANES * (heads + h):LANES * (heads + h + 1)].astype(v_ref.dtype)
        v_ref[:, 2 * LANES * h + LANES:2 * LANES * (h + 1)] = jnp.ones((acc.shape[0], LANES), v_ref.dtype)


def kv_proj(z, g, w, rt, *, ckv_block, kr_block, rank, heads, tm):
    m = z.shape[0]
    tm = _tile(m, tm)
    return pl.pallas_call(
        functools.partial(_kv_proj_kernel, heads=heads),
        grid=(m // tm,),
        in_specs=[pl.BlockSpec((tm, rank), lambda i: (i, ckv_block)),
                  pl.BlockSpec((tm, LANES), lambda i: (i, kr_block)),
                  pl.BlockSpec((1, rank), lambda i: (0, 0)),
                  pl.BlockSpec((rank, 2 * LANES * heads), lambda i: (0, 0)),
                  pl.BlockSpec((tm, LANES), lambda i: (i, 0))],
        out_specs=[pl.BlockSpec((tm, 2 * LANES * heads), lambda i: (i, 0))] * 2,
        out_shape=[jax.ShapeDtypeStruct((m, 2 * LANES * heads), BF16)] * 2,
        compiler_params=_params("parallel"),
        name="kv_proj",
    )(z, z, g.reshape(1, rank).astype(F32), w, rt)


def _flash_kernel(q_ref, k_ref, v_ref, *rest, tk, nk, ncast):
    cast_in, o_ref, cast_out = rest[:ncast], rest[ncast], rest[ncast + 1:2 * ncast + 1]
    s0_ref, s1_ref = rest[2 * ncast + 1:]
    for src, dst in zip(cast_in, cast_out):
        dst[...] = src[...].astype(dst.dtype)
    q = q_ref[0]
    tq = q.shape[0]
    dv = o_ref.shape[-1]

    def chunk(c):
        start = c * tk
        return pl.ds(start if isinstance(start, int) else pl.multiple_of(start, tk), tk)

    def scores(c):
        return lax.dot_general(q, k_ref[0, chunk(c), :], (((1,), (1,)), ((), ())), preferred_element_type=F32)

    def fold(s_ref, c, m, acc):
        s = s_ref[...]
        m_new = jnp.maximum(m, jnp.max(s, axis=1, keepdims=True))
        p = jnp.exp2(s - m_new).astype(BF16)
        acc = jnp.exp2(m - m_new) * acc + jnp.dot(p, v_ref[0, chunk(c), :], preferred_element_type=F32)
        return m_new, acc

    def pair(j, carry, last):
        m, acc = carry
        s1_ref[...] = scores(2 * j + 1)
        m, acc = fold(s0_ref, 2 * j, m, acc)
        if not last:
            s0_ref[...] = scores(2 * j + 2)
        return fold(s1_ref, 2 * j + 1, m, acc)

    s0_ref[...] = scores(0)
    init = (jnp.full((tq, 1), -jnp.inf, F32), jnp.zeros((tq, 2 * dv), F32))
    carry = init
    for j in range(nk // 2):
        carry = pair(j, carry, j == nk // 2 - 1)
    _, acc = carry
    o_ref[0] = (acc[:, :dv] / acc[:, dv:]).astype(o_ref.dtype)


def flash_attention(q, k, v, *, heads, tq, tk, cast_weights=()):
    b, s, _ = q.shape
    dk = q.shape[-1] // heads
    dv2 = v.shape[-1] // heads
    dv = dv2 // 2
    tq, tk = _tile(s, tq), _tile(s, min(tk, s // 2))
    nk = s // tk
    assert nk % 2 == 0
    nq = s // tq
    steps = b * heads * nq
    wspecs = []
    for w in cast_weights:
        rows = w.shape[0] // steps
        assert rows * steps == w.shape[0] and rows % 16 == 0, (w.shape, steps)
        wspecs.append(pl.BlockSpec((rows, w.shape[1]), lambda bi, h, i: ((bi * heads + h) * nq + i, 0)))
    wspecs_in = wspecs_out = wspecs
    wshapes = [jax.ShapeDtypeStruct(w.shape, BF16) for w in cast_weights]
    outs = pl.pallas_call(
        functools.partial(_flash_kernel, tk=tk, nk=nk, ncast=len(cast_weights)),
        grid=(b, heads, nq),
        in_specs=[pl.BlockSpec((1, tq, dk), lambda bi, h, i: (bi, i, h)),
                  pl.BlockSpec((1, s, dk), lambda bi, h, i: (bi, 0, h)),
                  pl.BlockSpec((1, s, dv2), lambda bi, h, i: (bi, 0, h))] + wspecs_in,
        out_specs=[pl.BlockSpec((1, tq, dv), lambda bi, h, i: (bi, i, h))] + wspecs_out,
        out_shape=[jax.ShapeDtypeStruct((b, s, heads * dv), BF16)] + wshapes,
        scratch_shapes=[pltpu.VMEM((tq, tk), F32), pltpu.VMEM((tq, tk), F32)],
        compiler_params=_params("parallel", "parallel", "arbitrary"),
        name="flash_attention",
    )(q, k, v, *cast_weights)
    return outs[0], outs[1:]


def _out_proj_kernel(yf_ref, a_ref, ga_ref, w_ref, x_ref, o_ref, ya_ref):
    fw = yf_ref.shape[1]

    @pl.when(pl.program_id(1) == 0)
    def _():
        _rms_rows(a_ref, ga_ref, ya_ref)

    acc = jnp.dot(yf_ref[...], w_ref[:fw, :], preferred_element_type=F32)
    acc += jnp.dot(ya_ref[...], w_ref[fw:, :], preferred_element_type=F32)
    o_ref[...] = (acc + x_ref[...]).astype(o_ref.dtype)


def out_proj(yf, a, ga, w, x, *, tm, tn):
    m, fw = yf.shape
    aw = a.shape[1]
    n = w.shape[1]
    tm, tn = _tile(m, tm), _tile(n, tn)
    return pl.pallas_call(
        _out_proj_kernel,
        grid=(m // tm, n // tn),
        in_specs=[pl.BlockSpec((tm, fw), lambda i, j: (i, 0)),
                  pl.BlockSpec((tm, aw), lambda i, j: (i, 0)),
                  pl.BlockSpec((1, aw), lambda i, j: (0, 0)),
                  pl.BlockSpec((fw + aw, tn), lambda i, j: (0, j)),
                  pl.BlockSpec((tm, tn), lambda i, j: (i, j))],
        out_specs=pl.BlockSpec((tm, tn), lambda i, j: (i, j)),
        out_shape=jax.ShapeDtypeStruct((m, n), F32),
        scratch_shapes=[pltpu.VMEM((tm, aw), BF16)],
        compiler_params=_params("parallel", "arbitrary"),
        name="out_proj",
    )(yf, a, ga.reshape(1, aw).astype(F32), w, x)


def _absorb_qk_kernel(wq_ref, k_ref, o_ref, *, scale):
    acc = lax.dot_general(wq_ref[...], k_ref[...], (((1,), (1,)), ((), ())), preferred_element_type=F32)
    o_ref[0] = (acc * scale).astype(o_ref.dtype)


def absorb_qk(wq, kv, *, batch, heads, mem, hd, scale):
    d = wq.shape[0]
    return pl.pallas_call(
        functools.partial(_absorb_qk_kernel, scale=scale),
        grid=(heads, batch),
        in_specs=[pl.BlockSpec((d, hd), lambda h, b: (0, h)),
                  pl.BlockSpec((mem, hd), lambda h, b: (b, h))],
        out_specs=pl.BlockSpec((1, d, mem), lambda h, b: (b, 0, h)),
        out_shape=jax.ShapeDtypeStruct((batch, d, heads * mem), BF16),
        compiler_params=_params("parallel", "parallel"),
        name="absorb_qk",
    )(wq, kv)


def _absorb_vo_kernel(v_ref, wo_ref, o_ref):
    o_ref[0] = jnp.dot(v_ref[...], wo_ref[...], preferred_element_type=F32).astype(o_ref.dtype)


def absorb_vo(kv, wo, *, batch, heads, mem, hd, tn):
    d = wo.shape[1]
    tn = _tile(d, tn)
    return pl.pallas_call(
        _absorb_vo_kernel,
        grid=(heads, d // tn, batch),
        in_specs=[pl.BlockSpec((mem, hd), lambda h, j, b: (b, h)),
                  pl.BlockSpec((hd, tn), lambda h, j, b: (h, j))],
        out_specs=pl.BlockSpec((1, mem, tn), lambda h, j, b: (b, h, j)),
        out_shape=jax.ShapeDtypeStruct((batch, heads * mem, d), BF16),
        compiler_params=_params("parallel", "parallel", "parallel"),
        name="absorb_vo",
    )(kv, wo)


def _xattn_kernel(hf_ref, hnext_ref, g_ref, wqk_ref, vo_ref, g2_ref, o_ref, on_ref, p_ref, na_ref, nb_ref,
                  *, heads, mem):
    i = pl.program_id(0)
    rows = hf_ref.shape[0]

    @pl.when(i == 0)
    def _():
        _rms_rows(hf_ref, g_ref, na_ref)

    def step(cur_ref, nxt_ref):
        for c in range(rows // NORM_CHUNK_ROWS):
            r = pl.ds(c * NORM_CHUNK_ROWS, NORM_CHUNK_ROWS)
            nxt_ref[r, :] = _rms(hnext_ref[r, :], g_ref[...]).astype(BF16)
        s = jnp.dot(cur_ref[...], wqk_ref[0], preferred_element_type=F32)
        for h in range(heads):
            seg = s[:, h * mem:(h + 1) * mem]
            e = jnp.exp(seg - jnp.max(seg, axis=-1, keepdims=True))
            p_ref[:, h * mem:(h + 1) * mem] = (e / jnp.sum(e, axis=-1, keepdims=True)).astype(BF16)
        o_ref[...] = jnp.dot(p_ref[...], vo_ref[0], preferred_element_type=F32) + hf_ref[...]
        _rms_rows(o_ref, g2_ref, on_ref)

    @pl.when(i % 2 == 0)
    def _():
        step(na_ref, nb_ref)

    @pl.when(i % 2 == 1)
    def _():
        step(nb_ref, na_ref)


def xattn(h, g, wqk, vo, g_next, *, seq, heads, mem, tm):
    m, d = h.shape
    tm = _tile(seq, tm)
    per_b = seq // tm
    hm = heads * mem
    once = pl.Buffered(1)
    nt = m // tm
    return pl.pallas_call(
        functools.partial(_xattn_kernel, heads=heads, mem=mem),
        grid=(nt,),
        in_specs=[pl.BlockSpec((tm, d), lambda i: (i, 0)),
                  pl.BlockSpec((tm, d), lambda i: (jnp.minimum(i + 1, nt - 1), 0)),
                  pl.BlockSpec((1, d), lambda i: (0, 0)),
                  pl.BlockSpec((1, d, hm), lambda i: (i // per_b, 0, 0), pipeline_mode=once),
                  pl.BlockSpec((1, hm, d), lambda i: (i // per_b, 0, 0), pipeline_mode=once),
                  pl.BlockSpec((1, d), lambda i: (0, 0))],
        out_specs=[pl.BlockSpec((tm, d), lambda i: (i, 0))] * 2,
        out_shape=[jax.ShapeDtypeStruct((m, d), F32), jax.ShapeDtypeStruct((m, d), BF16)],
        scratch_shapes=[pltpu.VMEM((tm, hm), BF16), pltpu.VMEM((tm, d), BF16), pltpu.VMEM((tm, d), BF16)],
        compiler_params=_params("arbitrary"),
        name="xattn",
    )(h, h, g.reshape(1, d).astype(F32), wqk, vo, g_next.reshape(1, d).astype(F32))


def _pad_cols(w, n):
    return jnp.pad(w, ((0, 0), (0, n - w.shape[1])))


def kernel(x, mem, positions, g_mix, w_in, w_fourier, g_q_lora, w_uq, g_kv_lora, w_ukv, g_fourier_out, g_mla_out, w_out, g_xattn, g_mem, w_xq, w_xk, w_xv, w_xo, g_mlp, w_ff1, w_ff2, g_final):
    batch, seq, d = x.shape
    depth = g_mix.shape[0]
    groups, gdim = w_fourier.shape[1], w_fourier.shape[2]
    fw = groups * gdim
    q_rank, heads, qk_dim = w_uq.shape[1], w_uq.shape[2], w_uq.shape[3]
    kv_rank = w_ukv.shape[1]
    nope = LANES
    rope = qk_dim - nope
    vdim = w_ukv.shape[3] - nope
    assert rope == LANES // 2 and vdim == LANES and fw % q_rank == 0
    assert (fw + q_rank) % kv_rank == 0 and (fw + q_rank + kv_rank) % LANES == 0
    xheads, xhd = w_xq.shape[2], w_xq.shape[3]
    mtok = mem.shape[1]
    t = batch * seq
    s2 = FOURIER_SEQ_INNER
    s1 = seq // s2

    half = rope // 2
    inv_freq = ROPE_THETA ** (-jnp.arange(half, dtype=F32) / half)
    ang = positions.astype(F32).reshape(t, 1) * inv_freq
    rt = jnp.concatenate([jnp.cos(ang), jnp.sin(ang), jnp.zeros((t, LANES - 2 * half), F32)], axis=1)

    cs_rows, tab1, m2 = _dft_tables(seq, gdim)

    in_width = w_in.shape[2]
    z_width = -(-(in_width + rope) // 768) * 768
    scale = float(qk_dim) ** -0.5 * math.log2(math.e)

    h = x.reshape(t, d)
    mem2 = mem.reshape(batch * mtok, d)
    for layer in range(depth):
        w_in_b = _pad_cols(w_in[layer], z_width).astype(BF16)
        wq = jnp.pad(w_uq[layer], ((0, 0), (0, 0), (0, 2 * LANES - qk_dim))).reshape(q_rank, heads * 2 * LANES).astype(BF16)
        wkv = jnp.concatenate([w_ukv[layer][:, :, :nope].reshape(kv_rank, heads * nope),
                               w_ukv[layer][:, :, nope:].reshape(kv_rank, heads * vdim)], axis=1).astype(BF16)
        wcs = fold_fourier_weights(cs_rows, w_fourier[layer].astype(BF16))
        wxq = w_xq[layer].reshape(d, xheads * xhd).astype(BF16)
        wxk = w_xk[layer].reshape(d, xheads * xhd).astype(BF16)
        wxv = w_xv[layer].reshape(d, xheads * xhd).astype(BF16)

        z = norm_mm(h, g_mix[layer], w_in_b, tm=512, tn=768, out_dtype=BF16)
        zt = z[:, :fw].reshape(batch, s1, s2, fw).transpose(0, 2, 1, 3)
        a = seq_dft1(zt, tab1, nb=8)
        yf = seq_dft2(a.reshape(batch, 2 * s1, s2, fw), m2, wcs, g_fourier_out[layer],
                      s1=s1, kb=4).reshape(t, fw)
        qh = q_proj(z, g_q_lora[layer], wq, rt, col_block=fw // q_rank, rank=q_rank,
                    heads=heads, scale=scale, tm=512)
        kh, vh = kv_proj(z, g_kv_lora[layer], wkv, rt, ckv_block=(fw + q_rank) // kv_rank,
                         kr_block=(fw + q_rank + kv_rank) // LANES, rank=kv_rank, heads=heads, tm=512)
        o, (w_out_b, wxo, w1, w2) = flash_attention(
            qh.reshape(batch, seq, -1), kh.reshape(batch, seq, -1), vh.reshape(batch, seq, -1),
            heads=heads, tq=1024, tk=512,
            cast_weights=(w_out[layer], w_xo[layer].reshape(xheads * xhd, d), w_ff1[layer], w_ff2[layer]))
        o = o.reshape(t, heads * vdim)
        h = out_proj(yf, o, g_mla_out[layer], w_out_b, h, tm=1024, tn=512)

        km = norm_mm(mem2, g_mem[layer], wxk, tm=512, tn=512, out_dtype=BF16)
        vm = norm_mm(mem2, g_mem[layer], wxv, tm=512, tn=512, out_dtype=BF16)
        wqk = absorb_qk(wxq, km, batch=batch, heads=xheads, mem=mtok, hd=xhd, scale=float(xhd) ** -0.5)
        vo = absorb_vo(vm, wxo, batch=batch, heads=xheads, mem=mtok, hd=xhd, tn=2048)
        h, hn = xattn(h, g_xattn[layer], wqk, vo, g_mlp[layer], seq=seq, heads=xheads, mem=mtok, tm=256)

        act = mm_relu2(hn, w1, tm=1024, tn=1024, out_dtype=BF16)
        h = mm_res(act, w2, h, tm=1024, tn=1024, tk=4096)

    return rmsnorm(h, g_final, tm=256, out_dtype=x.dtype).reshape(batch, seq, d)
```

```python
import functools
import math

import numpy as np
import jax
import jax.numpy as jnp
from jax import lax
from jax.experimental import pallas as pl
from jax.experimental.pallas import tpu as pltpu

F32 = jnp.float32
BF16 = jnp.bfloat16

NORM_EPS = 1e-6
ROPE_THETA = 10000.0
LANES = 128
VMEM_LIMIT_BYTES = 56 * 1024 * 1024
FOURIER_SEQ_INNER = 128
FLASH_SCORE_BUFFERS = 2


def _params(*sem):
    return pltpu.CompilerParams(dimension_semantics=sem, vmem_limit_bytes=VMEM_LIMIT_BYTES)


def _rms(xf, g):
    ms = jnp.mean(xf * xf, axis=-1, keepdims=True)
    return xf * lax.rsqrt(ms + NORM_EPS) * g


def _rms_rows(a_ref, g_ref, o_ref, chunk=64):
    rows = a_ref.shape[0]
    chunk = min(chunk, rows)

    def body(c, carry):
        r = pl.ds(pl.multiple_of(c * chunk, chunk), chunk)
        o_ref[r, :] = _rms(a_ref[r, :].astype(F32), g_ref[...]).astype(o_ref.dtype)
        return carry

    lax.fori_loop(0, rows // chunk, body, 0)


def _tile(dim, pref):
    t = min(dim, pref)
    assert dim % t == 0, (dim, pref)
    return t


NORM_CHUNK_ROWS = 32


def _norm_mm_kernel(a_ref, g_ref, w_ref, o_ref, xa_ref, xb_ref, *, slice_rows, n_slices):
    r, j = pl.program_id(0), pl.program_id(1)

    def norm_slice(dst_ref):
        start = jnp.minimum(j, n_slices - 1) * slice_rows
        for c in range(slice_rows // NORM_CHUNK_ROWS):
            src = pl.ds(c * NORM_CHUNK_ROWS, NORM_CHUNK_ROWS)
            dst = pl.ds(pl.multiple_of(start + c * NORM_CHUNK_ROWS, NORM_CHUNK_ROWS), NORM_CHUNK_ROWS)
            dst_ref[dst, :] = _rms(a_ref[src, :].astype(F32), g_ref[...]).astype(BF16)

    def step(cur_ref, nxt_ref):
        norm_slice(nxt_ref)
        o_ref[...] = jnp.dot(cur_ref[...], w_ref[...], preferred_element_type=F32).astype(o_ref.dtype)

    @pl.when(r == 0)
    def _():
        norm_slice(xa_ref)

    @pl.when(r % 2 == 1)
    def _():
        step(xa_ref, xb_ref)

    @pl.when((r > 0) & (r % 2 == 0))
    def _():
        step(xb_ref, xa_ref)


def norm_mm(a, g, w, *, tm, tn, out_dtype):
    m, k = a.shape
    n = w.shape[1]
    tm, tn = _tile(m, tm), _tile(n, tn)
    ni, nj = m // tm, n // tn
    n_slices = 1
    while n_slices * 2 <= nj and (tm // (n_slices * 2)) % NORM_CHUNK_ROWS == 0:
        n_slices *= 2
    slice_rows = tm // n_slices
    first = lambda r, j: jnp.where(r == 0, 0, j)
    return pl.pallas_call(
        functools.partial(_norm_mm_kernel, slice_rows=slice_rows, n_slices=n_slices),
        grid=(ni + 1, nj),
        in_specs=[pl.BlockSpec((slice_rows, k),
                               lambda r, j: (jnp.minimum(r, ni - 1) * n_slices + jnp.minimum(j, n_slices - 1), 0)),
                  pl.BlockSpec((1, k), lambda r, j: (0, 0)),
                  pl.BlockSpec((k, tn), lambda r, j: (0, first(r, j)))],
        out_specs=pl.BlockSpec((tm, tn), lambda r, j: (jnp.maximum(r - 1, 0), first(r, j))),
        out_shape=jax.ShapeDtypeStruct((m, n), out_dtype),
        scratch_shapes=[pltpu.VMEM((tm, k), BF16), pltpu.VMEM((tm, k), BF16)],
        compiler_params=_params("arbitrary", "arbitrary"),
        name="norm_mm",
    )(a, g.reshape(1, k).astype(F32), w)


def _mm_relu2_kernel(a_ref, w_ref, o_ref):
    acc = jnp.dot(a_ref[...], w_ref[...], preferred_element_type=F32)
    o_ref[...] = jnp.square(jnp.maximum(acc, 0.0)).astype(o_ref.dtype)


def mm_relu2(a, w, *, tm, tn, out_dtype):
    m, k = a.shape
    n = w.shape[1]
    tm, tn = _tile(m, tm), _tile(n, tn)
    return pl.pallas_call(
        _mm_relu2_kernel,
        grid=(m // tm, n // tn),
        in_specs=[pl.BlockSpec((tm, k), lambda i, j: (i, 0)),
                  pl.BlockSpec((k, tn), lambda i, j: (0, j))],
        out_specs=pl.BlockSpec((tm, tn), lambda i, j: (i, j)),
        out_shape=jax.ShapeDtypeStruct((m, n), out_dtype),
        compiler_params=_params("parallel", "parallel"),
        name="mm_relu2",
    )(a, w)


def _mm_res_kernel(a_ref, w_ref, res_ref, o_ref):
    @pl.when(pl.program_id(2) == 0)
    def _():
        o_ref[...] = res_ref[...]

    o_ref[...] += jnp.dot(a_ref[...], w_ref[...], preferred_element_type=F32)


def mm_res(a, w, res, *, tm, tn, tk):
    m, k = a.shape
    n = w.shape[1]
    tm, tn, tk = _tile(m, tm), _tile(n, tn), _tile(k, tk)
    return pl.pallas_call(
        _mm_res_kernel,
        grid=(m // tm, n // tn, k // tk),
        in_specs=[pl.BlockSpec((tm, tk), lambda i, j, kk: (i, kk)),
                  pl.BlockSpec((tk, tn), lambda i, j, kk: (kk, j)),
                  pl.BlockSpec((tm, tn), lambda i, j, kk: (i, j))],
        out_specs=pl.BlockSpec((tm, tn), lambda i, j, kk: (i, j)),
        out_shape=jax.ShapeDtypeStruct((m, n), F32),
        compiler_params=_params("parallel", "parallel", "arbitrary"),
        name="mm_res",
    )(a, w, res)


def _rmsnorm_kernel(a_ref, g_ref, o_ref):
    _rms_rows(a_ref, g_ref, o_ref)


def rmsnorm(a, g, *, tm, out_dtype):
    m, k = a.shape
    tm = _tile(m, tm)
    return pl.pallas_call(
        _rmsnorm_kernel,
        grid=(m // tm,),
        in_specs=[pl.BlockSpec((tm, k), lambda i: (i, 0)), pl.BlockSpec((1, k), lambda i: (0, 0))],
        out_specs=pl.BlockSpec((tm, k), lambda i: (i, 0)),
        out_shape=jax.ShapeDtypeStruct((m, k), out_dtype),
        compiler_params=_params("parallel"),
        name="rmsnorm",
    )(a, g.reshape(1, k).astype(F32))


def _fold_fourier_kernel(cs_ref, wf_ref, o_ref):
    o_ref[0] = jnp.dot(cs_ref[...], wf_ref[0], preferred_element_type=F32).astype(o_ref.dtype)


def fold_fourier_weights(cs_rows, wf):
    groups, gdim, _ = wf.shape
    return pl.pallas_call(
        _fold_fourier_kernel,
        grid=(groups,),
        in_specs=[pl.BlockSpec((2 * gdim, gdim), lambda g: (0, 0)),
                  pl.BlockSpec((1, gdim, gdim), lambda g: (g, 0, 0))],
        out_specs=pl.BlockSpec((1, 2 * gdim, gdim), lambda g: (g, 0, 0)),
        out_shape=jax.ShapeDtypeStruct((groups, 2 * gdim, gdim), BF16),
        compiler_params=_params("parallel"),
        name="fold_fourier_weights",
    )(cs_rows, wf)


def _seq_dft1_kernel(z_ref, t_ref, o_ref, *, nb, width):
    for t in range(nb):
        acc = jnp.dot(t_ref[t], z_ref[0, t], preferred_element_type=F32)
        o_ref[0, :, t * width:(t + 1) * width] = acc.astype(o_ref.dtype)


def seq_dft1(zt, tab, *, nb):
    b, s2, s1, width = zt.shape
    nb = _tile(s2, nb)
    return pl.pallas_call(
        functools.partial(_seq_dft1_kernel, nb=nb, width=width),
        grid=(b, s2 // nb),
        in_specs=[pl.BlockSpec((1, nb, s1, width), lambda bi, j: (bi, j, 0, 0)),
                  pl.BlockSpec((nb, 2 * s1, s1), lambda bi, j: (j, 0, 0))],
        out_specs=pl.BlockSpec((1, 2 * s1, nb * width), lambda bi, j: (bi, 0, j)),
        out_shape=jax.ShapeDtypeStruct((b, 2 * s1, s2 * width), BF16),
        compiler_params=_params("parallel", "parallel"),
        name="seq_dft1",
    )(zt, tab)


def _seq_dft2_kernel(re_ref, im_ref, m2_ref, wcs_ref, gf_ref, o_ref, *, kb, width):
    groups, _, gdim = wcs_ref.shape
    s2 = re_ref.shape[2]
    for t in range(kb):
        a = jnp.concatenate([re_ref[0, t], im_ref[0, t]], axis=0)
        y = jnp.dot(m2_ref[...], a, preferred_element_type=F32).astype(BF16)
        y_re, y_im = y[:s2], y[s2:]
        parts = []
        for g in range(groups):
            sl = slice(g * gdim, (g + 1) * gdim)
            lhs = jnp.concatenate([y_re[:, sl], y_im[:, sl]], axis=1)
            parts.append(jnp.dot(lhs, wcs_ref[g], preferred_element_type=F32))
        ms = sum(jnp.sum(p * p, axis=-1, keepdims=True) for p in parts) / width
        inv = lax.rsqrt(ms + NORM_EPS)
        for g in range(groups):
            sl = slice(g * gdim, (g + 1) * gdim)
            o_ref[0, :, t * width + g * gdim:t * width + (g + 1) * gdim] = (
                parts[g] * inv * gf_ref[:, sl]).astype(o_ref.dtype)


def seq_dft2(a4, m2, wcs, gf, *, s1, kb):
    b, _, s2, width = a4.shape
    kb = _tile(s1, kb)
    nblk = s1 // kb
    return pl.pallas_call(
        functools.partial(_seq_dft2_kernel, kb=kb, width=width),
        grid=(b, nblk),
        in_specs=[
            pl.BlockSpec((1, kb, s2, width), lambda bi, j: (bi, j, 0, 0)),
            pl.BlockSpec((1, kb, s2, width), lambda bi, j: (bi, j + nblk, 0, 0)),
            pl.BlockSpec((2 * s2, 2 * s2), lambda bi, j: (0, 0)),
            pl.BlockSpec(wcs.shape, lambda bi, j: (0, 0, 0)),
            pl.BlockSpec((1, width), lambda bi, j: (0, 0)),
        ],
        out_specs=pl.BlockSpec((1, s2, kb * width), lambda bi, j: (bi, 0, j)),
        out_shape=jax.ShapeDtypeStruct((b, s2, s1 * width), BF16),
        compiler_params=_params("parallel", "parallel"),
        name="seq_dft2",
    )(a4, a4, m2, wcs, gf.reshape(1, width).astype(F32))


def _dft_tables(seq, gdim):
    s2 = FOURIER_SEQ_INNER
    s1 = seq // s2
    c = np.arange(gdim)
    ang = 2.0 * np.pi * ((c[:, None] * c[None, :]) % gdim) / gdim
    cs_rows = np.concatenate([np.cos(ang), np.sin(ang)], axis=0) / math.sqrt(gdim)
    k1 = np.arange(s1)
    n1 = np.arange(s1)
    n2 = np.arange(s2)
    idx = (k1[None, :, None] * (s2 * n1[None, None, :] + n2[:, None, None])) % seq
    phi = 2.0 * np.pi * idx / seq
    tab1 = np.concatenate([np.cos(phi), -np.sin(phi)], axis=1)
    k2 = np.arange(s2)
    ang2 = 2.0 * np.pi * ((k2[:, None] * n2[None, :]) % s2) / s2
    c2 = np.cos(ang2) / math.sqrt(seq)
    s2m = np.sin(ang2) / math.sqrt(seq)
    m2 = np.block([[c2, s2m], [-s2m, c2]])
    as_bf16 = lambda x: jnp.asarray(x, dtype=F32).astype(BF16)
    return as_bf16(cs_rows), as_bf16(tab1), as_bf16(m2)


def _rope_tables(rt):
    quarter = LANES // 4
    lane = lax.broadcasted_iota(jnp.int32, rt.shape, 1)
    q0 = lane < quarter
    q1 = (lane >= quarter) & (lane < 2 * quarter)
    ct = jnp.where(q0, rt, 0.0) + jnp.where(q1, pltpu.roll(rt, quarter, 1), 0.0)
    sa = jnp.where(q1, rt, 0.0)
    sb = jnp.where(q0, -pltpu.roll(rt, LANES - quarter, 1), 0.0)
    return ct, sa, sb


def _rope128(r, ct, sa, sb):
    quarter = LANES // 4
    return r * ct + pltpu.roll(r, quarter, 1) * sa + pltpu.roll(r, LANES - quarter, 1) * sb


def _q_proj_kernel(c_ref, g_ref, w_ref, rt_ref, o_ref, *, heads, scale):
    xn = _rms(c_ref[...].astype(F32), g_ref[...]).astype(BF16)
    acc = jnp.dot(xn, w_ref[...], preferred_element_type=F32)
    ct, sa, sb = _rope_tables(rt_ref[...])
    for h in range(heads):
        base = 2 * LANES * h
        o_ref[:, base:base + LANES] = (acc[:, base:base + LANES] * scale).astype(o_ref.dtype)
        r = _rope128(acc[:, base + LANES:base + 2 * LANES], ct, sa, sb)
        o_ref[:, base + LANES:base + 2 * LANES] = (r * scale).astype(o_ref.dtype)


def q_proj(z, g, w, rt, *, col_block, rank, heads, scale, tm):
    m = z.shape[0]
    tm = _tile(m, tm)
    n = w.shape[1]
    return pl.pallas_call(
        functools.partial(_q_proj_kernel, heads=heads, scale=scale),
        grid=(m // tm,),
        in_specs=[pl.BlockSpec((tm, rank), lambda i: (i, col_block)),
                  pl.BlockSpec((1, rank), lambda i: (0, 0)),
                  pl.BlockSpec((rank, n), lambda i: (0, 0)),
                  pl.BlockSpec((tm, LANES), lambda i: (i, 0))],
        out_specs=pl.BlockSpec((tm, n), lambda i: (i, 0)),
        out_shape=jax.ShapeDtypeStruct((m, n), BF16),
        compiler_params=_params("parallel"),
        name="q_proj",
    )(z, g.reshape(1, rank).astype(F32), w, rt)


def _kv_proj_kernel(c_ref, kr_ref, g_ref, w_ref, rt_ref, k_ref, v_ref, *, heads):
    xn = _rms(c_ref[...].astype(F32), g_ref[...]).astype(BF16)
    acc = jnp.dot(xn, w_ref[...], preferred_element_type=F32)
    kr = _rope128(kr_ref[...].astype(F32), *_rope_tables(rt_ref[...])).astype(k_ref.dtype)
    for h in range(heads):
        k_ref[:, 2 * LANES * h:2 * LANES * h + LANES] = acc[:, LANES * h:LANES * (h + 1)].astype(k_ref.dtype)
        k_ref[:, 2 * LANES * h + LANES:2 * LANES * (h + 1)] = kr
        v_ref[:, 2 * LANES * h:2 * LANES * h + LANES] = acc[:, LANES * (heads + h):LANES * (heads + h + 1)].astype(v_ref.dtype)
        v_ref[:, 2 * LANES * h + LANES:2 * LANES * (h + 1)] = jnp.ones((acc.shape[0], LANES), v_ref.dtype)


def kv_proj(z, g, w, rt, *, ckv_block, kr_block, rank, heads, tm):
    m = z.shape[0]
    tm = _tile(m, tm)
    return pl.pallas_call(
        functools.partial(_kv_proj_kernel, heads=heads),
        grid=(m // tm,),
        in_specs=[pl.BlockSpec((tm, rank), lambda i: (i, ckv_block)),
                  pl.BlockSpec((tm, LANES), lambda i: (i, kr_block)),
                  pl.BlockSpec((1, rank), lambda i: (0, 0)),
                  pl.BlockSpec((rank, 2 * LANES * heads), lambda i: (0, 0)),
                  pl.BlockSpec((tm, LANES), lambda i: (i, 0))],
        out_specs=[pl.BlockSpec((tm, 2 * LANES * heads), lambda i: (i, 0))] * 2,
        out_shape=[jax.ShapeDtypeStruct((m, 2 * LANES * heads), BF16)] * 2,
        compiler_params=_params("parallel"),
        name="kv_proj",
    )(z, z, g.reshape(1, rank).astype(F32), w, rt)


def _flash_kernel(q_ref, k_ref, v_ref, *rest, tk, nk, ncast):
    cast_in, o_ref, cast_out = rest[:ncast], rest[ncast], rest[ncast + 1:2 * ncast + 1]
    s_refs = rest[2 * ncast + 1:]
    for src, dst in zip(cast_in, cast_out):
        dst[...] = src[...].astype(dst.dtype)
    q = q_ref[0]
    tq = q.shape[0]
    dv = o_ref.shape[-1]

    def chunk(c):
        start = c * tk
        return pl.ds(start if isinstance(start, int) else pl.multiple_of(start, tk), tk)

    def scores(c):
        return lax.dot_general(q, k_ref[0, chunk(c), :], (((1,), (1,)), ((), ())), preferred_element_type=F32)

    def fold(s_ref, c, m, acc):
        s = s_ref[...]
        m_new = jnp.maximum(m, jnp.max(s, axis=1, keepdims=True))
        p = jnp.exp2(s - m_new).astype(BF16)
        acc = jnp.exp2(m - m_new) * acc + jnp.dot(p, v_ref[0, chunk(c), :], preferred_element_type=F32)
        return m_new, acc

    depth = len(s_refs)
    for c in range(min(depth - 1, nk)):
        s_refs[c][...] = scores(c)
    m, acc = jnp.full((tq, 1), -jnp.inf, F32), jnp.zeros((tq, 2 * dv), F32)
    for c in range(nk):
        ahead = c + depth - 1
        if ahead < nk:
            s_refs[ahead % depth][...] = scores(ahead)
        m, acc = fold(s_refs[c % depth], c, m, acc)
    o_ref[0] = (acc[:, :dv] / acc[:, dv:]).astype(o_ref.dtype)


def flash_attention(q, k, v, *, heads, tq, tk, cast_weights=()):
    b, s, _ = q.shape
    dk = q.shape[-1] // heads
    dv2 = v.shape[-1] // heads
    dv = dv2 // 2
    tq, tk = _tile(s, tq), _tile(s, min(tk, s // 2))
    nk = s // tk
    nq = s // tq
    steps = b * heads * nq
    wspecs = []
    for w in cast_weights:
        rows = w.shape[0] // steps
        assert rows * steps == w.shape[0] and rows % 16 == 0, (w.shape, steps)
        wspecs.append(pl.BlockSpec((rows, w.shape[1]), lambda bi, h, i: ((bi * heads + h) * nq + i, 0)))
    wspecs_in = wspecs_out = wspecs
    wshapes = [jax.ShapeDtypeStruct(w.shape, BF16) for w in cast_weights]
    outs = pl.pallas_call(
        functools.partial(_flash_kernel, tk=tk, nk=nk, ncast=len(cast_weights)),
        grid=(b, heads, nq),
        in_specs=[pl.BlockSpec((1, tq, dk), lambda bi, h, i: (bi, i, h)),
                  pl.BlockSpec((1, s, dk), lambda bi, h, i: (bi, 0, h)),
                  pl.BlockSpec((1, s, dv2), lambda bi, h, i: (bi, 0, h))] + wspecs_in,
        out_specs=[pl.BlockSpec((1, tq, dv), lambda bi, h, i: (bi, i, h))] + wspecs_out,
        out_shape=[jax.ShapeDtypeStruct((b, s, heads * dv), BF16)] + wshapes,
        scratch_shapes=[pltpu.VMEM((tq, tk), F32)] * FLASH_SCORE_BUFFERS,
        compiler_params=_params("parallel", "parallel", "arbitrary"),
        name="flash_attention",
    )(q, k, v, *cast_weights)
    return outs[0], outs[1:]


def _out_proj_kernel(yf_ref, a_ref, ga_ref, w_ref, x_ref, o_ref, ya_ref):
    fw = yf_ref.shape[1]

    @pl.when(pl.program_id(1) == 0)
    def _():
        _rms_rows(a_ref, ga_ref, ya_ref)

    acc = jnp.dot(yf_ref[...], w_ref[:fw, :], preferred_element_type=F32)
    acc += jnp.dot(ya_ref[...], w_ref[fw:, :], preferred_element_type=F32)
    o_ref[...] = (acc + x_ref[...]).astype(o_ref.dtype)


def out_proj(yf, a, ga, w, x, *, tm, tn):
    m, fw = yf.shape
    aw = a.shape[1]
    n = w.shape[1]
    tm, tn = _tile(m, tm), _tile(n, tn)
    return pl.pallas_call(
        _out_proj_kernel,
        grid=(m // tm, n // tn),
        in_specs=[pl.BlockSpec((tm, fw), lambda i, j: (i, 0)),
                  pl.BlockSpec((tm, aw), lambda i, j: (i, 0)),
                  pl.BlockSpec((1, aw), lambda i, j: (0, 0)),
                  pl.BlockSpec((fw + aw, tn), lambda i, j: (0, j)),
                  pl.BlockSpec((tm, tn), lambda i, j: (i, j))],
        out_specs=pl.BlockSpec((tm, tn), lambda i, j: (i, j)),
        out_shape=jax.ShapeDtypeStruct((m, n), F32),
        scratch_shapes=[pltpu.VMEM((tm, aw), BF16)],
        compiler_params=_params("parallel", "arbitrary"),
        name="out_proj",
    )(yf, a, ga.reshape(1, aw).astype(F32), w, x)


def _absorb_qk_kernel(wq_ref, k_ref, o_ref, *, scale):
    acc = lax.dot_general(wq_ref[...], k_ref[...], (((1,), (1,)), ((), ())), preferred_element_type=F32)
    o_ref[0] = (acc * scale).astype(o_ref.dtype)


def absorb_qk(wq, kv, *, batch, heads, mem, hd, scale):
    d = wq.shape[0]
    return pl.pallas_call(
        functools.partial(_absorb_qk_kernel, scale=scale),
        grid=(heads, batch),
        in_specs=[pl.BlockSpec((d, hd), lambda h, b: (0, h)),
                  pl.BlockSpec((mem, hd), lambda h, b: (b, h))],
        out_specs=pl.BlockSpec((1, d, mem), lambda h, b: (b, 0, h)),
        out_shape=jax.ShapeDtypeStruct((batch, d, heads * mem), BF16),
        compiler_params=_params("parallel", "parallel"),
        name="absorb_qk",
    )(wq, kv)


def _absorb_vo_kernel(v_ref, wo_ref, o_ref):
    o_ref[0] = jnp.dot(v_ref[...], wo_ref[...], preferred_element_type=F32).astype(o_ref.dtype)


def absorb_vo(kv, wo, *, batch, heads, mem, hd, tn):
    d = wo.shape[1]
    tn = _tile(d, tn)
    return pl.pallas_call(
        _absorb_vo_kernel,
        grid=(heads, d // tn, batch),
        in_specs=[pl.BlockSpec((mem, hd), lambda h, j, b: (b, h)),
                  pl.BlockSpec((hd, tn), lambda h, j, b: (h, j))],
        out_specs=pl.BlockSpec((1, mem, tn), lambda h, j, b: (b, h, j)),
        out_shape=jax.ShapeDtypeStruct((batch, heads * mem, d), BF16),
        compiler_params=_params("parallel", "parallel", "parallel"),
        name="absorb_vo",
    )(kv, wo)


def _xattn_kernel(hf_ref, hnext_ref, g_ref, wqk_ref, vo_ref, g2_ref, o_ref, on_ref, p_ref, na_ref, nb_ref,
                  *, heads, mem):
    i = pl.program_id(0)
    rows = hf_ref.shape[0]

    @pl.when(i == 0)
    def _():
        _rms_rows(hf_ref, g_ref, na_ref)

    def step(cur_ref, nxt_ref):
        for c in range(rows // NORM_CHUNK_ROWS):
            r = pl.ds(c * NORM_CHUNK_ROWS, NORM_CHUNK_ROWS)
            nxt_ref[r, :] = _rms(hnext_ref[r, :], g_ref[...]).astype(BF16)
        s = jnp.dot(cur_ref[...], wqk_ref[0], preferred_element_type=F32)
        for h in range(heads):
            seg = s[:, h * mem:(h + 1) * mem]
            e = jnp.exp(seg - jnp.max(seg, axis=-1, keepdims=True))
            p_ref[:, h * mem:(h + 1) * mem] = (e / jnp.sum(e, axis=-1, keepdims=True)).astype(BF16)
        o_ref[...] = jnp.dot(p_ref[...], vo_ref[0], preferred_element_type=F32) + hf_ref[...]
        _rms_rows(o_ref, g2_ref, on_ref)

    @pl.when(i % 2 == 0)
    def _():
        step(na_ref, nb_ref)

    @pl.when(i % 2 == 1)
    def _():
        step(nb_ref, na_ref)


def xattn(h, g, wqk, vo, g_next, *, seq, heads, mem, tm):
    m, d = h.shape
    tm = _tile(seq, tm)
    per_b = seq // tm
    hm = heads * mem
    once = pl.Buffered(1)
    nt = m // tm
    return pl.pallas_call(
        functools.partial(_xattn_kernel, heads=heads, mem=mem),
        grid=(nt,),
        in_specs=[pl.BlockSpec((tm, d), lambda i: (i, 0)),
                  pl.BlockSpec((tm, d), lambda i: (jnp.minimum(i + 1, nt - 1), 0)),
                  pl.BlockSpec((1, d), lambda i: (0, 0)),
                  pl.BlockSpec((1, d, hm), lambda i: (i // per_b, 0, 0), pipeline_mode=once),
                  pl.BlockSpec((1, hm, d), lambda i: (i // per_b, 0, 0), pipeline_mode=once),
                  pl.BlockSpec((1, d), lambda i: (0, 0))],
        out_specs=[pl.BlockSpec((tm, d), lambda i: (i, 0))] * 2,
        out_shape=[jax.ShapeDtypeStruct((m, d), F32), jax.ShapeDtypeStruct((m, d), BF16)],
        scratch_shapes=[pltpu.VMEM((tm, hm), BF16), pltpu.VMEM((tm, d), BF16), pltpu.VMEM((tm, d), BF16)],
        compiler_params=_params("arbitrary"),
        name="xattn",
    )(h, h, g.reshape(1, d).astype(F32), wqk, vo, g_next.reshape(1, d).astype(F32))


def _pad_cols(w, n):
    return jnp.pad(w, ((0, 0), (0, n - w.shape[1])))


def kernel(x, mem, positions, g_mix, w_in, w_fourier, g_q_lora, w_uq, g_kv_lora, w_ukv, g_fourier_out, g_mla_out, w_out, g_xattn, g_mem, w_xq, w_xk, w_xv, w_xo, g_mlp, w_ff1, w_ff2, g_final):
    batch, seq, d = x.shape
    depth = g_mix.shape[0]
    groups, gdim = w_fourier.shape[1], w_fourier.shape[2]
    fw = groups * gdim
    q_rank, heads, qk_dim = w_uq.shape[1], w_uq.shape[2], w_uq.shape[3]
    kv_rank = w_ukv.shape[1]
    nope = LANES
    rope = qk_dim - nope
    vdim = w_ukv.shape[3] - nope
    assert rope == LANES // 2 and vdim == LANES and fw % q_rank == 0
    assert (fw + q_rank) % kv_rank == 0 and (fw + q_rank + kv_rank) % LANES == 0
    xheads, xhd = w_xq.shape[2], w_xq.shape[3]
    mtok = mem.shape[1]
    t = batch * seq
    s2 = FOURIER_SEQ_INNER
    s1 = seq // s2

    half = rope // 2
    inv_freq = ROPE_THETA ** (-jnp.arange(half, dtype=F32) / half)
    ang = positions.astype(F32).reshape(t, 1) * inv_freq
    rt = jnp.concatenate([jnp.cos(ang), jnp.sin(ang), jnp.zeros((t, LANES - 2 * half), F32)], axis=1)

    cs_rows, tab1, m2 = _dft_tables(seq, gdim)

    in_width = w_in.shape[2]
    z_width = -(-(in_width + rope) // 768) * 768
    scale = float(qk_dim) ** -0.5 * math.log2(math.e)

    h = x.reshape(t, d)
    mem2 = mem.reshape(batch * mtok, d)
    for layer in range(depth):
        w_in_b = _pad_cols(w_in[layer], z_width).astype(BF16)
        wq = jnp.pad(w_uq[layer], ((0, 0), (0, 0), (0, 2 * LANES - qk_dim))).reshape(q_rank, heads * 2 * LANES).astype(BF16)
        wkv = jnp.concatenate([w_ukv[layer][:, :, :nope].reshape(kv_rank, heads * nope),
                               w_ukv[layer][:, :, nope:].reshape(kv_rank, heads * vdim)], axis=1).astype(BF16)
        wcs = fold_fourier_weights(cs_rows, w_fourier[layer].astype(BF16))
        wxq = w_xq[layer].reshape(d, xheads * xhd).astype(BF16)
        wxk = w_xk[layer].reshape(d, xheads * xhd).astype(BF16)
        wxv = w_xv[layer].reshape(d, xheads * xhd).astype(BF16)

        z = norm_mm(h, g_mix[layer], w_in_b, tm=1024, tn=768, out_dtype=BF16)
        zt = z[:, :fw].reshape(batch, s1, s2, fw).transpose(0, 2, 1, 3)
        a = seq_dft1(zt, tab1, nb=8)
        yf = seq_dft2(a.reshape(batch, 2 * s1, s2, fw), m2, wcs, g_fourier_out[layer],
                      s1=s1, kb=4).reshape(t, fw)
        qh = q_proj(z, g_q_lora[layer], wq, rt, col_block=fw // q_rank, rank=q_rank,
                    heads=heads, scale=scale, tm=512)
        kh, vh = kv_proj(z, g_kv_lora[layer], wkv, rt, ckv_block=(fw + q_rank) // kv_rank,
                         kr_block=(fw + q_rank + kv_rank) // LANES, rank=kv_rank, heads=heads, tm=512)
        o, (w_out_b, wxo, w1, w2) = flash_attention(
            qh.reshape(batch, seq, -1), kh.reshape(batch, seq, -1), vh.reshape(batch, seq, -1),
            heads=heads, tq=1024, tk=512,
            cast_weights=(w_out[layer], w_xo[layer].reshape(xheads * xhd, d), w_ff1[layer], w_ff2[layer]))
        o = o.reshape(t, heads * vdim)
        h = out_proj(yf, o, g_mla_out[layer], w_out_b, h, tm=1024, tn=512)

        km = norm_mm(mem2, g_mem[layer], wxk, tm=512, tn=512, out_dtype=BF16)
        vm = norm_mm(mem2, g_mem[layer], wxv, tm=512, tn=512, out_dtype=BF16)
        wqk = absorb_qk(wxq, km, batch=batch, heads=xheads, mem=mtok, hd=xhd, scale=float(xhd) ** -0.5)
        vo = absorb_vo(vm, wxo, batch=batch, heads=xheads, mem=mtok, hd=xhd, tn=2048)
        h, hn = xattn(h, g_xattn[layer], wqk, vo, g_mlp[layer], seq=seq, heads=xheads, mem=mtok, tm=256)

        act = mm_relu2(hn, w1, tm=1024, tn=1024, out_dtype=BF16)
        h = mm_res(act, w2, h, tm=1024, tn=1024, tk=4096)

    return rmsnorm(h, g_final, tm=256, out_dtype=x.dtype).reshape(batch, seq, d)
```

```python
import functools
import math

import numpy as np
import jax
import jax.numpy as jnp
from jax import lax
from jax.experimental import pallas as pl
from jax.experimental.pallas import tpu as pltpu

F32 = jnp.float32
BF16 = jnp.bfloat16

NORM_EPS = 1e-6
ROPE_THETA = 10000.0
LANES = 128
VMEM_LIMIT_BYTES = 56 * 1024 * 1024
FOURIER_SEQ_INNER = 128
FLASH_SCORE_BUFFERS = 2


def _params(*sem):
    return pltpu.CompilerParams(dimension_semantics=sem, vmem_limit_bytes=VMEM_LIMIT_BYTES)


def _rms(xf, g):
    ms = jnp.mean(xf * xf, axis=-1, keepdims=True)
    return xf * lax.rsqrt(ms + NORM_EPS) * g


def _rms_rows(a_ref, g_ref, o_ref, chunk=64):
    rows = a_ref.shape[0]
    chunk = min(chunk, rows)

    def body(c, carry):
        r = pl.ds(pl.multiple_of(c * chunk, chunk), chunk)
        o_ref[r, :] = _rms(a_ref[r, :].astype(F32), g_ref[...]).astype(o_ref.dtype)
        return carry

    lax.fori_loop(0, rows // chunk, body, 0)


def _tile(dim, pref):
    t = min(dim, pref)
    assert dim % t == 0, (dim, pref)
    return t


NORM_CHUNK_ROWS = 32


def _norm_mm_kernel(a_ref, g_ref, w_ref, o_ref, xa_ref, xb_ref, *, slice_rows, n_slices):
    r, j = pl.program_id(0), pl.program_id(1)

    def norm_slice(dst_ref):
        start = jnp.minimum(j, n_slices - 1) * slice_rows
        for c in range(slice_rows // NORM_CHUNK_ROWS):
            src = pl.ds(c * NORM_CHUNK_ROWS, NORM_CHUNK_ROWS)
            dst = pl.ds(pl.multiple_of(start + c * NORM_CHUNK_ROWS, NORM_CHUNK_ROWS), NORM_CHUNK_ROWS)
            dst_ref[dst, :] = _rms(a_ref[src, :].astype(F32), g_ref[...]).astype(BF16)

    def step(cur_ref, nxt_ref):
        norm_slice(nxt_ref)
        o_ref[...] = jnp.dot(cur_ref[...], w_ref[...], preferred_element_type=F32).astype(o_ref.dtype)

    @pl.when(r == 0)
    def _():
        norm_slice(xa_ref)

    @pl.when(r % 2 == 1)
    def _():
        step(xa_ref, xb_ref)

    @pl.when((r > 0) & (r % 2 == 0))
    def _():
        step(xb_ref, xa_ref)


def norm_mm(a, g, w, *, tm, tn, out_dtype):
    m, k = a.shape
    n = w.shape[1]
    tm, tn = _tile(m, tm), _tile(n, tn)
    ni, nj = m // tm, n // tn
    n_slices = 1
    while n_slices * 2 <= nj and (tm // (n_slices * 2)) % NORM_CHUNK_ROWS == 0:
        n_slices *= 2
    slice_rows = tm // n_slices
    first = lambda r, j: jnp.where(r == 0, 0, j)
    return pl.pallas_call(
        functools.partial(_norm_mm_kernel, slice_rows=slice_rows, n_slices=n_slices),
        grid=(ni + 1, nj),
        in_specs=[pl.BlockSpec((slice_rows, k),
                               lambda r, j: (jnp.minimum(r, ni - 1) * n_slices + jnp.minimum(j, n_slices - 1), 0)),
                  pl.BlockSpec((1, k), lambda r, j: (0, 0)),
                  pl.BlockSpec((k, tn), lambda r, j: (0, first(r, j)))],
        out_specs=pl.BlockSpec((tm, tn), lambda r, j: (jnp.maximum(r - 1, 0), first(r, j))),
        out_shape=jax.ShapeDtypeStruct((m, n), out_dtype),
        scratch_shapes=[pltpu.VMEM((tm, k), BF16), pltpu.VMEM((tm, k), BF16)],
        compiler_params=_params("arbitrary", "arbitrary"),
        name="norm_mm",
    )(a, g.reshape(1, k).astype(F32), w)


def _mm_relu2_kernel(a_ref, w_ref, o_ref):
    acc = jnp.dot(a_ref[...], w_ref[...], preferred_element_type=F32)
    o_ref[...] = jnp.square(jnp.maximum(acc, 0.0)).astype(o_ref.dtype)


def mm_relu2(a, w, *, tm, tn, out_dtype):
    m, k = a.shape
    n = w.shape[1]
    tm, tn = _tile(m, tm), _tile(n, tn)
    return pl.pallas_call(
        _mm_relu2_kernel,
        grid=(m // tm, n // tn),
        in_specs=[pl.BlockSpec((tm, k), lambda i, j: (i, 0)),
                  pl.BlockSpec((k, tn), lambda i, j: (0, j))],
        out_specs=pl.BlockSpec((tm, tn), lambda i, j: (i, j)),
        out_shape=jax.ShapeDtypeStruct((m, n), out_dtype),
        compiler_params=_params("parallel", "parallel"),
        name="mm_relu2",
    )(a, w)


def _mm_res_kernel(a_ref, w_ref, res_ref, o_ref):
    @pl.when(pl.program_id(2) == 0)
    def _():
        o_ref[...] = res_ref[...]

    o_ref[...] += jnp.dot(a_ref[...], w_ref[...], preferred_element_type=F32)


def mm_res(a, w, res, *, tm, tn, tk):
    m, k = a.shape
    n = w.shape[1]
    tm, tn, tk = _tile(m, tm), _tile(n, tn), _tile(k, tk)
    return pl.pallas_call(
        _mm_res_kernel,
        grid=(m // tm, n // tn, k // tk),
        in_specs=[pl.BlockSpec((tm, tk), lambda i, j, kk: (i, kk)),
                  pl.BlockSpec((tk, tn), lambda i, j, kk: (kk, j)),
                  pl.BlockSpec((tm, tn), lambda i, j, kk: (i, j))],
        out_specs=pl.BlockSpec((tm, tn), lambda i, j, kk: (i, j)),
        out_shape=jax.ShapeDtypeStruct((m, n), F32),
        compiler_params=_params("parallel", "parallel", "arbitrary"),
        name="mm_res",
    )(a, w, res)


def _rmsnorm_kernel(a_ref, g_ref, o_ref):
    _rms_rows(a_ref, g_ref, o_ref)


def rmsnorm(a, g, *, tm, out_dtype):
    m, k = a.shape
    tm = _tile(m, tm)
    return pl.pallas_call(
        _rmsnorm_kernel,
        grid=(m // tm,),
        in_specs=[pl.BlockSpec((tm, k), lambda i: (i, 0)), pl.BlockSpec((1, k), lambda i: (0, 0))],
        out_specs=pl.BlockSpec((tm, k), lambda i: (i, 0)),
        out_shape=jax.ShapeDtypeStruct((m, k), out_dtype),
        compiler_params=_params("parallel"),
        name="rmsnorm",
    )(a, g.reshape(1, k).astype(F32))


def _fold_fourier_kernel(cs_ref, wf_ref, o_ref):
    o_ref[0] = jnp.dot(cs_ref[...], wf_ref[0], preferred_element_type=F32).astype(o_ref.dtype)


def fold_fourier_weights(cs_rows, wf):
    groups, gdim, _ = wf.shape
    return pl.pallas_call(
        _fold_fourier_kernel,
        grid=(groups,),
        in_specs=[pl.BlockSpec((2 * gdim, gdim), lambda g: (0, 0)),
                  pl.BlockSpec((1, gdim, gdim), lambda g: (g, 0, 0))],
        out_specs=pl.BlockSpec((1, 2 * gdim, gdim), lambda g: (g, 0, 0)),
        out_shape=jax.ShapeDtypeStruct((groups, 2 * gdim, gdim), BF16),
        compiler_params=_params("parallel"),
        name="fold_fourier_weights",
    )(cs_rows, wf)


def _seq_dft1_kernel(z_ref, t_ref, o_ref, *, nb, width):
    for t in range(nb):
        acc = jnp.dot(t_ref[t], z_ref[0, t], preferred_element_type=F32)
        o_ref[0, :, t * width:(t + 1) * width] = acc.astype(o_ref.dtype)


def seq_dft1(zt, tab, *, nb):
    b, s2, s1, width = zt.shape
    nb = _tile(s2, nb)
    return pl.pallas_call(
        functools.partial(_seq_dft1_kernel, nb=nb, width=width),
        grid=(b, s2 // nb),
        in_specs=[pl.BlockSpec((1, nb, s1, width), lambda bi, j: (bi, j, 0, 0)),
                  pl.BlockSpec((nb, 2 * s1, s1), lambda bi, j: (j, 0, 0))],
        out_specs=pl.BlockSpec((1, 2 * s1, nb * width), lambda bi, j: (bi, 0, j)),
        out_shape=jax.ShapeDtypeStruct((b, 2 * s1, s2 * width), BF16),
        compiler_params=_params("parallel", "parallel"),
        name="seq_dft1",
    )(zt, tab)


def _seq_dft2_kernel(re_ref, im_ref, m2_ref, wcs_ref, gf_ref, o_ref, *, kb, width):
    groups, _, gdim = wcs_ref.shape
    s2 = re_ref.shape[2]
    for t in range(kb):
        a = jnp.concatenate([re_ref[0, t], im_ref[0, t]], axis=0)
        y = jnp.dot(m2_ref[...], a, preferred_element_type=F32).astype(BF16)
        y_re, y_im = y[:s2], y[s2:]
        parts = []
        for g in range(groups):
            sl = slice(g * gdim, (g + 1) * gdim)
            lhs = jnp.concatenate([y_re[:, sl], y_im[:, sl]], axis=1)
            parts.append(jnp.dot(lhs, wcs_ref[g], preferred_element_type=F32))
        ms = sum(jnp.sum(p * p, axis=-1, keepdims=True) for p in parts) / width
        inv = lax.rsqrt(ms + NORM_EPS)
        for g in range(groups):
            sl = slice(g * gdim, (g + 1) * gdim)
            o_ref[0, :, t * width + g * gdim:t * width + (g + 1) * gdim] = (
                parts[g] * inv * gf_ref[:, sl]).astype(o_ref.dtype)


def seq_dft2(a4, m2, wcs, gf, *, s1, kb):
    b, _, s2, width = a4.shape
    kb = _tile(s1, kb)
    nblk = s1 // kb
    return pl.pallas_call(
        functools.partial(_seq_dft2_kernel, kb=kb, width=width),
        grid=(b, nblk),
        in_specs=[
            pl.BlockSpec((1, kb, s2, width), lambda bi, j: (bi, j, 0, 0)),
            pl.BlockSpec((1, kb, s2, width), lambda bi, j: (bi, j + nblk, 0, 0)),
            pl.BlockSpec((2 * s2, 2 * s2), lambda bi, j: (0, 0)),
            pl.BlockSpec(wcs.shape, lambda bi, j: (0, 0, 0)),
            pl.BlockSpec((1, width), lambda bi, j: (0, 0)),
        ],
        out_specs=pl.BlockSpec((1, s2, kb * width), lambda bi, j: (bi, 0, j)),
        out_shape=jax.ShapeDtypeStruct((b, s2, s1 * width), BF16),
        compiler_params=_params("parallel", "parallel"),
        name="seq_dft2",
    )(a4, a4, m2, wcs, gf.reshape(1, width).astype(F32))


def _dft_tables(seq, gdim):
    s2 = FOURIER_SEQ_INNER
    s1 = seq // s2
    c = np.arange(gdim)
    ang = 2.0 * np.pi * ((c[:, None] * c[None, :]) % gdim) / gdim
    cs_rows = np.concatenate([np.cos(ang), np.sin(ang)], axis=0) / math.sqrt(gdim)
    k1 = np.arange(s1)
    n1 = np.arange(s1)
    n2 = np.arange(s2)
    idx = (k1[None, :, None] * (s2 * n1[None, None, :] + n2[:, None, None])) % seq
    phi = 2.0 * np.pi * idx / seq
    tab1 = np.concatenate([np.cos(phi), -np.sin(phi)], axis=1)
    k2 = np.arange(s2)
    ang2 = 2.0 * np.pi * ((k2[:, None] * n2[None, :]) % s2) / s2
    c2 = np.cos(ang2) / math.sqrt(seq)
    s2m = np.sin(ang2) / math.sqrt(seq)
    m2 = np.block([[c2, s2m], [-s2m, c2]])
    as_bf16 = lambda x: jnp.asarray(x, dtype=F32).astype(BF16)
    return as_bf16(cs_rows), as_bf16(tab1), as_bf16(m2)


def _rope_tables(rt):
    quarter = LANES // 4
    lane = lax.broadcasted_iota(jnp.int32, rt.shape, 1)
    q0 = lane < quarter
    q1 = (lane >= quarter) & (lane < 2 * quarter)
    ct = jnp.where(q0, rt, 0.0) + jnp.where(q1, pltpu.roll(rt, quarter, 1), 0.0)
    sa = jnp.where(q1, rt, 0.0)
    sb = jnp.where(q0, -pltpu.roll(rt, LANES - quarter, 1), 0.0)
    return ct, sa, sb


def _rope128(r, ct, sa, sb):
    quarter = LANES // 4
    return r * ct + pltpu.roll(r, quarter, 1) * sa + pltpu.roll(r, LANES - quarter, 1) * sb


def _q_proj_kernel(c_ref, g_ref, w_ref, rt_ref, o_ref, *, heads, scale):
    xn = _rms(c_ref[...].astype(F32), g_ref[...]).astype(BF16)
    acc = jnp.dot(xn, w_ref[...], preferred_element_type=F32)
    ct, sa, sb = _rope_tables(rt_ref[...])
    for h in range(heads):
        base = 2 * LANES * h
        o_ref[:, base:base + LANES] = (acc[:, base:base + LANES] * scale).astype(o_ref.dtype)
        r = _rope128(acc[:, base + LANES:base + 2 * LANES], ct, sa, sb)
        o_ref[:, base + LANES:base + 2 * LANES] = (r * scale).astype(o_ref.dtype)


def q_proj(z, g, w, rt, *, col_block, rank, heads, scale, tm):
    m = z.shape[0]
    tm = _tile(m, tm)
    n = w.shape[1]
    return pl.pallas_call(
        functools.partial(_q_proj_kernel, heads=heads, scale=scale),
        grid=(m // tm,),
        in_specs=[pl.BlockSpec((tm, rank), lambda i: (i, col_block)),
                  pl.BlockSpec((1, rank), lambda i: (0, 0)),
                  pl.BlockSpec((rank, n), lambda i: (0, 0)),
                  pl.BlockSpec((tm, LANES), lambda i: (i, 0))],
        out_specs=pl.BlockSpec((tm, n), lambda i: (i, 0)),
        out_shape=jax.ShapeDtypeStruct((m, n), BF16),
        compiler_params=_params("parallel"),
        name="q_proj",
    )(z, g.reshape(1, rank).astype(F32), w, rt)


def _kv_proj_kernel(c_ref, kr_ref, g_ref, w_ref, rt_ref, k_ref, v_ref, *, heads):
    xn = _rms(c_ref[...].astype(F32), g_ref[...]).astype(BF16)
    acc = jnp.dot(xn, w_ref[...], preferred_element_type=F32)
    kr = _rope128(kr_ref[...].astype(F32), *_rope_tables(rt_ref[...])).astype(k_ref.dtype)
    for h in range(heads):
        k_ref[:, 2 * LANES * h:2 * LANES * h + LANES] = acc[:, LANES * h:LANES * (h + 1)].astype(k_ref.dtype)
        k_ref[:, 2 * LANES * h + LANES:2 * LANES * (h + 1)] = kr
        v_ref[:, 2 * LANES * h:2 * LANES * h + LANES] = acc[:, LANES * (heads + h):LANES * (heads + h + 1)].astype(v_ref.dtype)
        v_ref[:, 2 * LANES * h + LANES:2 * LANES * (h + 1)] = jnp.ones((acc.shape[0], LANES), v_ref.dtype)


def kv_proj(z, g, w, rt, *, ckv_block, kr_block, rank, heads, tm):
    m = z.shape[0]
    tm = _tile(m, tm)
    return pl.pallas_call(
        functools.partial(_kv_proj_kernel, heads=heads),
        grid=(m // tm,),
        in_specs=[pl.BlockSpec((tm, rank), lambda i: (i, ckv_block)),
                  pl.BlockSpec((tm, LANES), lambda i: (i, kr_block)),
                  pl.BlockSpec((1, rank), lambda i: (0, 0)),
                  pl.BlockSpec((rank, 2 * LANES * heads), lambda i: (0, 0)),
                  pl.BlockSpec((tm, LANES), lambda i: (i, 0))],
        out_specs=[pl.BlockSpec((tm, 2 * LANES * heads), lambda i: (i, 0))] * 2,
        out_shape=[jax.ShapeDtypeStruct((m, 2 * LANES * heads), BF16)] * 2,
        compiler_params=_params("parallel"),
        name="kv_proj",
    )(z, z, g.reshape(1, rank).astype(F32), w, rt)


def _flash_kernel(q_ref, k_ref, v_ref, *rest, tk, nk, ncast):
    cast_in, o_ref, cast_out = rest[:ncast], rest[ncast], rest[ncast + 1:2 * ncast + 1]
    s_refs = rest[2 * ncast + 1:]
    for src, dst in zip(cast_in, cast_out):
        dst[...] = src[...].astype(dst.dtype)
    q = q_ref[0]
    tq = q.shape[0]
    dv = o_ref.shape[-1]

    def chunk(c):
        start = c * tk
        return pl.ds(start if isinstance(start, int) else pl.multiple_of(start, tk), tk)

    def scores(c):
        return lax.dot_general(q, k_ref[0, chunk(c), :], (((1,), (1,)), ((), ())), preferred_element_type=F32)

    def fold(s_ref, c, m, acc):
        s = s_ref[...]
        m_new = jnp.maximum(m, jnp.max(s, axis=1, keepdims=True))
        p = jnp.exp2(s - m_new).astype(BF16)
        acc = jnp.exp2(m - m_new) * acc + jnp.dot(p, v_ref[0, chunk(c), :], preferred_element_type=F32)
        return m_new, acc

    depth = len(s_refs)
    for c in range(min(depth - 1, nk)):
        s_refs[c][...] = scores(c)
    m, acc = jnp.full((tq, 1), -jnp.inf, F32), jnp.zeros((tq, 2 * dv), F32)
    for c in range(nk):
        ahead = c + depth - 1
        if ahead < nk:
            s_refs[ahead % depth][...] = scores(ahead)
        m, acc = fold(s_refs[c % depth], c, m, acc)
    o_ref[0] = (acc[:, :dv] / acc[:, dv:]).astype(o_ref.dtype)


def flash_attention(q, k, v, *, heads, tq, tk, cast_weights=()):
    b, s, _ = q.shape
    dk = q.shape[-1] // heads
    dv2 = v.shape[-1] // heads
    dv = dv2 // 2
    tq, tk = _tile(s, tq), _tile(s, min(tk, s // 2))
    nk = s // tk
    nq = s // tq
    steps = b * heads * nq
    wspecs = []
    for w in cast_weights:
        rows = w.shape[0] // steps
        assert rows * steps == w.shape[0] and rows % 16 == 0, (w.shape, steps)
        wspecs.append(pl.BlockSpec((rows, w.shape[1]), lambda bi, h, i: ((bi * heads + h) * nq + i, 0)))
    wspecs_in = wspecs_out = wspecs
    wshapes = [jax.ShapeDtypeStruct(w.shape, BF16) for w in cast_weights]
    outs = pl.pallas_call(
        functools.partial(_flash_kernel, tk=tk, nk=nk, ncast=len(cast_weights)),
        grid=(b, heads, nq),
        in_specs=[pl.BlockSpec((1, tq, dk), lambda bi, h, i: (bi, i, h)),
                  pl.BlockSpec((1, s, dk), lambda bi, h, i: (bi, 0, h)),
                  pl.BlockSpec((1, s, dv2), lambda bi, h, i: (bi, 0, h))] + wspecs_in,
        out_specs=[pl.BlockSpec((1, tq, dv), lambda bi, h, i: (bi, i, h))] + wspecs_out,
        out_shape=[jax.ShapeDtypeStruct((b, s, heads * dv), BF16)] + wshapes,
        scratch_shapes=[pltpu.VMEM((tq, tk), F32)] * FLASH_SCORE_BUFFERS,
        compiler_params=_params("parallel", "parallel", "arbitrary"),
        name="flash_attention",
    )(q, k, v, *cast_weights)
    return outs[0], outs[1:]


def _out_proj_kernel(yf_ref, a_ref, ga_ref, w_ref, x_ref, o_ref, ya_ref):
    fw = yf_ref.shape[1]

    @pl.when(pl.program_id(1) == 0)
    def _():
        _rms_rows(a_ref, ga_ref, ya_ref)

    acc = jnp.dot(yf_ref[...], w_ref[:fw, :], preferred_element_type=F32)
    acc += jnp.dot(ya_ref[...], w_ref[fw:, :], preferred_element_type=F32)
    o_ref[...] = (acc + x_ref[...]).astype(o_ref.dtype)


def out_proj(yf, a, ga, w, x, *, tm, tn):
    m, fw = yf.shape
    aw = a.shape[1]
    n = w.shape[1]
    tm, tn = _tile(m, tm), _tile(n, tn)
    return pl.pallas_call(
        _out_proj_kernel,
        grid=(m // tm, n // tn),
        in_specs=[pl.BlockSpec((tm, fw), lambda i, j: (i, 0)),
                  pl.BlockSpec((tm, aw), lambda i, j: (i, 0)),
                  pl.BlockSpec((1, aw), lambda i, j: (0, 0)),
                  pl.BlockSpec((fw + aw, tn), lambda i, j: (0, j)),
                  pl.BlockSpec((tm, tn), lambda i, j: (i, j))],
        out_specs=pl.BlockSpec((tm, tn), lambda i, j: (i, j)),
        out_shape=jax.ShapeDtypeStruct((m, n), F32),
        scratch_shapes=[pltpu.VMEM((tm, aw), BF16)],
        compiler_params=_params("parallel", "arbitrary"),
        name="out_proj",
    )(yf, a, ga.reshape(1, aw).astype(F32), w, x)


def _absorb_qk_kernel(wq_ref, k_ref, o_ref, *, scale):
    acc = lax.dot_general(wq_ref[...], k_ref[...], (((1,), (1,)), ((), ())), preferred_element_type=F32)
    o_ref[0] = (acc * scale).astype(o_ref.dtype)


def absorb_qk(wq, kv, *, batch, heads, mem, hd, scale):
    d = wq.shape[0]
    return pl.pallas_call(
        functools.partial(_absorb_qk_kernel, scale=scale),
        grid=(heads, batch),
        in_specs=[pl.BlockSpec((d, hd), lambda h, b: (0, h)),
                  pl.BlockSpec((mem, hd), lambda h, b: (b, h))],
        out_specs=pl.BlockSpec((1, d, mem), lambda h, b: (b, 0, h)),
        out_shape=jax.ShapeDtypeStruct((batch, d, heads * mem), BF16),
        compiler_params=_params("parallel", "parallel"),
        name="absorb_qk",
    )(wq, kv)


def _absorb_vo_kernel(v_ref, wo_ref, o_ref):
    o_ref[0] = jnp.dot(v_ref[...], wo_ref[...], preferred_element_type=F32).astype(o_ref.dtype)


def absorb_vo(kv, wo, *, batch, heads, mem, hd, tn):
    d = wo.shape[1]
    tn = _tile(d, tn)
    return pl.pallas_call(
        _absorb_vo_kernel,
        grid=(heads, d // tn, batch),
        in_specs=[pl.BlockSpec((mem, hd), lambda h, j, b: (b, h)),
                  pl.BlockSpec((hd, tn), lambda h, j, b: (h, j))],
        out_specs=pl.BlockSpec((1, mem, tn), lambda h, j, b: (b, h, j)),
        out_shape=jax.ShapeDtypeStruct((batch, heads * mem, d), BF16),
        compiler_params=_params("parallel", "parallel", "parallel"),
        name="absorb_vo",
    )(kv, wo)


def _xattn_kernel(hs_ref, hp_ref, g_ref, wqk_ref, vo_ref, g2_ref, o_ref, on_ref, hn_ref, pa_ref, pb_ref,
                  *, heads, mem, nt):
    r = pl.program_id(0)
    rows = hs_ref.shape[0]
    chunks = [pl.ds(c * NORM_CHUNK_ROWS, NORM_CHUNK_ROWS) for c in range(rows // NORM_CHUNK_ROWS)]

    def stage1(p_ref):
        for rc in chunks:
            hn_ref[rc, :] = _rms(hs_ref[rc, :], g_ref[...]).astype(BF16)
        s = jnp.dot(hn_ref[...], wqk_ref[0], preferred_element_type=F32)
        for h in range(heads):
            seg = s[:, h * mem:(h + 1) * mem]
            e = jnp.exp(seg - jnp.max(seg, axis=-1, keepdims=True))
            p_ref[:, h * mem:(h + 1) * mem] = (e / jnp.sum(e, axis=-1, keepdims=True)).astype(BF16)

    def stage2(p_ref):
        o_ref[...] = jnp.dot(p_ref[...], vo_ref[0], preferred_element_type=F32) + hp_ref[...]
        for rc in chunks:
            on_ref[rc, :] = _rms(o_ref[rc, :], g2_ref[...]).astype(on_ref.dtype)

    @pl.when(r == 0)
    def _():
        stage1(pa_ref)

    @pl.when((r > 0) & (r < nt) & (r % 2 == 1))
    def _():
        stage1(pb_ref)
        stage2(pa_ref)

    @pl.when((r > 0) & (r < nt) & (r % 2 == 0))
    def _():
        stage1(pa_ref)
        stage2(pb_ref)

    @pl.when(r == nt)
    def _():
        stage2(pb_ref if (nt - 1) % 2 else pa_ref)


def xattn(h, g, wqk, vo, g_next, *, seq, heads, mem, tm):
    m, d = h.shape
    tm = _tile(seq, tm)
    per_b = seq // tm
    hm = heads * mem
    nt = m // tm
    once = pl.Buffered(1)
    s1_tile = lambda r: jnp.minimum(r, nt - 1)
    s2_tile = lambda r: jnp.maximum(r - 1, 0)
    return pl.pallas_call(
        functools.partial(_xattn_kernel, heads=heads, mem=mem, nt=nt),
        grid=(nt + 1,),
        in_specs=[pl.BlockSpec((tm, d), lambda r: (s1_tile(r), 0)),
                  pl.BlockSpec((tm, d), lambda r: (s2_tile(r), 0)),
                  pl.BlockSpec((1, d), lambda r: (0, 0)),
                  pl.BlockSpec((1, d, hm), lambda r: (s1_tile(r) // per_b, 0, 0), pipeline_mode=once),
                  pl.BlockSpec((1, hm, d), lambda r: (s2_tile(r) // per_b, 0, 0), pipeline_mode=once),
                  pl.BlockSpec((1, d), lambda r: (0, 0))],
        out_specs=[pl.BlockSpec((tm, d), lambda r: (s2_tile(r), 0))] * 2,
        out_shape=[jax.ShapeDtypeStruct((m, d), F32), jax.ShapeDtypeStruct((m, d), BF16)],
        scratch_shapes=[pltpu.VMEM((tm, d), BF16), pltpu.VMEM((tm, hm), BF16), pltpu.VMEM((tm, hm), BF16)],
        compiler_params=_params("arbitrary"),
        name="xattn",
    )(h, h, g.reshape(1, d).astype(F32), wqk, vo, g_next.reshape(1, d).astype(F32))


def _pad_cols(w, n):
    return jnp.pad(w, ((0, 0), (0, n - w.shape[1])))


def kernel(x, mem, positions, g_mix, w_in, w_fourier, g_q_lora, w_uq, g_kv_lora, w_ukv, g_fourier_out, g_mla_out, w_out, g_xattn, g_mem, w_xq, w_xk, w_xv, w_xo, g_mlp, w_ff1, w_ff2, g_final):
    batch, seq, d = x.shape
    depth = g_mix.shape[0]
    groups, gdim = w_fourier.shape[1], w_fourier.shape[2]
    fw = groups * gdim
    q_rank, heads, qk_dim = w_uq.shape[1], w_uq.shape[2], w_uq.shape[3]
    kv_rank = w_ukv.shape[1]
    nope = LANES
    rope = qk_dim - nope
    vdim = w_ukv.shape[3] - nope
    assert rope == LANES // 2 and vdim == LANES and fw % q_rank == 0
    assert (fw + q_rank) % kv_rank == 0 and (fw + q_rank + kv_rank) % LANES == 0
    xheads, xhd = w_xq.shape[2], w_xq.shape[3]
    mtok = mem.shape[1]
    t = batch * seq
    s2 = FOURIER_SEQ_INNER
    s1 = seq // s2

    half = rope // 2
    inv_freq = ROPE_THETA ** (-jnp.arange(half, dtype=F32) / half)
    ang = positions.astype(F32).reshape(t, 1) * inv_freq
    rt = jnp.concatenate([jnp.cos(ang), jnp.sin(ang), jnp.zeros((t, LANES - 2 * half), F32)], axis=1)

    cs_rows, tab1, m2 = _dft_tables(seq, gdim)

    in_width = w_in.shape[2]
    z_width = -(-(in_width + rope) // 768) * 768
    scale = float(qk_dim) ** -0.5 * math.log2(math.e)

    h = x.reshape(t, d)
    mem2 = mem.reshape(batch * mtok, d)
    for layer in range(depth):
        w_in_b = _pad_cols(w_in[layer], z_width).astype(BF16)
        wq = jnp.pad(w_uq[layer], ((0, 0), (0, 0), (0, 2 * LANES - qk_dim))).reshape(q_rank, heads * 2 * LANES).astype(BF16)
        wkv = jnp.concatenate([w_ukv[layer][:, :, :nope].reshape(kv_rank, heads * nope),
                               w_ukv[layer][:, :, nope:].reshape(kv_rank, heads * vdim)], axis=1).astype(BF16)
        wcs = fold_fourier_weights(cs_rows, w_fourier[layer].astype(BF16))
        wxq = w_xq[layer].reshape(d, xheads * xhd).astype(BF16)
        wxk = w_xk[layer].reshape(d, xheads * xhd).astype(BF16)
        wxv = w_xv[layer].reshape(d, xheads * xhd).astype(BF16)

        z = norm_mm(h, g_mix[layer], w_in_b, tm=1024, tn=768, out_dtype=BF16)
        zt = z[:, :fw].reshape(batch, s1, s2, fw).transpose(0, 2, 1, 3)
        a = seq_dft1(zt, tab1, nb=8)
        yf = seq_dft2(a.reshape(batch, 2 * s1, s2, fw), m2, wcs, g_fourier_out[layer],
                      s1=s1, kb=4).reshape(t, fw)
        qh = q_proj(z, g_q_lora[layer], wq, rt, col_block=fw // q_rank, rank=q_rank,
                    heads=heads, scale=scale, tm=512)
        kh, vh = kv_proj(z, g_kv_lora[layer], wkv, rt, ckv_block=(fw + q_rank) // kv_rank,
                         kr_block=(fw + q_rank + kv_rank) // LANES, rank=kv_rank, heads=heads, tm=512)
        o, (w_out_b, wxo, w1, w2) = flash_attention(
            qh.reshape(batch, seq, -1), kh.reshape(batch, seq, -1), vh.reshape(batch, seq, -1),
            heads=heads, tq=1024, tk=512,
            cast_weights=(w_out[layer], w_xo[layer].reshape(xheads * xhd, d), w_ff1[layer], w_ff2[layer]))
        o = o.reshape(t, heads * vdim)
        h = out_proj(yf, o, g_mla_out[layer], w_out_b, h, tm=1024, tn=512)

        km = norm_mm(mem2, g_mem[layer], wxk, tm=512, tn=512, out_dtype=BF16)
        vm = norm_mm(mem2, g_mem[layer], wxv, tm=512, tn=512, out_dtype=BF16)
        wqk = absorb_qk(wxq, km, batch=batch, heads=xheads, mem=mtok, hd=xhd, scale=float(xhd) ** -0.5)
        vo = absorb_vo(vm, wxo, batch=batch, heads=xheads, mem=mtok, hd=xhd, tn=2048)
        h, hn = xattn(h, g_xattn[layer], wqk, vo, g_mlp[layer], seq=seq, heads=xheads, mem=mtok, tm=256)

        act = mm_relu2(hn, w1, tm=1024, tn=1024, out_dtype=BF16)
        h = mm_res(act, w2, h, tm=1024, tn=1024, tk=4096)

    return rmsnorm(h, g_final, tm=256, out_dtype=x.dtype).reshape(batch, seq, d)
```

```python
import functools
import math

import numpy as np
import jax
import jax.numpy as jnp
from jax import lax
from jax.experimental import pallas as pl
from jax.experimental.pallas import tpu as pltpu

F32 = jnp.float32
BF16 = jnp.bfloat16

NORM_EPS = 1e-6
ROPE_THETA = 10000.0
LANES = 128
VMEM_LIMIT_BYTES = 56 * 1024 * 1024
FOURIER_SEQ_INNER = 128
FLASH_SCORE_BUFFERS = 2
SEQ_DFT1_POSITIONS = 16


def _params(*sem):
    return pltpu.CompilerParams(dimension_semantics=sem, vmem_limit_bytes=VMEM_LIMIT_BYTES)


def _rms(xf, g):
    ms = jnp.mean(xf * xf, axis=-1, keepdims=True)
    return xf * lax.rsqrt(ms + NORM_EPS) * g


def _rms_rows(a_ref, g_ref, o_ref, chunk=64):
    rows = a_ref.shape[0]
    chunk = min(chunk, rows)

    def body(c, carry):
        r = pl.ds(pl.multiple_of(c * chunk, chunk), chunk)
        o_ref[r, :] = _rms(a_ref[r, :].astype(F32), g_ref[...]).astype(o_ref.dtype)
        return carry

    lax.fori_loop(0, rows // chunk, body, 0)


def _tile(dim, pref):
    t = min(dim, pref)
    assert dim % t == 0, (dim, pref)
    return t


NORM_CHUNK_ROWS = 32


def _norm_mm_kernel(a_ref, g_ref, w_ref, o_ref, xa_ref, xb_ref, *, slice_rows, n_slices):
    r, j = pl.program_id(0), pl.program_id(1)

    def norm_slice(dst_ref):
        start = jnp.minimum(j, n_slices - 1) * slice_rows
        for c in range(slice_rows // NORM_CHUNK_ROWS):
            src = pl.ds(c * NORM_CHUNK_ROWS, NORM_CHUNK_ROWS)
            dst = pl.ds(pl.multiple_of(start + c * NORM_CHUNK_ROWS, NORM_CHUNK_ROWS), NORM_CHUNK_ROWS)
            dst_ref[dst, :] = _rms(a_ref[src, :].astype(F32), g_ref[...]).astype(BF16)

    def step(cur_ref, nxt_ref):
        norm_slice(nxt_ref)
        o_ref[...] = jnp.dot(cur_ref[...], w_ref[...], preferred_element_type=F32).astype(o_ref.dtype)

    @pl.when(r == 0)
    def _():
        norm_slice(xa_ref)

    @pl.when(r % 2 == 1)
    def _():
        step(xa_ref, xb_ref)

    @pl.when((r > 0) & (r % 2 == 0))
    def _():
        step(xb_ref, xa_ref)


def norm_mm(a, g, w, *, tm, tn, out_dtype):
    m, k = a.shape
    n = w.shape[1]
    tm, tn = _tile(m, tm), _tile(n, tn)
    ni, nj = m // tm, n // tn
    n_slices = 1
    while n_slices * 2 <= nj and (tm // (n_slices * 2)) % NORM_CHUNK_ROWS == 0:
        n_slices *= 2
    slice_rows = tm // n_slices
    first = lambda r, j: jnp.where(r == 0, 0, j)
    return pl.pallas_call(
        functools.partial(_norm_mm_kernel, slice_rows=slice_rows, n_slices=n_slices),
        grid=(ni + 1, nj),
        in_specs=[pl.BlockSpec((slice_rows, k),
                               lambda r, j: (jnp.minimum(r, ni - 1) * n_slices + jnp.minimum(j, n_slices - 1), 0)),
                  pl.BlockSpec((1, k), lambda r, j: (0, 0)),
                  pl.BlockSpec((k, tn), lambda r, j: (0, first(r, j)))],
        out_specs=pl.BlockSpec((tm, tn), lambda r, j: (jnp.maximum(r - 1, 0), first(r, j))),
        out_shape=jax.ShapeDtypeStruct((m, n), out_dtype),
        scratch_shapes=[pltpu.VMEM((tm, k), BF16), pltpu.VMEM((tm, k), BF16)],
        compiler_params=_params("arbitrary", "arbitrary"),
        name="norm_mm",
    )(a, g.reshape(1, k).astype(F32), w)


def _mm_relu2_kernel(a_ref, w_ref, o_ref):
    acc = jnp.dot(a_ref[...], w_ref[...], preferred_element_type=F32)
    o_ref[...] = jnp.square(jnp.maximum(acc, 0.0)).astype(o_ref.dtype)


def mm_relu2(a, w, *, tm, tn, out_dtype):
    m, k = a.shape
    n = w.shape[1]
    tm, tn = _tile(m, tm), _tile(n, tn)
    return pl.pallas_call(
        _mm_relu2_kernel,
        grid=(m // tm, n // tn),
        in_specs=[pl.BlockSpec((tm, k), lambda i, j: (i, 0)),
                  pl.BlockSpec((k, tn), lambda i, j: (0, j))],
        out_specs=pl.BlockSpec((tm, tn), lambda i, j: (i, j)),
        out_shape=jax.ShapeDtypeStruct((m, n), out_dtype),
        compiler_params=_params("parallel", "parallel"),
        name="mm_relu2",
    )(a, w)


def _mm_res_kernel(a_ref, w_ref, res_ref, o_ref):
    @pl.when(pl.program_id(2) == 0)
    def _():
        o_ref[...] = res_ref[...]

    o_ref[...] += jnp.dot(a_ref[...], w_ref[...], preferred_element_type=F32)


def mm_res(a, w, res, *, tm, tn, tk):
    m, k = a.shape
    n = w.shape[1]
    tm, tn, tk = _tile(m, tm), _tile(n, tn), _tile(k, tk)
    return pl.pallas_call(
        _mm_res_kernel,
        grid=(m // tm, n // tn, k // tk),
        in_specs=[pl.BlockSpec((tm, tk), lambda i, j, kk: (i, kk)),
                  pl.BlockSpec((tk, tn), lambda i, j, kk: (kk, j)),
                  pl.BlockSpec((tm, tn), lambda i, j, kk: (i, j))],
        out_specs=pl.BlockSpec((tm, tn), lambda i, j, kk: (i, j)),
        out_shape=jax.ShapeDtypeStruct((m, n), F32),
        compiler_params=_params("parallel", "parallel", "arbitrary"),
        name="mm_res",
    )(a, w, res)


def _rmsnorm_kernel(a_ref, g_ref, o_ref):
    _rms_rows(a_ref, g_ref, o_ref)


def rmsnorm(a, g, *, tm, out_dtype):
    m, k = a.shape
    tm = _tile(m, tm)
    return pl.pallas_call(
        _rmsnorm_kernel,
        grid=(m // tm,),
        in_specs=[pl.BlockSpec((tm, k), lambda i: (i, 0)), pl.BlockSpec((1, k), lambda i: (0, 0))],
        out_specs=pl.BlockSpec((tm, k), lambda i: (i, 0)),
        out_shape=jax.ShapeDtypeStruct((m, k), out_dtype),
        compiler_params=_params("parallel"),
        name="rmsnorm",
    )(a, g.reshape(1, k).astype(F32))


def _fold_fourier_kernel(cs_ref, wf_ref, o_ref):
    o_ref[0] = jnp.dot(cs_ref[...], wf_ref[0], preferred_element_type=F32).astype(o_ref.dtype)


def fold_fourier_weights(cs_rows, wf):
    groups, gdim, _ = wf.shape
    return pl.pallas_call(
        _fold_fourier_kernel,
        grid=(groups,),
        in_specs=[pl.BlockSpec((2 * gdim, gdim), lambda g: (0, 0)),
                  pl.BlockSpec((1, gdim, gdim), lambda g: (g, 0, 0))],
        out_specs=pl.BlockSpec((1, 2 * gdim, gdim), lambda g: (g, 0, 0)),
        out_shape=jax.ShapeDtypeStruct((groups, 2 * gdim, gdim), BF16),
        compiler_params=_params("parallel"),
        name="fold_fourier_weights",
    )(cs_rows, wf)


def _seq_dft1_kernel(z_ref, sel_ref, t_ref, o_ref, *, nb, width):
    s1 = z_ref.shape[0]
    zz = z_ref[...].reshape(s1 * nb, width)
    xx = jnp.dot(sel_ref[...], zz, preferred_element_type=F32).astype(BF16)
    for t in range(nb):
        acc = jnp.dot(t_ref[t], xx[t * s1:(t + 1) * s1], preferred_element_type=F32)
        o_ref[0, :, t * width:(t + 1) * width] = acc.astype(o_ref.dtype)


def seq_dft1(z4, sel, tab, *, width):
    b, s1, s2, _ = z4.shape
    nb = sel.shape[0] // s1
    assert s2 % nb == 0
    return pl.pallas_call(
        functools.partial(_seq_dft1_kernel, nb=nb, width=width),
        grid=(b, s2 // nb),
        in_specs=[pl.BlockSpec((None, s1, nb, width), lambda bi, j: (bi, 0, j, 0)),
                  pl.BlockSpec(sel.shape, lambda bi, j: (0, 0)),
                  pl.BlockSpec((nb, 2 * s1, s1), lambda bi, j: (j, 0, 0))],
        out_specs=pl.BlockSpec((1, 2 * s1, nb * width), lambda bi, j: (bi, 0, j)),
        out_shape=jax.ShapeDtypeStruct((b, 2 * s1, s2 * width), BF16),
        compiler_params=_params("parallel", "parallel"),
        name="seq_dft1",
    )(z4, sel, tab)


def _seq_dft2_kernel(re_ref, im_ref, m2_ref, wcs_ref, gf_ref, o_ref, *, kb, width):
    groups, _, gdim = wcs_ref.shape
    s2 = re_ref.shape[2]
    for t in range(kb):
        a = jnp.concatenate([re_ref[0, t], im_ref[0, t]], axis=0)
        y = jnp.dot(m2_ref[...], a, preferred_element_type=F32).astype(BF16)
        y_re, y_im = y[:s2], y[s2:]
        parts = []
        for g in range(groups):
            sl = slice(g * gdim, (g + 1) * gdim)
            lhs = jnp.concatenate([y_re[:, sl], y_im[:, sl]], axis=1)
            parts.append(jnp.dot(lhs, wcs_ref[g], preferred_element_type=F32))
        ms = sum(jnp.sum(p * p, axis=-1, keepdims=True) for p in parts) / width
        inv = lax.rsqrt(ms + NORM_EPS)
        for g in range(groups):
            sl = slice(g * gdim, (g + 1) * gdim)
            o_ref[0, :, t * width + g * gdim:t * width + (g + 1) * gdim] = (
                parts[g] * inv * gf_ref[:, sl]).astype(o_ref.dtype)


def seq_dft2(a4, m2, wcs, gf, *, s1, kb):
    b, _, s2, width = a4.shape
    kb = _tile(s1, kb)
    nblk = s1 // kb
    return pl.pallas_call(
        functools.partial(_seq_dft2_kernel, kb=kb, width=width),
        grid=(b, nblk),
        in_specs=[
            pl.BlockSpec((1, kb, s2, width), lambda bi, j: (bi, j, 0, 0)),
            pl.BlockSpec((1, kb, s2, width), lambda bi, j: (bi, j + nblk, 0, 0)),
            pl.BlockSpec((2 * s2, 2 * s2), lambda bi, j: (0, 0)),
            pl.BlockSpec(wcs.shape, lambda bi, j: (0, 0, 0)),
            pl.BlockSpec((1, width), lambda bi, j: (0, 0)),
        ],
        out_specs=pl.BlockSpec((1, s2, kb * width), lambda bi, j: (bi, 0, j)),
        out_shape=jax.ShapeDtypeStruct((b, s2, s1 * width), BF16),
        compiler_params=_params("parallel", "parallel"),
        name="seq_dft2",
    )(a4, a4, m2, wcs, gf.reshape(1, width).astype(F32))


def _dft_tables(seq, gdim):
    s2 = FOURIER_SEQ_INNER
    s1 = seq // s2
    c = np.arange(gdim)
    ang = 2.0 * np.pi * ((c[:, None] * c[None, :]) % gdim) / gdim
    cs_rows = np.concatenate([np.cos(ang), np.sin(ang)], axis=0) / math.sqrt(gdim)
    k1 = np.arange(s1)
    n1 = np.arange(s1)
    n2 = np.arange(s2)
    idx = (k1[None, :, None] * (s2 * n1[None, None, :] + n2[:, None, None])) % seq
    phi = 2.0 * np.pi * idx / seq
    tab1 = np.concatenate([np.cos(phi), -np.sin(phi)], axis=1)
    nb = SEQ_DFT1_POSITIONS
    sel = np.zeros((nb, s1, s1 * nb))
    for tt in range(nb):
        sel[tt, n1, n1 * nb + tt] = 1.0
    sel = sel.reshape(nb * s1, s1 * nb)
    k2 = np.arange(s2)
    ang2 = 2.0 * np.pi * ((k2[:, None] * n2[None, :]) % s2) / s2
    c2 = np.cos(ang2) / math.sqrt(seq)
    s2m = np.sin(ang2) / math.sqrt(seq)
    m2 = np.block([[c2, s2m], [-s2m, c2]])
    as_bf16 = lambda x: jnp.asarray(x, dtype=F32).astype(BF16)
    return as_bf16(cs_rows), as_bf16(sel), as_bf16(tab1), as_bf16(m2)


def _rope_tables(rt):
    quarter = LANES // 4
    lane = lax.broadcasted_iota(jnp.int32, rt.shape, 1)
    q0 = lane < quarter
    q1 = (lane >= quarter) & (lane < 2 * quarter)
    ct = jnp.where(q0, rt, 0.0) + jnp.where(q1, pltpu.roll(rt, quarter, 1), 0.0)
    sa = jnp.where(q1, rt, 0.0)
    sb = jnp.where(q0, -pltpu.roll(rt, LANES - quarter, 1), 0.0)
    return ct, sa, sb


def _rope128(r, ct, sa, sb):
    quarter = LANES // 4
    return r * ct + pltpu.roll(r, quarter, 1) * sa + pltpu.roll(r, LANES - quarter, 1) * sb


def _q_proj_kernel(c_ref, g_ref, w_ref, rt_ref, o_ref, *, heads):
    xn = _rms(c_ref[...].astype(F32), g_ref[...]).astype(BF16)
    acc = jnp.dot(xn, w_ref[...], preferred_element_type=F32)
    ct, sa, sb = _rope_tables(rt_ref[...])
    for h in range(heads):
        base = 2 * LANES * h
        o_ref[:, base:base + LANES] = acc[:, base:base + LANES].astype(o_ref.dtype)
        r = _rope128(acc[:, base + LANES:base + 2 * LANES], ct, sa, sb)
        o_ref[:, base + LANES:base + 2 * LANES] = r.astype(o_ref.dtype)


def q_proj(z, g, w, rt, *, col_block, rank, heads, tm):
    m = z.shape[0]
    tm = _tile(m, tm)
    n = w.shape[1]
    return pl.pallas_call(
        functools.partial(_q_proj_kernel, heads=heads),
        grid=(m // tm,),
        in_specs=[pl.BlockSpec((tm, rank), lambda i: (i, col_block)),
                  pl.BlockSpec((1, rank), lambda i: (0, 0)),
                  pl.BlockSpec((rank, n), lambda i: (0, 0)),
                  pl.BlockSpec((tm, LANES), lambda i: (i, 0))],
        out_specs=pl.BlockSpec((tm, n), lambda i: (i, 0)),
        out_shape=jax.ShapeDtypeStruct((m, n), BF16),
        compiler_params=_params("parallel"),
        name="q_proj",
    )(z, g.reshape(1, rank).astype(F32), w, rt)


def _kv_proj_kernel(c_ref, kr_ref, g_ref, w_ref, rt_ref, k_ref, v_ref, *, heads):
    xn = _rms(c_ref[...].astype(F32), g_ref[...]).astype(BF16)
    acc = jnp.dot(xn, w_ref[...], preferred_element_type=F32)
    kr = _rope128(kr_ref[...].astype(F32), *_rope_tables(rt_ref[...])).astype(k_ref.dtype)
    for h in range(heads):
        k_ref[:, 2 * LANES * h:2 * LANES * h + LANES] = acc[:, LANES * h:LANES * (h + 1)].astype(k_ref.dtype)
        k_ref[:, 2 * LANES * h + LANES:2 * LANES * (h + 1)] = kr
        v_ref[:, 2 * LANES * h:2 * LANES * h + LANES] = acc[:, LANES * (heads + h):LANES * (heads + h + 1)].astype(v_ref.dtype)
        v_ref[:, 2 * LANES * h + LANES:2 * LANES * (h + 1)] = jnp.ones((acc.shape[0], LANES), v_ref.dtype)


def kv_proj(z, g, w, rt, *, ckv_block, kr_block, rank, heads, tm):
    m = z.shape[0]
    tm = _tile(m, tm)
    return pl.pallas_call(
        functools.partial(_kv_proj_kernel, heads=heads),
        grid=(m // tm,),
        in_specs=[pl.BlockSpec((tm, rank), lambda i: (i, ckv_block)),
                  pl.BlockSpec((tm, LANES), lambda i: (i, kr_block)),
                  pl.BlockSpec((1, rank), lambda i: (0, 0)),
                  pl.BlockSpec((rank, 2 * LANES * heads), lambda i: (0, 0)),
                  pl.BlockSpec((tm, LANES), lambda i: (i, 0))],
        out_specs=[pl.BlockSpec((tm, 2 * LANES * heads), lambda i: (i, 0))] * 2,
        out_shape=[jax.ShapeDtypeStruct((m, 2 * LANES * heads), BF16)] * 2,
        compiler_params=_params("parallel"),
        name="kv_proj",
    )(z, z, g.reshape(1, rank).astype(F32), w, rt)


def _flash_kernel(q_ref, k_ref, v_ref, *rest, tk, nk, ncast):
    cast_in, o_ref, cast_out = rest[:ncast], rest[ncast], rest[ncast + 1:2 * ncast + 1]
    s_refs = rest[2 * ncast + 1:]
    for src, dst in zip(cast_in, cast_out):
        dst[...] = src[...].astype(dst.dtype)
    q = q_ref[0]
    tq = q.shape[0]
    dv = o_ref.shape[-1]

    def chunk(c):
        start = c * tk
        return pl.ds(start if isinstance(start, int) else pl.multiple_of(start, tk), tk)

    def scores(c):
        return lax.dot_general(q, k_ref[0, chunk(c), :], (((1,), (1,)), ((), ())), preferred_element_type=F32)

    def fold(s_ref, c, m, acc):
        s = s_ref[...]
        m_new = jnp.maximum(m, jnp.max(s, axis=1, keepdims=True))
        p = jnp.exp2(s - m_new).astype(BF16)
        acc = jnp.exp2(m - m_new) * acc + jnp.dot(p, v_ref[0, chunk(c), :], preferred_element_type=F32)
        return m_new, acc

    depth = len(s_refs)
    for c in range(min(depth - 1, nk)):
        s_refs[c][...] = scores(c)
    m, acc = jnp.full((tq, 1), -jnp.inf, F32), jnp.zeros((tq, 2 * dv), F32)
    for c in range(nk):
        ahead = c + depth - 1
        if ahead < nk:
            s_refs[ahead % depth][...] = scores(ahead)
        m, acc = fold(s_refs[c % depth], c, m, acc)
    o_ref[0] = (acc[:, :dv] / acc[:, dv:]).astype(o_ref.dtype)


def flash_attention(q, k, v, *, heads, tq, tk, cast_weights=()):
    b, s, _ = q.shape
    dk = q.shape[-1] // heads
    dv2 = v.shape[-1] // heads
    dv = dv2 // 2
    tq, tk = _tile(s, tq), _tile(s, min(tk, s // 2))
    nk = s // tk
    nq = s // tq
    steps = b * heads * nq
    wspecs = []
    for w in cast_weights:
        rows = w.shape[0] // steps
        assert rows * steps == w.shape[0] and rows % 16 == 0, (w.shape, steps)
        wspecs.append(pl.BlockSpec((rows, w.shape[1]), lambda bi, h, i: ((bi * heads + h) * nq + i, 0)))
    wspecs_in = wspecs_out = wspecs
    wshapes = [jax.ShapeDtypeStruct(w.shape, BF16) for w in cast_weights]
    outs = pl.pallas_call(
        functools.partial(_flash_kernel, tk=tk, nk=nk, ncast=len(cast_weights)),
        grid=(b, heads, nq),
        in_specs=[pl.BlockSpec((1, tq, dk), lambda bi, h, i: (bi, i, h)),
                  pl.BlockSpec((1, s, dk), lambda bi, h, i: (bi, 0, h)),
                  pl.BlockSpec((1, s, dv2), lambda bi, h, i: (bi, 0, h))] + wspecs_in,
        out_specs=[pl.BlockSpec((1, tq, dv), lambda bi, h, i: (bi, i, h))] + wspecs_out,
        out_shape=[jax.ShapeDtypeStruct((b, s, heads * dv), BF16)] + wshapes,
        scratch_shapes=[pltpu.VMEM((tq, tk), F32)] * FLASH_SCORE_BUFFERS,
        compiler_params=_params("parallel", "parallel", "arbitrary"),
        name="flash_attention",
    )(q, k, v, *cast_weights)
    return outs[0], outs[1:]


def _out_proj_kernel(yf_ref, a_ref, ga_ref, w_ref, x_ref, o_ref, ya_ref):
    fw = yf_ref.shape[1]

    @pl.when(pl.program_id(1) == 0)
    def _():
        _rms_rows(a_ref, ga_ref, ya_ref)

    acc = jnp.dot(yf_ref[...], w_ref[:fw, :], preferred_element_type=F32)
    acc += jnp.dot(ya_ref[...], w_ref[fw:, :], preferred_element_type=F32)
    o_ref[...] = (acc + x_ref[...]).astype(o_ref.dtype)


def out_proj(yf, a, ga, w, x, *, tm, tn):
    m, fw = yf.shape
    aw = a.shape[1]
    n = w.shape[1]
    tm, tn = _tile(m, tm), _tile(n, tn)
    return pl.pallas_call(
        _out_proj_kernel,
        grid=(m // tm, n // tn),
        in_specs=[pl.BlockSpec((tm, fw), lambda i, j: (i, 0)),
                  pl.BlockSpec((tm, aw), lambda i, j: (i, 0)),
                  pl.BlockSpec((1, aw), lambda i, j: (0, 0)),
                  pl.BlockSpec((fw + aw, tn), lambda i, j: (0, j)),
                  pl.BlockSpec((tm, tn), lambda i, j: (i, j))],
        out_specs=pl.BlockSpec((tm, tn), lambda i, j: (i, j)),
        out_shape=jax.ShapeDtypeStruct((m, n), F32),
        scratch_shapes=[pltpu.VMEM((tm, aw), BF16)],
        compiler_params=_params("parallel", "arbitrary"),
        name="out_proj",
    )(yf, a, ga.reshape(1, aw).astype(F32), w, x)


def _absorb_qk_kernel(wq_ref, k_ref, o_ref, *, scale):
    acc = lax.dot_general(wq_ref[...], k_ref[...], (((1,), (1,)), ((), ())), preferred_element_type=F32)
    o_ref[0] = (acc * scale).astype(o_ref.dtype)


def absorb_qk(wq, kv, *, batch, heads, mem, hd, scale):
    d = wq.shape[0]
    return pl.pallas_call(
        functools.partial(_absorb_qk_kernel, scale=scale),
        grid=(heads, batch),
        in_specs=[pl.BlockSpec((d, hd), lambda h, b: (0, h)),
                  pl.BlockSpec((mem, hd), lambda h, b: (b, h))],
        out_specs=pl.BlockSpec((1, d, mem), lambda h, b: (b, 0, h)),
        out_shape=jax.ShapeDtypeStruct((batch, d, heads * mem), BF16),
        compiler_params=_params("parallel", "parallel"),
        name="absorb_qk",
    )(wq, kv)


def _absorb_vo_kernel(v_ref, wo_ref, o_ref):
    o_ref[0] = jnp.dot(v_ref[...], wo_ref[...], preferred_element_type=F32).astype(o_ref.dtype)


def absorb_vo(kv, wo, *, batch, heads, mem, hd, tn):
    d = wo.shape[1]
    tn = _tile(d, tn)
    return pl.pallas_call(
        _absorb_vo_kernel,
        grid=(heads, d // tn, batch),
        in_specs=[pl.BlockSpec((mem, hd), lambda h, j, b: (b, h)),
                  pl.BlockSpec((hd, tn), lambda h, j, b: (h, j))],
        out_specs=pl.BlockSpec((1, mem, tn), lambda h, j, b: (b, h, j)),
        out_shape=jax.ShapeDtypeStruct((batch, heads * mem, d), BF16),
        compiler_params=_params("parallel", "parallel", "parallel"),
        name="absorb_vo",
    )(kv, wo)


def _xattn_kernel(hs_ref, hp_ref, g_ref, wqk_ref, vo_ref, g2_ref, o_ref, on_ref, hn_ref, pa_ref, pb_ref,
                  *, heads, mem, nt):
    r = pl.program_id(0)
    rows = hs_ref.shape[0]
    chunks = [pl.ds(c * NORM_CHUNK_ROWS, NORM_CHUNK_ROWS) for c in range(rows // NORM_CHUNK_ROWS)]

    def stage1(p_ref):
        for rc in chunks:
            hn_ref[rc, :] = _rms(hs_ref[rc, :], g_ref[...]).astype(BF16)
        s = jnp.dot(hn_ref[...], wqk_ref[0], preferred_element_type=F32)
        for h in range(heads):
            seg = s[:, h * mem:(h + 1) * mem]
            e = jnp.exp(seg - jnp.max(seg, axis=-1, keepdims=True))
            p_ref[:, h * mem:(h + 1) * mem] = (e / jnp.sum(e, axis=-1, keepdims=True)).astype(BF16)

    def stage2(p_ref):
        o_ref[...] = jnp.dot(p_ref[...], vo_ref[0], preferred_element_type=F32) + hp_ref[...]
        for rc in chunks:
            on_ref[rc, :] = _rms(o_ref[rc, :], g2_ref[...]).astype(on_ref.dtype)

    @pl.when(r == 0)
    def _():
        stage1(pa_ref)

    @pl.when((r > 0) & (r < nt) & (r % 2 == 1))
    def _():
        stage1(pb_ref)
        stage2(pa_ref)

    @pl.when((r > 0) & (r < nt) & (r % 2 == 0))
    def _():
        stage1(pa_ref)
        stage2(pb_ref)

    @pl.when(r == nt)
    def _():
        stage2(pb_ref if (nt - 1) % 2 else pa_ref)


def xattn(h, g, wqk, vo, g_next, *, seq, heads, mem, tm):
    m, d = h.shape
    tm = _tile(seq, tm)
    per_b = seq // tm
    hm = heads * mem
    nt = m // tm
    once = pl.Buffered(1)
    s1_tile = lambda r: jnp.minimum(r, nt - 1)
    s2_tile = lambda r: jnp.maximum(r - 1, 0)
    return pl.pallas_call(
        functools.partial(_xattn_kernel, heads=heads, mem=mem, nt=nt),
        grid=(nt + 1,),
        in_specs=[pl.BlockSpec((tm, d), lambda r: (s1_tile(r), 0)),
                  pl.BlockSpec((tm, d), lambda r: (s2_tile(r), 0)),
                  pl.BlockSpec((1, d), lambda r: (0, 0)),
                  pl.BlockSpec((1, d, hm), lambda r: (s1_tile(r) // per_b, 0, 0), pipeline_mode=once),
                  pl.BlockSpec((1, hm, d), lambda r: (s2_tile(r) // per_b, 0, 0), pipeline_mode=once),
                  pl.BlockSpec((1, d), lambda r: (0, 0))],
        out_specs=[pl.BlockSpec((tm, d), lambda r: (s2_tile(r), 0))] * 2,
        out_shape=[jax.ShapeDtypeStruct((m, d), F32), jax.ShapeDtypeStruct((m, d), BF16)],
        scratch_shapes=[pltpu.VMEM((tm, d), BF16), pltpu.VMEM((tm, hm), BF16), pltpu.VMEM((tm, hm), BF16)],
        compiler_params=_params("arbitrary"),
        name="xattn",
    )(h, h, g.reshape(1, d).astype(F32), wqk, vo, g_next.reshape(1, d).astype(F32))


def _pad_cols(w, n):
    return jnp.pad(w, ((0, 0), (0, n - w.shape[1])))


def kernel(x, mem, positions, g_mix, w_in, w_fourier, g_q_lora, w_uq, g_kv_lora, w_ukv, g_fourier_out, g_mla_out, w_out, g_xattn, g_mem, w_xq, w_xk, w_xv, w_xo, g_mlp, w_ff1, w_ff2, g_final):
    batch, seq, d = x.shape
    depth = g_mix.shape[0]
    groups, gdim = w_fourier.shape[1], w_fourier.shape[2]
    fw = groups * gdim
    q_rank, heads, qk_dim = w_uq.shape[1], w_uq.shape[2], w_uq.shape[3]
    kv_rank = w_ukv.shape[1]
    nope = LANES
    rope = qk_dim - nope
    vdim = w_ukv.shape[3] - nope
    assert rope == LANES // 2 and vdim == LANES and fw % q_rank == 0
    assert (fw + q_rank) % kv_rank == 0 and (fw + q_rank + kv_rank) % LANES == 0
    xheads, xhd = w_xq.shape[2], w_xq.shape[3]
    mtok = mem.shape[1]
    t = batch * seq
    s2 = FOURIER_SEQ_INNER
    s1 = seq // s2

    half = rope // 2
    inv_freq = ROPE_THETA ** (-jnp.arange(half, dtype=F32) / half)
    ang = positions.astype(F32).reshape(t, 1) * inv_freq
    rt = jnp.concatenate([jnp.cos(ang), jnp.sin(ang), jnp.zeros((t, LANES - 2 * half), F32)], axis=1)

    cs_rows, sel, tab1, m2 = _dft_tables(seq, gdim)

    in_width = w_in.shape[2]
    z_width = -(-(in_width + rope) // 768) * 768
    scale = float(qk_dim) ** -0.5 * math.log2(math.e)

    h = x.reshape(t, d)
    mem2 = mem.reshape(batch * mtok, d)
    for layer in range(depth):
        w_in_b = _pad_cols(w_in[layer].astype(BF16), z_width)
        wq = (jnp.pad(w_uq[layer], ((0, 0), (0, 0), (0, 2 * LANES - qk_dim))) * scale).reshape(
            q_rank, heads * 2 * LANES).astype(BF16)
        wkv = jnp.concatenate([w_ukv[layer][:, :, :nope].reshape(kv_rank, heads * nope),
                               w_ukv[layer][:, :, nope:].reshape(kv_rank, heads * vdim)], axis=1).astype(BF16)
        wcs = fold_fourier_weights(cs_rows, w_fourier[layer].astype(BF16))
        wxq = w_xq[layer].reshape(d, xheads * xhd).astype(BF16)
        wxk = w_xk[layer].reshape(d, xheads * xhd).astype(BF16)
        wxv = w_xv[layer].reshape(d, xheads * xhd).astype(BF16)

        z = norm_mm(h, g_mix[layer], w_in_b, tm=1024, tn=768, out_dtype=BF16)
        a = seq_dft1(z.reshape(batch, s1, s2, z_width), sel, tab1, width=fw)
        yf = seq_dft2(a.reshape(batch, 2 * s1, s2, fw), m2, wcs, g_fourier_out[layer],
                      s1=s1, kb=4).reshape(t, fw)
        qh = q_proj(z, g_q_lora[layer], wq, rt, col_block=fw // q_rank, rank=q_rank,
                    heads=heads, tm=512)
        kh, vh = kv_proj(z, g_kv_lora[layer], wkv, rt, ckv_block=(fw + q_rank) // kv_rank,
                         kr_block=(fw + q_rank + kv_rank) // LANES, rank=kv_rank, heads=heads, tm=1024)
        o, (w_out_b, wxo, w1, w2) = flash_attention(
            qh.reshape(batch, seq, -1), kh.reshape(batch, seq, -1), vh.reshape(batch, seq, -1),
            heads=heads, tq=1024, tk=512,
            cast_weights=(w_out[layer], w_xo[layer].reshape(xheads * xhd, d), w_ff1[layer], w_ff2[layer]))
        o = o.reshape(t, heads * vdim)
        h = out_proj(yf, o, g_mla_out[layer], w_out_b, h, tm=1024, tn=512)

        km = norm_mm(mem2, g_mem[layer], wxk, tm=512, tn=512, out_dtype=BF16)
        vm = norm_mm(mem2, g_mem[layer], wxv, tm=512, tn=512, out_dtype=BF16)
        wqk = absorb_qk(wxq, km, batch=batch, heads=xheads, mem=mtok, hd=xhd, scale=float(xhd) ** -0.5)
        vo = absorb_vo(vm, wxo, batch=batch, heads=xheads, mem=mtok, hd=xhd, tn=2048)
        h, hn = xattn(h, g_xattn[layer], wqk, vo, g_mlp[layer], seq=seq, heads=xheads, mem=mtok, tm=256)

        act = mm_relu2(hn, w1, tm=1024, tn=1024, out_dtype=BF16)
        h = mm_res(act, w2, h, tm=1024, tn=1024, tk=4096)

    return rmsnorm(h, g_final, tm=256, out_dtype=x.dtype).reshape(batch, seq, d)
```

```python
import functools
import math

import numpy as np
import jax
import jax.numpy as jnp
from jax import lax
from jax.experimental import pallas as pl
from jax.experimental.pallas import tpu as pltpu

F32 = jnp.float32
BF16 = jnp.bfloat16

NORM_EPS = 1e-6
ROPE_THETA = 10000.0
LANES = 128
VMEM_LIMIT_BYTES = 56 * 1024 * 1024
FOURIER_SEQ_INNER = 128
FLASH_SCORE_BUFFERS = 2
SEQ_DFT1_POSITIONS = 16


def _params(*sem):
    return pltpu.CompilerParams(dimension_semantics=sem, vmem_limit_bytes=VMEM_LIMIT_BYTES)


def _rms(xf, g):
    ms = jnp.mean(xf * xf, axis=-1, keepdims=True)
    return xf * lax.rsqrt(ms + NORM_EPS) * g


def _rms_rows(a_ref, g_ref, o_ref, chunk=64):
    rows = a_ref.shape[0]
    chunk = min(chunk, rows)

    def body(c, carry):
        r = pl.ds(pl.multiple_of(c * chunk, chunk), chunk)
        o_ref[r, :] = _rms(a_ref[r, :].astype(F32), g_ref[...]).astype(o_ref.dtype)
        return carry

    lax.fori_loop(0, rows // chunk, body, 0)


def _tile(dim, pref):
    t = min(dim, pref)
    assert dim % t == 0, (dim, pref)
    return t


NORM_CHUNK_ROWS = 32


def _norm_mm_kernel(a_ref, g_ref, w_ref, o_ref, xa_ref, xb_ref, *, slice_rows, n_slices):
    r, j = pl.program_id(0), pl.program_id(1)

    def norm_slice(dst_ref):
        start = jnp.minimum(j, n_slices - 1) * slice_rows
        for c in range(slice_rows // NORM_CHUNK_ROWS):
            src = pl.ds(c * NORM_CHUNK_ROWS, NORM_CHUNK_ROWS)
            dst = pl.ds(pl.multiple_of(start + c * NORM_CHUNK_ROWS, NORM_CHUNK_ROWS), NORM_CHUNK_ROWS)
            dst_ref[dst, :] = _rms(a_ref[src, :].astype(F32), g_ref[...]).astype(BF16)

    def step(cur_ref, nxt_ref):
        norm_slice(nxt_ref)
        o_ref[...] = jnp.dot(cur_ref[...], w_ref[...], preferred_element_type=F32).astype(o_ref.dtype)

    @pl.when(r == 0)
    def _():
        norm_slice(xa_ref)

    @pl.when(r % 2 == 1)
    def _():
        step(xa_ref, xb_ref)

    @pl.when((r > 0) & (r % 2 == 0))
    def _():
        step(xb_ref, xa_ref)


def norm_mm(a, g, w, *, tm, tn, out_dtype):
    m, k = a.shape
    n = w.shape[1]
    tm, tn = _tile(m, tm), _tile(n, tn)
    ni, nj = m // tm, n // tn
    n_slices = 1
    while n_slices * 2 <= nj and (tm // (n_slices * 2)) % NORM_CHUNK_ROWS == 0:
        n_slices *= 2
    slice_rows = tm // n_slices
    first = lambda r, j: jnp.where(r == 0, 0, j)
    return pl.pallas_call(
        functools.partial(_norm_mm_kernel, slice_rows=slice_rows, n_slices=n_slices),
        grid=(ni + 1, nj),
        in_specs=[pl.BlockSpec((slice_rows, k),
                               lambda r, j: (jnp.minimum(r, ni - 1) * n_slices + jnp.minimum(j, n_slices - 1), 0)),
                  pl.BlockSpec((1, k), lambda r, j: (0, 0)),
                  pl.BlockSpec((k, tn), lambda r, j: (0, first(r, j)))],
        out_specs=pl.BlockSpec((tm, tn), lambda r, j: (jnp.maximum(r - 1, 0), first(r, j))),
        out_shape=jax.ShapeDtypeStruct((m, n), out_dtype),
        scratch_shapes=[pltpu.VMEM((tm, k), BF16), pltpu.VMEM((tm, k), BF16)],
        compiler_params=_params("arbitrary", "arbitrary"),
        name="norm_mm",
    )(a, g.reshape(1, k).astype(F32), w)


def _mm_relu2_kernel(a_ref, w_ref, o_ref):
    acc = jnp.dot(a_ref[...], w_ref[...], preferred_element_type=F32)
    o_ref[...] = jnp.square(jnp.maximum(acc, 0.0)).astype(o_ref.dtype)


def mm_relu2(a, w, *, tm, tn, out_dtype):
    m, k = a.shape
    n = w.shape[1]
    tm, tn = _tile(m, tm), _tile(n, tn)
    return pl.pallas_call(
        _mm_relu2_kernel,
        grid=(m // tm, n // tn),
        in_specs=[pl.BlockSpec((tm, k), lambda i, j: (i, 0)),
                  pl.BlockSpec((k, tn), lambda i, j: (0, j))],
        out_specs=pl.BlockSpec((tm, tn), lambda i, j: (i, j)),
        out_shape=jax.ShapeDtypeStruct((m, n), out_dtype),
        compiler_params=_params("parallel", "parallel"),
        name="mm_relu2",
    )(a, w)


def _mm_res_kernel(a_ref, w_ref, res_ref, o_ref):
    @pl.when(pl.program_id(2) == 0)
    def _():
        o_ref[...] = res_ref[...]

    o_ref[...] += jnp.dot(a_ref[...], w_ref[...], preferred_element_type=F32)


def mm_res(a, w, res, *, tm, tn, tk):
    m, k = a.shape
    n = w.shape[1]
    tm, tn, tk = _tile(m, tm), _tile(n, tn), _tile(k, tk)
    return pl.pallas_call(
        _mm_res_kernel,
        grid=(m // tm, n // tn, k // tk),
        in_specs=[pl.BlockSpec((tm, tk), lambda i, j, kk: (i, kk)),
                  pl.BlockSpec((tk, tn), lambda i, j, kk: (kk, j)),
                  pl.BlockSpec((tm, tn), lambda i, j, kk: (i, j))],
        out_specs=pl.BlockSpec((tm, tn), lambda i, j, kk: (i, j)),
        out_shape=jax.ShapeDtypeStruct((m, n), F32),
        compiler_params=_params("parallel", "parallel", "arbitrary"),
        name="mm_res",
    )(a, w, res)


def _rmsnorm_kernel(a_ref, g_ref, o_ref):
    _rms_rows(a_ref, g_ref, o_ref)


def rmsnorm(a, g, *, tm, out_dtype):
    m, k = a.shape
    tm = _tile(m, tm)
    return pl.pallas_call(
        _rmsnorm_kernel,
        grid=(m // tm,),
        in_specs=[pl.BlockSpec((tm, k), lambda i: (i, 0)), pl.BlockSpec((1, k), lambda i: (0, 0))],
        out_specs=pl.BlockSpec((tm, k), lambda i: (i, 0)),
        out_shape=jax.ShapeDtypeStruct((m, k), out_dtype),
        compiler_params=_params("parallel"),
        name="rmsnorm",
    )(a, g.reshape(1, k).astype(F32))


def _fold_fourier_kernel(cs_ref, wf_ref, o_ref):
    o_ref[0] = jnp.dot(cs_ref[...], wf_ref[0], preferred_element_type=F32).astype(o_ref.dtype)


def fold_fourier_weights(cs_rows, wf):
    groups, gdim, _ = wf.shape
    return pl.pallas_call(
        _fold_fourier_kernel,
        grid=(groups,),
        in_specs=[pl.BlockSpec((2 * gdim, gdim), lambda g: (0, 0)),
                  pl.BlockSpec((1, gdim, gdim), lambda g: (g, 0, 0))],
        out_specs=pl.BlockSpec((1, 2 * gdim, gdim), lambda g: (g, 0, 0)),
        out_shape=jax.ShapeDtypeStruct((groups, 2 * gdim, gdim), BF16),
        compiler_params=_params("parallel"),
        name="fold_fourier_weights",
    )(cs_rows, wf)


def _seq_dft1_kernel(z_ref, sel_ref, t_ref, o_ref, *, nb, width):
    s1 = z_ref.shape[0]
    zz = z_ref[...].reshape(s1 * nb, width)
    xx = jnp.dot(sel_ref[...], zz, preferred_element_type=F32).astype(BF16)
    for t in range(nb):
        acc = jnp.dot(t_ref[t], xx[t * s1:(t + 1) * s1], preferred_element_type=F32)
        o_ref[0, :, t * width:(t + 1) * width] = acc.astype(o_ref.dtype)


def seq_dft1(z4, sel, tab, *, width):
    b, s1, s2, _ = z4.shape
    nb = sel.shape[0] // s1
    assert s2 % nb == 0
    return pl.pallas_call(
        functools.partial(_seq_dft1_kernel, nb=nb, width=width),
        grid=(b, s2 // nb),
        in_specs=[pl.BlockSpec((None, s1, nb, width), lambda bi, j: (bi, 0, j, 0)),
                  pl.BlockSpec(sel.shape, lambda bi, j: (0, 0)),
                  pl.BlockSpec((nb, 2 * s1, s1), lambda bi, j: (j, 0, 0))],
        out_specs=pl.BlockSpec((1, 2 * s1, nb * width), lambda bi, j: (bi, 0, j)),
        out_shape=jax.ShapeDtypeStruct((b, 2 * s1, s2 * width), BF16),
        compiler_params=_params("parallel", "parallel"),
        name="seq_dft1",
    )(z4, sel, tab)


def _seq_dft2_kernel(re_ref, im_ref, m2_ref, wcs_ref, gf_ref, o_ref, *, kb, width):
    groups, _, gdim = wcs_ref.shape
    s2 = re_ref.shape[2]
    for t in range(kb):
        a = jnp.concatenate([re_ref[0, t], im_ref[0, t]], axis=0)
        y = jnp.dot(m2_ref[...], a, preferred_element_type=F32).astype(BF16)
        y_re, y_im = y[:s2], y[s2:]
        parts = []
        for g in range(groups):
            sl = slice(g * gdim, (g + 1) * gdim)
            lhs = jnp.concatenate([y_re[:, sl], y_im[:, sl]], axis=1)
            parts.append(jnp.dot(lhs, wcs_ref[g], preferred_element_type=F32))
        ms = sum(jnp.sum(p * p, axis=-1, keepdims=True) for p in parts) / width
        inv = lax.rsqrt(ms + NORM_EPS)
        for g in range(groups):
            sl = slice(g * gdim, (g + 1) * gdim)
            o_ref[0, :, t * width + g * gdim:t * width + (g + 1) * gdim] = (
                parts[g] * inv * gf_ref[:, sl]).astype(o_ref.dtype)


def seq_dft2(a4, m2, wcs, gf, *, s1, kb):
    b, _, s2, width = a4.shape
    kb = _tile(s1, kb)
    nblk = s1 // kb
    return pl.pallas_call(
        functools.partial(_seq_dft2_kernel, kb=kb, width=width),
        grid=(b, nblk),
        in_specs=[
            pl.BlockSpec((1, kb, s2, width), lambda bi, j: (bi, j, 0, 0)),
            pl.BlockSpec((1, kb, s2, width), lambda bi, j: (bi, j + nblk, 0, 0)),
            pl.BlockSpec((2 * s2, 2 * s2), lambda bi, j: (0, 0)),
            pl.BlockSpec(wcs.shape, lambda bi, j: (0, 0, 0)),
            pl.BlockSpec((1, width), lambda bi, j: (0, 0)),
        ],
        out_specs=pl.BlockSpec((1, s2, kb * width), lambda bi, j: (bi, 0, j)),
        out_shape=jax.ShapeDtypeStruct((b, s2, s1 * width), BF16),
        compiler_params=_params("parallel", "parallel"),
        name="seq_dft2",
    )(a4, a4, m2, wcs, gf.reshape(1, width).astype(F32))


def _dft_tables(seq, gdim):
    s2 = FOURIER_SEQ_INNER
    s1 = seq // s2
    c = np.arange(gdim)
    ang = 2.0 * np.pi * ((c[:, None] * c[None, :]) % gdim) / gdim
    cs_rows = np.concatenate([np.cos(ang), np.sin(ang)], axis=0) / math.sqrt(gdim)
    k1 = np.arange(s1)
    n1 = np.arange(s1)
    n2 = np.arange(s2)
    idx = (k1[None, :, None] * (s2 * n1[None, None, :] + n2[:, None, None])) % seq
    phi = 2.0 * np.pi * idx / seq
    tab1 = np.concatenate([np.cos(phi), -np.sin(phi)], axis=1)
    nb = SEQ_DFT1_POSITIONS
    sel = np.zeros((nb, s1, s1 * nb))
    for tt in range(nb):
        sel[tt, n1, n1 * nb + tt] = 1.0
    sel = sel.reshape(nb * s1, s1 * nb)
    k2 = np.arange(s2)
    ang2 = 2.0 * np.pi * ((k2[:, None] * n2[None, :]) % s2) / s2
    c2 = np.cos(ang2) / math.sqrt(seq)
    s2m = np.sin(ang2) / math.sqrt(seq)
    m2 = np.block([[c2, s2m], [-s2m, c2]])
    as_bf16 = lambda x: jnp.asarray(x, dtype=F32).astype(BF16)
    return as_bf16(cs_rows), as_bf16(sel), as_bf16(tab1), as_bf16(m2)


def _rope_tables(rt):
    quarter = LANES // 4
    lane = lax.broadcasted_iota(jnp.int32, rt.shape, 1)
    q0 = lane < quarter
    q1 = (lane >= quarter) & (lane < 2 * quarter)
    ct = jnp.where(q0, rt, 0.0) + jnp.where(q1, pltpu.roll(rt, quarter, 1), 0.0)
    sa = jnp.where(q1, rt, 0.0)
    sb = jnp.where(q0, -pltpu.roll(rt, LANES - quarter, 1), 0.0)
    return ct, sa, sb


def _rope128(r, ct, sa, sb):
    quarter = LANES // 4
    return r * ct + pltpu.roll(r, quarter, 1) * sa + pltpu.roll(r, LANES - quarter, 1) * sb


def _q_proj_kernel(c_ref, g_ref, w_ref, rt_ref, o_ref, *, heads):
    xn = _rms(c_ref[...].astype(F32), g_ref[...]).astype(BF16)
    acc = jnp.dot(xn, w_ref[...], preferred_element_type=F32)
    ct, sa, sb = _rope_tables(rt_ref[...])
    for h in range(heads):
        base = 2 * LANES * h
        o_ref[:, base:base + LANES] = acc[:, base:base + LANES].astype(o_ref.dtype)
        r = _rope128(acc[:, base + LANES:base + 2 * LANES], ct, sa, sb)
        o_ref[:, base + LANES:base + 2 * LANES] = r.astype(o_ref.dtype)


def q_proj(z, g, w, rt, *, col_block, rank, heads, tm):
    m = z.shape[0]
    tm = _tile(m, tm)
    n = w.shape[1]
    return pl.pallas_call(
        functools.partial(_q_proj_kernel, heads=heads),
        grid=(m // tm,),
        in_specs=[pl.BlockSpec((tm, rank), lambda i: (i, col_block)),
                  pl.BlockSpec((1, rank), lambda i: (0, 0)),
                  pl.BlockSpec((rank, n), lambda i: (0, 0)),
                  pl.BlockSpec((tm, LANES), lambda i: (i, 0))],
        out_specs=pl.BlockSpec((tm, n), lambda i: (i, 0)),
        out_shape=jax.ShapeDtypeStruct((m, n), BF16),
        compiler_params=_params("parallel"),
        name="q_proj",
    )(z, g.reshape(1, rank).astype(F32), w, rt)


def _kv_proj_kernel(c_ref, kr_ref, g_ref, w_ref, rt_ref, k_ref, v_ref, *, heads):
    xn = _rms(c_ref[...].astype(F32), g_ref[...]).astype(BF16)
    acc = jnp.dot(xn, w_ref[...], preferred_element_type=F32)
    kr = _rope128(kr_ref[...].astype(F32), *_rope_tables(rt_ref[...])).astype(k_ref.dtype)
    for h in range(heads):
        k_ref[:, 2 * LANES * h:2 * LANES * h + LANES] = acc[:, LANES * h:LANES * (h + 1)].astype(k_ref.dtype)
        k_ref[:, 2 * LANES * h + LANES:2 * LANES * (h + 1)] = kr
        v_ref[:, 2 * LANES * h:2 * LANES * h + LANES] = acc[:, LANES * (heads + h):LANES * (heads + h + 1)].astype(v_ref.dtype)
        v_ref[:, 2 * LANES * h + LANES:2 * LANES * (h + 1)] = jnp.ones((acc.shape[0], LANES), v_ref.dtype)


def kv_proj(z, g, w, rt, *, ckv_block, kr_block, rank, heads, tm):
    m = z.shape[0]
    tm = _tile(m, tm)
    return pl.pallas_call(
        functools.partial(_kv_proj_kernel, heads=heads),
        grid=(m // tm,),
        in_specs=[pl.BlockSpec((tm, rank), lambda i: (i, ckv_block)),
                  pl.BlockSpec((tm, LANES), lambda i: (i, kr_block)),
                  pl.BlockSpec((1, rank), lambda i: (0, 0)),
                  pl.BlockSpec((rank, 2 * LANES * heads), lambda i: (0, 0)),
                  pl.BlockSpec((tm, LANES), lambda i: (i, 0))],
        out_specs=[pl.BlockSpec((tm, 2 * LANES * heads), lambda i: (i, 0))] * 2,
        out_shape=[jax.ShapeDtypeStruct((m, 2 * LANES * heads), BF16)] * 2,
        compiler_params=_params("parallel"),
        name="kv_proj",
    )(z, z, g.reshape(1, rank).astype(F32), w, rt)


def _flash_kernel(q_ref, k_ref, v_ref, *rest, tk, nk, ncast):
    cast_in, o_ref, cast_out = rest[:ncast], rest[ncast], rest[ncast + 1:2 * ncast + 1]
    s_refs = rest[2 * ncast + 1:]
    for src, dst in zip(cast_in, cast_out):
        dst[...] = src[...].astype(dst.dtype)
    q = q_ref[0]
    tq = q.shape[0]
    dv = o_ref.shape[-1]

    def chunk(c):
        start = c * tk
        return pl.ds(start if isinstance(start, int) else pl.multiple_of(start, tk), tk)

    def scores(c):
        return lax.dot_general(q, k_ref[0, chunk(c), :], (((1,), (1,)), ((), ())), preferred_element_type=F32)

    def fold(s_ref, c, m, acc):
        s = s_ref[...]
        m_new = jnp.maximum(m, jnp.max(s, axis=1, keepdims=True))
        p = jnp.exp2(s - m_new).astype(BF16)
        acc = jnp.exp2(m - m_new) * acc + jnp.dot(p, v_ref[0, chunk(c), :], preferred_element_type=F32)
        return m_new, acc

    depth = len(s_refs)
    for c in range(min(depth - 1, nk)):
        s_refs[c][...] = scores(c)
    m, acc = jnp.full((tq, 1), -jnp.inf, F32), jnp.zeros((tq, 2 * dv), F32)
    for c in range(nk):
        ahead = c + depth - 1
        if ahead < nk:
            s_refs[ahead % depth][...] = scores(ahead)
        m, acc = fold(s_refs[c % depth], c, m, acc)
    o_ref[0] = (acc[:, :dv] / acc[:, dv:]).astype(o_ref.dtype)


def flash_attention(q, k, v, *, heads, tq, tk, cast_weights=()):
    b, s, _ = q.shape
    dk = q.shape[-1] // heads
    dv2 = v.shape[-1] // heads
    dv = dv2 // 2
    tq, tk = _tile(s, tq), _tile(s, min(tk, s // 2))
    nk = s // tk
    nq = s // tq
    steps = b * heads * nq
    wspecs = []
    for w in cast_weights:
        rows = w.shape[0] // steps
        assert rows * steps == w.shape[0] and rows % 16 == 0, (w.shape, steps)
        wspecs.append(pl.BlockSpec((rows, w.shape[1]), lambda bi, h, i: ((bi * heads + h) * nq + i, 0)))
    wspecs_in = wspecs_out = wspecs
    wshapes = [jax.ShapeDtypeStruct(w.shape, BF16) for w in cast_weights]
    outs = pl.pallas_call(
        functools.partial(_flash_kernel, tk=tk, nk=nk, ncast=len(cast_weights)),
        grid=(b, heads, nq),
        in_specs=[pl.BlockSpec((1, tq, dk), lambda bi, h, i: (bi, i, h)),
                  pl.BlockSpec((1, s, dk), lambda bi, h, i: (bi, 0, h)),
                  pl.BlockSpec((1, s, dv2), lambda bi, h, i: (bi, 0, h))] + wspecs_in,
        out_specs=[pl.BlockSpec((1, tq, dv), lambda bi, h, i: (bi, i, h))] + wspecs_out,
        out_shape=[jax.ShapeDtypeStruct((b, s, heads * dv), BF16)] + wshapes,
        scratch_shapes=[pltpu.VMEM((tq, tk), F32)] * FLASH_SCORE_BUFFERS,
        compiler_params=_params("parallel", "parallel", "arbitrary"),
        name="flash_attention",
    )(q, k, v, *cast_weights)
    return outs[0], outs[1:]


def _out_proj_kernel(yf_ref, a_ref, ga_ref, w_ref, x_ref, o_ref, ya_ref, yb_ref, *, slice_rows, n_slices):
    fw = yf_ref.shape[1]
    r, j = pl.program_id(0), pl.program_id(1)

    def norm_slice(dst_ref):
        start = jnp.minimum(j, n_slices - 1) * slice_rows
        for c in range(slice_rows // NORM_CHUNK_ROWS):
            src = pl.ds(c * NORM_CHUNK_ROWS, NORM_CHUNK_ROWS)
            dst = pl.ds(pl.multiple_of(start + c * NORM_CHUNK_ROWS, NORM_CHUNK_ROWS), NORM_CHUNK_ROWS)
            dst_ref[dst, :] = _rms(a_ref[src, :].astype(F32), ga_ref[...]).astype(BF16)

    def step(cur_ref, nxt_ref):
        norm_slice(nxt_ref)
        acc = jnp.dot(yf_ref[...], w_ref[:fw, :], preferred_element_type=F32)
        acc += jnp.dot(cur_ref[...], w_ref[fw:, :], preferred_element_type=F32)
        o_ref[...] = (acc + x_ref[...]).astype(o_ref.dtype)

    @pl.when(r == 0)
    def _():
        norm_slice(ya_ref)

    @pl.when(r % 2 == 1)
    def _():
        step(ya_ref, yb_ref)

    @pl.when((r > 0) & (r % 2 == 0))
    def _():
        step(yb_ref, ya_ref)


def out_proj(yf, a, ga, w, x, *, tm, tn):
    m, fw = yf.shape
    aw = a.shape[1]
    n = w.shape[1]
    tm, tn = _tile(m, tm), _tile(n, tn)
    ni, nj = m // tm, n // tn
    n_slices = 1
    while n_slices * 2 <= nj and (tm // (n_slices * 2)) % NORM_CHUNK_ROWS == 0:
        n_slices *= 2
    slice_rows = tm // n_slices
    first = lambda r, j: jnp.where(r == 0, 0, j)
    prev = lambda r: jnp.maximum(r - 1, 0)
    return pl.pallas_call(
        functools.partial(_out_proj_kernel, slice_rows=slice_rows, n_slices=n_slices),
        grid=(ni + 1, nj),
        in_specs=[pl.BlockSpec((tm, fw), lambda r, j: (prev(r), 0)),
                  pl.BlockSpec((slice_rows, aw),
                               lambda r, j: (jnp.minimum(r, ni - 1) * n_slices + jnp.minimum(j, n_slices - 1), 0)),
                  pl.BlockSpec((1, aw), lambda r, j: (0, 0)),
                  pl.BlockSpec((fw + aw, tn), lambda r, j: (0, first(r, j))),
                  pl.BlockSpec((tm, tn), lambda r, j: (prev(r), first(r, j)))],
        out_specs=pl.BlockSpec((tm, tn), lambda r, j: (prev(r), first(r, j))),
        out_shape=jax.ShapeDtypeStruct((m, n), F32),
        scratch_shapes=[pltpu.VMEM((tm, aw), BF16), pltpu.VMEM((tm, aw), BF16)],
        compiler_params=_params("arbitrary", "arbitrary"),
        name="out_proj",
    )(yf, a, ga.reshape(1, aw).astype(F32), w, x)


def _absorb_qk_kernel(wq_ref, k_ref, o_ref, *, scale):
    acc = lax.dot_general(wq_ref[...], k_ref[...], (((1,), (1,)), ((), ())), preferred_element_type=F32)
    o_ref[0] = (acc * scale).astype(o_ref.dtype)


def absorb_qk(wq, kv, *, batch, heads, mem, hd, scale):
    d = wq.shape[0]
    return pl.pallas_call(
        functools.partial(_absorb_qk_kernel, scale=scale),
        grid=(heads, batch),
        in_specs=[pl.BlockSpec((d, hd), lambda h, b: (0, h)),
                  pl.BlockSpec((mem, hd), lambda h, b: (b, h))],
        out_specs=pl.BlockSpec((1, d, mem), lambda h, b: (b, 0, h)),
        out_shape=jax.ShapeDtypeStruct((batch, d, heads * mem), BF16),
        compiler_params=_params("parallel", "parallel"),
        name="absorb_qk",
    )(wq, kv)


def _absorb_vo_kernel(v_ref, wo_ref, o_ref):
    o_ref[0] = jnp.dot(v_ref[...], wo_ref[...], preferred_element_type=F32).astype(o_ref.dtype)


def absorb_vo(kv, wo, *, batch, heads, mem, hd, tn):
    d = wo.shape[1]
    tn = _tile(d, tn)
    return pl.pallas_call(
        _absorb_vo_kernel,
        grid=(heads, d // tn, batch),
        in_specs=[pl.BlockSpec((mem, hd), lambda h, j, b: (b, h)),
                  pl.BlockSpec((hd, tn), lambda h, j, b: (h, j))],
        out_specs=pl.BlockSpec((1, mem, tn), lambda h, j, b: (b, h, j)),
        out_shape=jax.ShapeDtypeStruct((batch, heads * mem, d), BF16),
        compiler_params=_params("parallel", "parallel", "parallel"),
        name="absorb_vo",
    )(kv, wo)


def _xattn_kernel(hs_ref, hp_ref, g_ref, wqk_ref, vo_ref, g2_ref, o_ref, on_ref, hn_ref, pa_ref, pb_ref,
                  *, heads, mem, nt):
    r = pl.program_id(0)
    rows = hs_ref.shape[0]
    chunks = [pl.ds(c * NORM_CHUNK_ROWS, NORM_CHUNK_ROWS) for c in range(rows // NORM_CHUNK_ROWS)]

    def stage1(p_ref):
        for rc in chunks:
            hn_ref[rc, :] = _rms(hs_ref[rc, :], g_ref[...]).astype(BF16)
        s = jnp.dot(hn_ref[...], wqk_ref[0], preferred_element_type=F32)
        for h in range(heads):
            seg = s[:, h * mem:(h + 1) * mem]
            e = jnp.exp(seg - jnp.max(seg, axis=-1, keepdims=True))
            p_ref[:, h * mem:(h + 1) * mem] = (e / jnp.sum(e, axis=-1, keepdims=True)).astype(BF16)

    def stage2(p_ref):
        o_ref[...] = jnp.dot(p_ref[...], vo_ref[0], preferred_element_type=F32) + hp_ref[...]
        for rc in chunks:
            on_ref[rc, :] = _rms(o_ref[rc, :], g2_ref[...]).astype(on_ref.dtype)

    @pl.when(r == 0)
    def _():
        stage1(pa_ref)

    @pl.when((r > 0) & (r < nt) & (r % 2 == 1))
    def _():
        stage1(pb_ref)
        stage2(pa_ref)

    @pl.when((r > 0) & (r < nt) & (r % 2 == 0))
    def _():
        stage1(pa_ref)
        stage2(pb_ref)

    @pl.when(r == nt)
    def _():
        stage2(pb_ref if (nt - 1) % 2 else pa_ref)


def xattn(h, g, wqk, vo, g_next, *, seq, heads, mem, tm):
    m, d = h.shape
    tm = _tile(seq, tm)
    per_b = seq // tm
    hm = heads * mem
    nt = m // tm
    once = pl.Buffered(1)
    s1_tile = lambda r: jnp.minimum(r, nt - 1)
    s2_tile = lambda r: jnp.maximum(r - 1, 0)
    return pl.pallas_call(
        functools.partial(_xattn_kernel, heads=heads, mem=mem, nt=nt),
        grid=(nt + 1,),
        in_specs=[pl.BlockSpec((tm, d), lambda r: (s1_tile(r), 0)),
                  pl.BlockSpec((tm, d), lambda r: (s2_tile(r), 0)),
                  pl.BlockSpec((1, d), lambda r: (0, 0)),
                  pl.BlockSpec((1, d, hm), lambda r: (s1_tile(r) // per_b, 0, 0), pipeline_mode=once),
                  pl.BlockSpec((1, hm, d), lambda r: (s2_tile(r) // per_b, 0, 0), pipeline_mode=once),
                  pl.BlockSpec((1, d), lambda r: (0, 0))],
        out_specs=[pl.BlockSpec((tm, d), lambda r: (s2_tile(r), 0))] * 2,
        out_shape=[jax.ShapeDtypeStruct((m, d), F32), jax.ShapeDtypeStruct((m, d), BF16)],
        scratch_shapes=[pltpu.VMEM((tm, d), BF16), pltpu.VMEM((tm, hm), BF16), pltpu.VMEM((tm, hm), BF16)],
        compiler_params=_params("arbitrary"),
        name="xattn",
    )(h, h, g.reshape(1, d).astype(F32), wqk, vo, g_next.reshape(1, d).astype(F32))


def _pad_cols(w, n):
    return jnp.pad(w, ((0, 0), (0, n - w.shape[1])))


def kernel(x, mem, positions, g_mix, w_in, w_fourier, g_q_lora, w_uq, g_kv_lora, w_ukv, g_fourier_out, g_mla_out, w_out, g_xattn, g_mem, w_xq, w_xk, w_xv, w_xo, g_mlp, w_ff1, w_ff2, g_final):
    batch, seq, d = x.shape
    depth = g_mix.shape[0]
    groups, gdim = w_fourier.shape[1], w_fourier.shape[2]
    fw = groups * gdim
    q_rank, heads, qk_dim = w_uq.shape[1], w_uq.shape[2], w_uq.shape[3]
    kv_rank = w_ukv.shape[1]
    nope = LANES
    rope = qk_dim - nope
    vdim = w_ukv.shape[3] - nope
    assert rope == LANES // 2 and vdim == LANES and fw % q_rank == 0
    assert (fw + q_rank) % kv_rank == 0 and (fw + q_rank + kv_rank) % LANES == 0
    xheads, xhd = w_xq.shape[2], w_xq.shape[3]
    mtok = mem.shape[1]
    t = batch * seq
    s2 = FOURIER_SEQ_INNER
    s1 = seq // s2

    half = rope // 2
    inv_freq = ROPE_THETA ** (-jnp.arange(half, dtype=F32) / half)
    ang = positions.astype(F32).reshape(t, 1) * inv_freq
    rt = jnp.concatenate([jnp.cos(ang), jnp.sin(ang), jnp.zeros((t, LANES - 2 * half), F32)], axis=1)

    cs_rows, sel, tab1, m2 = _dft_tables(seq, gdim)

    in_tn = 6 * LANES
    in_width = w_in.shape[2]
    z_width = -(-(in_width + rope) // in_tn) * in_tn
    scale = float(qk_dim) ** -0.5 * math.log2(math.e)

    h = x.reshape(t, d)
    mem2 = mem.reshape(batch * mtok, d)
    for layer in range(depth):
        w_in_b = _pad_cols(w_in[layer].astype(BF16), z_width)
        wq = (jnp.pad(w_uq[layer], ((0, 0), (0, 0), (0, 2 * LANES - qk_dim))) * scale).reshape(
            q_rank, heads * 2 * LANES).astype(BF16)
        wkv = jnp.concatenate([w_ukv[layer][:, :, :nope].reshape(kv_rank, heads * nope),
                               w_ukv[layer][:, :, nope:].reshape(kv_rank, heads * vdim)], axis=1).astype(BF16)
        wcs = fold_fourier_weights(cs_rows, w_fourier[layer].astype(BF16))
        wxq = w_xq[layer].reshape(d, xheads * xhd).astype(BF16)
        wxk = w_xk[layer].reshape(d, xheads * xhd).astype(BF16)
        wxv = w_xv[layer].reshape(d, xheads * xhd).astype(BF16)

        z = norm_mm(h, g_mix[layer], w_in_b, tm=1024, tn=in_tn, out_dtype=BF16)
        a = seq_dft1(z.reshape(batch, s1, s2, z_width), sel, tab1, width=fw)
        yf = seq_dft2(a.reshape(batch, 2 * s1, s2, fw), m2, wcs, g_fourier_out[layer],
                      s1=s1, kb=4).reshape(t, fw)
        qh = q_proj(z, g_q_lora[layer], wq, rt, col_block=fw // q_rank, rank=q_rank,
                    heads=heads, tm=512)
        kh, vh = kv_proj(z, g_kv_lora[layer], wkv, rt, ckv_block=(fw + q_rank) // kv_rank,
                         kr_block=(fw + q_rank + kv_rank) // LANES, rank=kv_rank, heads=heads, tm=1024)
        o, (w_out_b, wxo, w1, w2) = flash_attention(
            qh.reshape(batch, seq, -1), kh.reshape(batch, seq, -1), vh.reshape(batch, seq, -1),
            heads=heads, tq=1024, tk=512,
            cast_weights=(w_out[layer], w_xo[layer].reshape(xheads * xhd, d), w_ff1[layer], w_ff2[layer]))
        o = o.reshape(t, heads * vdim)
        h = out_proj(yf, o, g_mla_out[layer], w_out_b, h, tm=1024, tn=512)

        km = norm_mm(mem2, g_mem[layer], wxk, tm=512, tn=512, out_dtype=BF16)
        vm = norm_mm(mem2, g_mem[layer], wxv, tm=512, tn=512, out_dtype=BF16)
        wqk = absorb_qk(wxq, km, batch=batch, heads=xheads, mem=mtok, hd=xhd, scale=float(xhd) ** -0.5)
        vo = absorb_vo(vm, wxo, batch=batch, heads=xheads, mem=mtok, hd=xhd, tn=2048)
        h, hn = xattn(h, g_xattn[layer], wqk, vo, g_mlp[layer], seq=seq, heads=xheads, mem=mtok, tm=256)

        act = mm_relu2(hn, w1, tm=1024, tn=1024, out_dtype=BF16)
        h = mm_res(act, w2, h, tm=1024, tn=1024, tk=4096)

    return rmsnorm(h, g_final, tm=256, out_dtype=x.dtype).reshape(batch, seq, d)
```

```python
import functools
import math

import numpy as np
import jax
import jax.numpy as jnp
from jax import lax
from jax.experimental import pallas as pl
from jax.experimental.pallas import tpu as pltpu

F32 = jnp.float32
BF16 = jnp.bfloat16

NORM_EPS = 1e-6
ROPE_THETA = 10000.0
LANES = 128
VMEM_LIMIT_BYTES = 56 * 1024 * 1024
FOURIER_SEQ_INNER = 128
FLASH_SCORE_BUFFERS = 2
SEQ_DFT1_POSITIONS = 16


def _params(*sem):
    return pltpu.CompilerParams(dimension_semantics=sem, vmem_limit_bytes=VMEM_LIMIT_BYTES)


def _rms(xf, g):
    ms = jnp.mean(xf * xf, axis=-1, keepdims=True)
    return xf * lax.rsqrt(ms + NORM_EPS) * g


def _rms_rows(a_ref, g_ref, o_ref, chunk=64):
    rows = a_ref.shape[0]
    chunk = min(chunk, rows)

    def body(c, carry):
        r = pl.ds(pl.multiple_of(c * chunk, chunk), chunk)
        o_ref[r, :] = _rms(a_ref[r, :].astype(F32), g_ref[...]).astype(o_ref.dtype)
        return carry

    lax.fori_loop(0, rows // chunk, body, 0)


def _tile(dim, pref):
    t = min(dim, pref)
    assert dim % t == 0, (dim, pref)
    return t


NORM_CHUNK_ROWS = 32


def _norm_mm_kernel(a_ref, g_ref, w_ref, o_ref, xa_ref, xb_ref, *, slice_rows, n_slices):
    r, j = pl.program_id(0), pl.program_id(1)

    def norm_slice(dst_ref):
        start = jnp.minimum(j, n_slices - 1) * slice_rows
        for c in range(slice_rows // NORM_CHUNK_ROWS):
            src = pl.ds(c * NORM_CHUNK_ROWS, NORM_CHUNK_ROWS)
            dst = pl.ds(pl.multiple_of(start + c * NORM_CHUNK_ROWS, NORM_CHUNK_ROWS), NORM_CHUNK_ROWS)
            dst_ref[dst, :] = _rms(a_ref[src, :].astype(F32), g_ref[...]).astype(BF16)

    def step(cur_ref, nxt_ref):
        norm_slice(nxt_ref)
        o_ref[...] = jnp.dot(cur_ref[...], w_ref[...], preferred_element_type=F32).astype(o_ref.dtype)

    @pl.when(r == 0)
    def _():
        norm_slice(xa_ref)

    @pl.when(r % 2 == 1)
    def _():
        step(xa_ref, xb_ref)

    @pl.when((r > 0) & (r % 2 == 0))
    def _():
        step(xb_ref, xa_ref)


def norm_mm(a, g, w, *, tm, tn, out_dtype):
    m, k = a.shape
    n = w.shape[1]
    tm, tn = _tile(m, tm), _tile(n, tn)
    ni, nj = m // tm, n // tn
    n_slices = 1
    while n_slices * 2 <= nj and (tm // (n_slices * 2)) % NORM_CHUNK_ROWS == 0:
        n_slices *= 2
    slice_rows = tm // n_slices
    first = lambda r, j: jnp.where(r == 0, 0, j)
    return pl.pallas_call(
        functools.partial(_norm_mm_kernel, slice_rows=slice_rows, n_slices=n_slices),
        grid=(ni + 1, nj),
        in_specs=[pl.BlockSpec((slice_rows, k),
                               lambda r, j: (jnp.minimum(r, ni - 1) * n_slices + jnp.minimum(j, n_slices - 1), 0)),
                  pl.BlockSpec((1, k), lambda r, j: (0, 0)),
                  pl.BlockSpec((k, tn), lambda r, j: (0, first(r, j)))],
        out_specs=pl.BlockSpec((tm, tn), lambda r, j: (jnp.maximum(r - 1, 0), first(r, j))),
        out_shape=jax.ShapeDtypeStruct((m, n), out_dtype),
        scratch_shapes=[pltpu.VMEM((tm, k), BF16), pltpu.VMEM((tm, k), BF16)],
        compiler_params=_params("arbitrary", "arbitrary"),
        name="norm_mm",
    )(a, g.reshape(1, k).astype(F32), w)


def _mm_relu2_kernel(a_ref, w_ref, o_ref):
    acc = jnp.dot(a_ref[...], w_ref[...], preferred_element_type=F32)
    o_ref[...] = jnp.square(jnp.maximum(acc, 0.0)).astype(o_ref.dtype)


def mm_relu2(a, w, *, tm, tn, out_dtype):
    m, k = a.shape
    n = w.shape[1]
    tm, tn = _tile(m, tm), _tile(n, tn)
    return pl.pallas_call(
        _mm_relu2_kernel,
        grid=(m // tm, n // tn),
        in_specs=[pl.BlockSpec((tm, k), lambda i, j: (i, 0)),
                  pl.BlockSpec((k, tn), lambda i, j: (0, j))],
        out_specs=pl.BlockSpec((tm, tn), lambda i, j: (i, j)),
        out_shape=jax.ShapeDtypeStruct((m, n), out_dtype),
        compiler_params=_params("parallel", "parallel"),
        name="mm_relu2",
    )(a, w)


def _mm_res_kernel(a_ref, w_ref, res_ref, o_ref):
    @pl.when(pl.program_id(2) == 0)
    def _():
        o_ref[...] = res_ref[...]

    o_ref[...] += jnp.dot(a_ref[...], w_ref[...], preferred_element_type=F32)


def mm_res(a, w, res, *, tm, tn, tk):
    m, k = a.shape
    n = w.shape[1]
    tm, tn, tk = _tile(m, tm), _tile(n, tn), _tile(k, tk)
    return pl.pallas_call(
        _mm_res_kernel,
        grid=(m // tm, n // tn, k // tk),
        in_specs=[pl.BlockSpec((tm, tk), lambda i, j, kk: (i, kk)),
                  pl.BlockSpec((tk, tn), lambda i, j, kk: (kk, j)),
                  pl.BlockSpec((tm, tn), lambda i, j, kk: (i, j))],
        out_specs=pl.BlockSpec((tm, tn), lambda i, j, kk: (i, j)),
        out_shape=jax.ShapeDtypeStruct((m, n), F32),
        compiler_params=_params("parallel", "parallel", "arbitrary"),
        name="mm_res",
    )(a, w, res)


def _rmsnorm_kernel(a_ref, g_ref, o_ref):
    _rms_rows(a_ref, g_ref, o_ref)


def rmsnorm(a, g, *, tm, out_dtype):
    m, k = a.shape
    tm = _tile(m, tm)
    return pl.pallas_call(
        _rmsnorm_kernel,
        grid=(m // tm,),
        in_specs=[pl.BlockSpec((tm, k), lambda i: (i, 0)), pl.BlockSpec((1, k), lambda i: (0, 0))],
        out_specs=pl.BlockSpec((tm, k), lambda i: (i, 0)),
        out_shape=jax.ShapeDtypeStruct((m, k), out_dtype),
        compiler_params=_params("parallel"),
        name="rmsnorm",
    )(a, g.reshape(1, k).astype(F32))


def _fold_fourier_kernel(cs_ref, wf_ref, o_ref):
    o_ref[0] = jnp.dot(cs_ref[...], wf_ref[0], preferred_element_type=F32).astype(o_ref.dtype)


def fold_fourier_weights(cs_rows, wf):
    groups, gdim, _ = wf.shape
    return pl.pallas_call(
        _fold_fourier_kernel,
        grid=(groups,),
        in_specs=[pl.BlockSpec((2 * gdim, gdim), lambda g: (0, 0)),
                  pl.BlockSpec((1, gdim, gdim), lambda g: (g, 0, 0))],
        out_specs=pl.BlockSpec((1, 2 * gdim, gdim), lambda g: (g, 0, 0)),
        out_shape=jax.ShapeDtypeStruct((groups, 2 * gdim, gdim), BF16),
        compiler_params=_params("parallel"),
        name="fold_fourier_weights",
    )(cs_rows, wf)


def _seq_dft1_kernel(z_ref, sel_ref, t_ref, o_ref, *, nb, width):
    s1 = z_ref.shape[0]
    zz = z_ref[...].reshape(s1 * nb, width)
    xx = jnp.dot(sel_ref[...], zz, preferred_element_type=F32).astype(BF16)
    for t in range(nb):
        acc = jnp.dot(t_ref[t], xx[t * s1:(t + 1) * s1], preferred_element_type=F32)
        o_ref[0, :, t * width:(t + 1) * width] = acc.astype(o_ref.dtype)


def seq_dft1(z4, sel, tab, *, width):
    b, s1, s2, _ = z4.shape
    nb = sel.shape[0] // s1
    assert s2 % nb == 0
    return pl.pallas_call(
        functools.partial(_seq_dft1_kernel, nb=nb, width=width),
        grid=(b, s2 // nb),
        in_specs=[pl.BlockSpec((None, s1, nb, width), lambda bi, j: (bi, 0, j, 0)),
                  pl.BlockSpec(sel.shape, lambda bi, j: (0, 0)),
                  pl.BlockSpec((nb, 2 * s1, s1), lambda bi, j: (j, 0, 0))],
        out_specs=pl.BlockSpec((1, 2 * s1, nb * width), lambda bi, j: (bi, 0, j)),
        out_shape=jax.ShapeDtypeStruct((b, 2 * s1, s2 * width), BF16),
        compiler_params=_params("parallel", "parallel"),
        name="seq_dft1",
    )(z4, sel, tab)


def _seq_dft2_kernel(re_ref, im_ref, m2_ref, wcs_ref, gf_ref, o_ref, *, kb, width):
    groups, _, gdim = wcs_ref.shape
    s2 = re_ref.shape[2]
    for t in range(kb):
        a = jnp.concatenate([re_ref[0, t], im_ref[0, t]], axis=0)
        y = jnp.dot(m2_ref[...], a, preferred_element_type=F32).astype(BF16)
        y_re, y_im = y[:s2], y[s2:]
        parts = []
        for g in range(groups):
            sl = slice(g * gdim, (g + 1) * gdim)
            lhs = jnp.concatenate([y_re[:, sl], y_im[:, sl]], axis=1)
            parts.append(jnp.dot(lhs, wcs_ref[g], preferred_element_type=F32))
        ms = sum(jnp.sum(p * p, axis=-1, keepdims=True) for p in parts) / width
        inv = lax.rsqrt(ms + NORM_EPS)
        for g in range(groups):
            sl = slice(g * gdim, (g + 1) * gdim)
            o_ref[0, :, t * width + g * gdim:t * width + (g + 1) * gdim] = (
                parts[g] * inv * gf_ref[:, sl]).astype(o_ref.dtype)


def seq_dft2(a4, m2, wcs, gf, *, s1, kb):
    b, _, s2, width = a4.shape
    kb = _tile(s1, kb)
    nblk = s1 // kb
    return pl.pallas_call(
        functools.partial(_seq_dft2_kernel, kb=kb, width=width),
        grid=(b, nblk),
        in_specs=[
            pl.BlockSpec((1, kb, s2, width), lambda bi, j: (bi, j, 0, 0)),
            pl.BlockSpec((1, kb, s2, width), lambda bi, j: (bi, j + nblk, 0, 0)),
            pl.BlockSpec((2 * s2, 2 * s2), lambda bi, j: (0, 0)),
            pl.BlockSpec(wcs.shape, lambda bi, j: (0, 0, 0)),
            pl.BlockSpec((1, width), lambda bi, j: (0, 0)),
        ],
        out_specs=pl.BlockSpec((1, s2, kb * width), lambda bi, j: (bi, 0, j)),
        out_shape=jax.ShapeDtypeStruct((b, s2, s1 * width), BF16),
        compiler_params=_params("parallel", "parallel"),
        name="seq_dft2",
    )(a4, a4, m2, wcs, gf.reshape(1, width).astype(F32))


def _dft_tables(seq, gdim):
    s2 = FOURIER_SEQ_INNER
    s1 = seq // s2
    c = np.arange(gdim)
    ang = 2.0 * np.pi * ((c[:, None] * c[None, :]) % gdim) / gdim
    cs_rows = np.concatenate([np.cos(ang), np.sin(ang)], axis=0) / math.sqrt(gdim)
    k1 = np.arange(s1)
    n1 = np.arange(s1)
    n2 = np.arange(s2)
    idx = (k1[None, :, None] * (s2 * n1[None, None, :] + n2[:, None, None])) % seq
    phi = 2.0 * np.pi * idx / seq
    tab1 = np.concatenate([np.cos(phi), -np.sin(phi)], axis=1)
    nb = SEQ_DFT1_POSITIONS
    sel = np.zeros((nb, s1, s1 * nb))
    for tt in range(nb):
        sel[tt, n1, n1 * nb + tt] = 1.0
    sel = sel.reshape(nb * s1, s1 * nb)
    k2 = np.arange(s2)
    ang2 = 2.0 * np.pi * ((k2[:, None] * n2[None, :]) % s2) / s2
    c2 = np.cos(ang2) / math.sqrt(seq)
    s2m = np.sin(ang2) / math.sqrt(seq)
    m2 = np.block([[c2, s2m], [-s2m, c2]])
    as_bf16 = lambda x: jnp.asarray(x, dtype=F32).astype(BF16)
    return as_bf16(cs_rows), as_bf16(sel), as_bf16(tab1), as_bf16(m2)


def _rope_tables(rt):
    quarter = LANES // 4
    lane = lax.broadcasted_iota(jnp.int32, rt.shape, 1)
    q0 = lane < quarter
    q1 = (lane >= quarter) & (lane < 2 * quarter)
    ct = jnp.where(q0, rt, 0.0) + jnp.where(q1, pltpu.roll(rt, quarter, 1), 0.0)
    sa = jnp.where(q1, rt, 0.0)
    sb = jnp.where(q0, -pltpu.roll(rt, LANES - quarter, 1), 0.0)
    return ct, sa, sb


def _rope128(r, ct, sa, sb):
    quarter = LANES // 4
    return r * ct + pltpu.roll(r, quarter, 1) * sa + pltpu.roll(r, LANES - quarter, 1) * sb


def _q_proj_kernel(c_ref, g_ref, w_ref, rt_ref, o_ref, *, heads):
    xn = _rms(c_ref[...].astype(F32), g_ref[...]).astype(BF16)
    acc = jnp.dot(xn, w_ref[...], preferred_element_type=F32)
    ct, sa, sb = _rope_tables(rt_ref[...])
    for h in range(heads):
        base = 2 * LANES * h
        o_ref[:, base:base + LANES] = acc[:, base:base + LANES].astype(o_ref.dtype)
        r = _rope128(acc[:, base + LANES:base + 2 * LANES], ct, sa, sb)
        o_ref[:, base + LANES:base + 2 * LANES] = r.astype(o_ref.dtype)


def q_proj(z, g, w, rt, *, col_block, rank, heads, tm):
    m = z.shape[0]
    tm = _tile(m, tm)
    n = w.shape[1]
    return pl.pallas_call(
        functools.partial(_q_proj_kernel, heads=heads),
        grid=(m // tm,),
        in_specs=[pl.BlockSpec((tm, rank), lambda i: (i, col_block)),
                  pl.BlockSpec((1, rank), lambda i: (0, 0)),
                  pl.BlockSpec((rank, n), lambda i: (0, 0)),
                  pl.BlockSpec((tm, LANES), lambda i: (i, 0))],
        out_specs=pl.BlockSpec((tm, n), lambda i: (i, 0)),
        out_shape=jax.ShapeDtypeStruct((m, n), BF16),
        compiler_params=_params("parallel"),
        name="q_proj",
    )(z, g.reshape(1, rank).astype(F32), w, rt)


def _kv_proj_kernel(c_ref, kr_ref, g_ref, w_ref, rt_ref, k_ref, v_ref, *, heads):
    xn = _rms(c_ref[...].astype(F32), g_ref[...]).astype(BF16)
    acc = jnp.dot(xn, w_ref[...], preferred_element_type=F32)
    kr = _rope128(kr_ref[...].astype(F32), *_rope_tables(rt_ref[...])).astype(k_ref.dtype)
    for h in range(heads):
        k_ref[:, 2 * LANES * h:2 * LANES * h + LANES] = acc[:, LANES * h:LANES * (h + 1)].astype(k_ref.dtype)
        k_ref[:, 2 * LANES * h + LANES:2 * LANES * (h + 1)] = kr
        v_ref[:, 2 * LANES * h:2 * LANES * h + LANES] = acc[:, LANES * (heads + h):LANES * (heads + h + 1)].astype(v_ref.dtype)
        v_ref[:, 2 * LANES * h + LANES:2 * LANES * (h + 1)] = jnp.ones((acc.shape[0], LANES), v_ref.dtype)


def kv_proj(z, g, w, rt, *, ckv_block, kr_block, rank, heads, tm):
    m = z.shape[0]
    tm = _tile(m, tm)
    return pl.pallas_call(
        functools.partial(_kv_proj_kernel, heads=heads),
        grid=(m // tm,),
        in_specs=[pl.BlockSpec((tm, rank), lambda i: (i, ckv_block)),
                  pl.BlockSpec((tm, LANES), lambda i: (i, kr_block)),
                  pl.BlockSpec((1, rank), lambda i: (0, 0)),
                  pl.BlockSpec((rank, 2 * LANES * heads), lambda i: (0, 0)),
                  pl.BlockSpec((tm, LANES), lambda i: (i, 0))],
        out_specs=[pl.BlockSpec((tm, 2 * LANES * heads), lambda i: (i, 0))] * 2,
        out_shape=[jax.ShapeDtypeStruct((m, 2 * LANES * heads), BF16)] * 2,
        compiler_params=_params("parallel"),
        name="kv_proj",
    )(z, z, g.reshape(1, rank).astype(F32), w, rt)


def _flash_kernel(q_ref, k_ref, v_ref, *rest, tk, nk, ncast):
    cast_in, o_ref, cast_out = rest[:ncast], rest[ncast], rest[ncast + 1:2 * ncast + 1]
    s_refs = rest[2 * ncast + 1:]
    for src, dst in zip(cast_in, cast_out):
        dst[...] = src[...].astype(dst.dtype)
    q = q_ref[0]
    tq = q.shape[0]
    dv = o_ref.shape[-1]

    def chunk(c):
        start = c * tk
        return pl.ds(start if isinstance(start, int) else pl.multiple_of(start, tk), tk)

    def scores(c):
        return lax.dot_general(q, k_ref[0, chunk(c), :], (((1,), (1,)), ((), ())), preferred_element_type=F32)

    def fold(s_ref, c, m, acc):
        s = s_ref[...]
        m_new = jnp.maximum(m, jnp.max(s, axis=1, keepdims=True))
        p = jnp.exp2(s - m_new).astype(BF16)
        acc = jnp.exp2(m - m_new) * acc + jnp.dot(p, v_ref[0, chunk(c), :], preferred_element_type=F32)
        return m_new, acc

    depth = len(s_refs)
    for c in range(min(depth - 1, nk)):
        s_refs[c][...] = scores(c)
    m, acc = jnp.full((tq, 1), -jnp.inf, F32), jnp.zeros((tq, 2 * dv), F32)
    for c in range(nk):
        ahead = c + depth - 1
        if ahead < nk:
            s_refs[ahead % depth][...] = scores(ahead)
        m, acc = fold(s_refs[c % depth], c, m, acc)
    o_ref[0] = (acc[:, :dv] / acc[:, dv:]).astype(o_ref.dtype)


def flash_attention(q, k, v, *, heads, tq, tk, cast_weights=()):
    b, s, _ = q.shape
    dk = q.shape[-1] // heads
    dv2 = v.shape[-1] // heads
    dv = dv2 // 2
    tq, tk = _tile(s, tq), _tile(s, min(tk, s // 2))
    nk = s // tk
    nq = s // tq
    steps = b * heads * nq
    wspecs = []
    for w in cast_weights:
        rows = w.shape[0] // steps
        assert rows * steps == w.shape[0] and rows % 16 == 0, (w.shape, steps)
        wspecs.append(pl.BlockSpec((rows, w.shape[1]), lambda bi, h, i: ((bi * heads + h) * nq + i, 0)))
    wspecs_in = wspecs_out = wspecs
    wshapes = [jax.ShapeDtypeStruct(w.shape, BF16) for w in cast_weights]
    outs = pl.pallas_call(
        functools.partial(_flash_kernel, tk=tk, nk=nk, ncast=len(cast_weights)),
        grid=(b, heads, nq),
        in_specs=[pl.BlockSpec((1, tq, dk), lambda bi, h, i: (bi, i, h)),
                  pl.BlockSpec((1, s, dk), lambda bi, h, i: (bi, 0, h)),
                  pl.BlockSpec((1, s, dv2), lambda bi, h, i: (bi, 0, h))] + wspecs_in,
        out_specs=[pl.BlockSpec((1, tq, dv), lambda bi, h, i: (bi, i, h))] + wspecs_out,
        out_shape=[jax.ShapeDtypeStruct((b, s, heads * dv), BF16)] + wshapes,
        scratch_shapes=[pltpu.VMEM((tq, tk), F32)] * FLASH_SCORE_BUFFERS,
        compiler_params=_params("parallel", "parallel", "arbitrary"),
        name="flash_attention",
    )(q, k, v, *cast_weights)
    return outs[0], outs[1:]


def _out_proj_kernel(yf_ref, a_ref, ga_ref, w_ref, x_ref, o_ref, ya_ref, yb_ref, *, slice_rows, n_slices):
    fw = yf_ref.shape[1]
    r, j = pl.program_id(0), pl.program_id(1)

    def norm_slice(dst_ref):
        start = jnp.minimum(j, n_slices - 1) * slice_rows
        for c in range(slice_rows // NORM_CHUNK_ROWS):
            src = pl.ds(c * NORM_CHUNK_ROWS, NORM_CHUNK_ROWS)
            dst = pl.ds(pl.multiple_of(start + c * NORM_CHUNK_ROWS, NORM_CHUNK_ROWS), NORM_CHUNK_ROWS)
            dst_ref[dst, :] = _rms(a_ref[src, :].astype(F32), ga_ref[...]).astype(BF16)

    def step(cur_ref, nxt_ref):
        norm_slice(nxt_ref)
        acc = jnp.dot(yf_ref[...], w_ref[:fw, :], preferred_element_type=F32)
        acc += jnp.dot(cur_ref[...], w_ref[fw:, :], preferred_element_type=F32)
        o_ref[...] = (acc + x_ref[...]).astype(o_ref.dtype)

    @pl.when(r == 0)
    def _():
        norm_slice(ya_ref)

    @pl.when(r % 2 == 1)
    def _():
        step(ya_ref, yb_ref)

    @pl.when((r > 0) & (r % 2 == 0))
    def _():
        step(yb_ref, ya_ref)


def out_proj(yf, a, ga, w, x, *, tm, tn):
    m, fw = yf.shape
    aw = a.shape[1]
    n = w.shape[1]
    tm, tn = _tile(m, tm), _tile(n, tn)
    ni, nj = m // tm, n // tn
    n_slices = 1
    while n_slices * 2 <= nj and (tm // (n_slices * 2)) % NORM_CHUNK_ROWS == 0:
        n_slices *= 2
    slice_rows = tm // n_slices
    first = lambda r, j: jnp.where(r == 0, 0, j)
    prev = lambda r: jnp.maximum(r - 1, 0)
    return pl.pallas_call(
        functools.partial(_out_proj_kernel, slice_rows=slice_rows, n_slices=n_slices),
        grid=(ni + 1, nj),
        in_specs=[pl.BlockSpec((tm, fw), lambda r, j: (prev(r), 0)),
                  pl.BlockSpec((slice_rows, aw),
                               lambda r, j: (jnp.minimum(r, ni - 1) * n_slices + jnp.minimum(j, n_slices - 1), 0)),
                  pl.BlockSpec((1, aw), lambda r, j: (0, 0)),
                  pl.BlockSpec((fw + aw, tn), lambda r, j: (0, first(r, j))),
                  pl.BlockSpec((tm, tn), lambda r, j: (prev(r), first(r, j)))],
        out_specs=pl.BlockSpec((tm, tn), lambda r, j: (prev(r), first(r, j))),
        out_shape=jax.ShapeDtypeStruct((m, n), F32),
        scratch_shapes=[pltpu.VMEM((tm, aw), BF16), pltpu.VMEM((tm, aw), BF16)],
        compiler_params=_params("arbitrary", "arbitrary"),
        name="out_proj",
    )(yf, a, ga.reshape(1, aw).astype(F32), w, x)


def _absorb_qk_kernel(wq_ref, k_ref, o_ref, *, scale):
    acc = lax.dot_general(wq_ref[...], k_ref[...], (((1,), (1,)), ((), ())), preferred_element_type=F32)
    o_ref[0] = (acc * scale).astype(o_ref.dtype)


def absorb_qk(wq, kv, *, batch, heads, mem, hd, scale):
    d = wq.shape[0]
    return pl.pallas_call(
        functools.partial(_absorb_qk_kernel, scale=scale),
        grid=(heads, batch),
        in_specs=[pl.BlockSpec((d, hd), lambda h, b: (0, h)),
                  pl.BlockSpec((mem, hd), lambda h, b: (b, h))],
        out_specs=pl.BlockSpec((1, d, mem), lambda h, b: (b, 0, h)),
        out_shape=jax.ShapeDtypeStruct((batch, d, heads * mem), BF16),
        compiler_params=_params("parallel", "parallel"),
        name="absorb_qk",
    )(wq, kv)


def _absorb_vo_kernel(v_ref, wo_ref, o_ref):
    o_ref[0] = jnp.dot(v_ref[...], wo_ref[...], preferred_element_type=F32).astype(o_ref.dtype)


def absorb_vo(kv, wo, *, batch, heads, mem, hd, tn):
    d = wo.shape[1]
    tn = _tile(d, tn)
    return pl.pallas_call(
        _absorb_vo_kernel,
        grid=(heads, d // tn, batch),
        in_specs=[pl.BlockSpec((mem, hd), lambda h, j, b: (b, h)),
                  pl.BlockSpec((hd, tn), lambda h, j, b: (h, j))],
        out_specs=pl.BlockSpec((1, mem, tn), lambda h, j, b: (b, h, j)),
        out_shape=jax.ShapeDtypeStruct((batch, heads * mem, d), BF16),
        compiler_params=_params("parallel", "parallel", "parallel"),
        name="absorb_vo",
    )(kv, wo)


def _xattn_kernel(hs_ref, hp_ref, g_ref, wqk_ref, vo_ref, g2_ref, o_ref, on_ref, hn_ref, pa_ref, pb_ref,
                  *, heads, mem, nt):
    r = pl.program_id(0)
    rows = hs_ref.shape[0]
    chunks = [pl.ds(c * NORM_CHUNK_ROWS, NORM_CHUNK_ROWS) for c in range(rows // NORM_CHUNK_ROWS)]

    def stage1(p_ref):
        for rc in chunks:
            hn_ref[rc, :] = _rms(hs_ref[rc, :], g_ref[...]).astype(BF16)
        s = jnp.dot(hn_ref[...], wqk_ref[0], preferred_element_type=F32)
        for h in range(heads):
            seg = s[:, h * mem:(h + 1) * mem]
            e = jnp.exp(seg - jnp.max(seg, axis=-1, keepdims=True))
            p_ref[:, h * mem:(h + 1) * mem] = (e / jnp.sum(e, axis=-1, keepdims=True)).astype(BF16)

    def stage2(p_ref):
        o_ref[...] = jnp.dot(p_ref[...], vo_ref[0], preferred_element_type=F32) + hp_ref[...]
        for rc in chunks:
            on_ref[rc, :] = _rms(o_ref[rc, :], g2_ref[...]).astype(on_ref.dtype)

    @pl.when(r == 0)
    def _():
        stage1(pa_ref)

    @pl.when((r > 0) & (r < nt) & (r % 2 == 1))
    def _():
        stage1(pb_ref)
        stage2(pa_ref)

    @pl.when((r > 0) & (r < nt) & (r % 2 == 0))
    def _():
        stage1(pa_ref)
        stage2(pb_ref)

    @pl.when(r == nt)
    def _():
        stage2(pb_ref if (nt - 1) % 2 else pa_ref)


def xattn(h, g, wqk, vo, g_next, *, seq, heads, mem, tm):
    m, d = h.shape
    tm = _tile(seq, tm)
    per_b = seq // tm
    hm = heads * mem
    nt = m // tm
    once = pl.Buffered(1)
    s1_tile = lambda r: jnp.minimum(r, nt - 1)
    s2_tile = lambda r: jnp.maximum(r - 1, 0)
    return pl.pallas_call(
        functools.partial(_xattn_kernel, heads=heads, mem=mem, nt=nt),
        grid=(nt + 1,),
        in_specs=[pl.BlockSpec((tm, d), lambda r: (s1_tile(r), 0)),
                  pl.BlockSpec((tm, d), lambda r: (s2_tile(r), 0)),
                  pl.BlockSpec((1, d), lambda r: (0, 0)),
                  pl.BlockSpec((1, d, hm), lambda r: (s1_tile(r) // per_b, 0, 0), pipeline_mode=once),
                  pl.BlockSpec((1, hm, d), lambda r: (s2_tile(r) // per_b, 0, 0), pipeline_mode=once),
                  pl.BlockSpec((1, d), lambda r: (0, 0))],
        out_specs=[pl.BlockSpec((tm, d), lambda r: (s2_tile(r), 0))] * 2,
        out_shape=[jax.ShapeDtypeStruct((m, d), F32), jax.ShapeDtypeStruct((m, d), BF16)],
        scratch_shapes=[pltpu.VMEM((tm, d), BF16), pltpu.VMEM((tm, hm), BF16), pltpu.VMEM((tm, hm), BF16)],
        compiler_params=_params("arbitrary"),
        name="xattn",
    )(h, h, g.reshape(1, d).astype(F32), wqk, vo, g_next.reshape(1, d).astype(F32))


def _pad_cols(w, n):
    return jnp.pad(w, ((0, 0), (0, n - w.shape[1])))


def kernel(x, mem, positions, g_mix, w_in, w_fourier, g_q_lora, w_uq, g_kv_lora, w_ukv, g_fourier_out, g_mla_out, w_out, g_xattn, g_mem, w_xq, w_xk, w_xv, w_xo, g_mlp, w_ff1, w_ff2, g_final):
    batch, seq, d = x.shape
    depth = g_mix.shape[0]
    groups, gdim = w_fourier.shape[1], w_fourier.shape[2]
    fw = groups * gdim
    q_rank, heads, qk_dim = w_uq.shape[1], w_uq.shape[2], w_uq.shape[3]
    kv_rank = w_ukv.shape[1]
    nope = LANES
    rope = qk_dim - nope
    vdim = w_ukv.shape[3] - nope
    assert rope == LANES // 2 and vdim == LANES and fw % q_rank == 0
    assert (fw + q_rank) % kv_rank == 0 and (fw + q_rank + kv_rank) % LANES == 0
    xheads, xhd = w_xq.shape[2], w_xq.shape[3]
    mtok = mem.shape[1]
    t = batch * seq
    s2 = FOURIER_SEQ_INNER
    s1 = seq // s2

    half = rope // 2
    inv_freq = ROPE_THETA ** (-jnp.arange(half, dtype=F32) / half)
    ang = positions.astype(F32).reshape(t, 1) * inv_freq
    rt = jnp.concatenate([jnp.cos(ang), jnp.sin(ang), jnp.zeros((t, LANES - 2 * half), F32)], axis=1)

    cs_rows, sel, tab1, m2 = _dft_tables(seq, gdim)

    in_tn = 6 * LANES
    in_width = w_in.shape[2]
    z_width = -(-(in_width + rope) // in_tn) * in_tn
    scale = float(qk_dim) ** -0.5 * math.log2(math.e)

    h = x.reshape(t, d)
    mem2 = mem.reshape(batch * mtok, d)
    for layer in range(depth):
        w_in_b = _pad_cols(w_in[layer].astype(BF16), z_width)
        wq = (jnp.pad(w_uq[layer], ((0, 0), (0, 0), (0, 2 * LANES - qk_dim))) * scale).reshape(
            q_rank, heads * 2 * LANES).astype(BF16)
        wkv = jnp.concatenate([w_ukv[layer][:, :, :nope].reshape(kv_rank, heads * nope),
                               w_ukv[layer][:, :, nope:].reshape(kv_rank, heads * vdim)], axis=1).astype(BF16)
        wcs = fold_fourier_weights(cs_rows, w_fourier[layer].astype(BF16))
        wxq = w_xq[layer].reshape(d, xheads * xhd).astype(BF16)
        wxk = w_xk[layer].reshape(d, xheads * xhd).astype(BF16)
        wxv = w_xv[layer].reshape(d, xheads * xhd).astype(BF16)

        z = norm_mm(h, g_mix[layer], w_in_b, tm=1024, tn=in_tn, out_dtype=BF16)
        a = seq_dft1(z.reshape(batch, s1, s2, z_width), sel, tab1, width=fw)
        yf = seq_dft2(a.reshape(batch, 2 * s1, s2, fw), m2, wcs, g_fourier_out[layer],
                      s1=s1, kb=4).reshape(t, fw)
        qh = q_proj(z, g_q_lora[layer], wq, rt, col_block=fw // q_rank, rank=q_rank,
                    heads=heads, tm=512)
        kh, vh = kv_proj(z, g_kv_lora[layer], wkv, rt, ckv_block=(fw + q_rank) // kv_rank,
                         kr_block=(fw + q_rank + kv_rank) // LANES, rank=kv_rank, heads=heads, tm=1024)
        o, (w_out_b, wxo, w1, w2) = flash_attention(
            qh.reshape(batch, seq, -1), kh.reshape(batch, seq, -1), vh.reshape(batch, seq, -1),
            heads=heads, tq=1024, tk=512,
            cast_weights=(w_out[layer], w_xo[layer].reshape(xheads * xhd, d), w_ff1[layer], w_ff2[layer]))
        o = o.reshape(t, heads * vdim)
        h = out_proj(yf, o, g_mla_out[layer], w_out_b, h, tm=1024, tn=512)

        km = norm_mm(mem2, g_mem[layer], wxk, tm=512, tn=512, out_dtype=BF16)
        vm = norm_mm(mem2, g_mem[layer], wxv, tm=512, tn=512, out_dtype=BF16)
        wqk = absorb_qk(wxq, km, batch=batch, heads=xheads, mem=mtok, hd=xhd, scale=float(xhd) ** -0.5)
        vo = absorb_vo(vm, wxo, batch=batch, heads=xheads, mem=mtok, hd=xhd, tn=2048)
        h, hn = xattn(h, g_xattn[layer], wqk, vo, g_mlp[layer], seq=seq, heads=xheads, mem=mtok, tm=256)

        act = mm_relu2(hn, w1, tm=2048, tn=512, out_dtype=BF16)
        h = mm_res(act, w2, h, tm=1024, tn=1024, tk=4096)

    return rmsnorm(h, g_final, tm=256, out_dtype=x.dtype).reshape(batch, seq, d)
```
